```python
import math
import jax, jax.numpy as jnp
from jax import lax
import numpy as np

D_MODEL = 1024
BATCH = 8
SEQ = 2048
DEPTH = 1
DEC_BATCH = 128
DEC_SEQ = 4
PAST_LEN = 16384
PAGE_SIZE = 128

N_META = 16
D_CONF = D_MODEL
CONF_K = 31
D_INNER = 2 * D_MODEL
SSM_HEAD_DIM = 64
SSM_HEADS = D_INNER // SSM_HEAD_DIM
SSM_GROUPS = 8
SSM_STATE = 128
SSM_K = 4
CONV_DIM = D_INNER + 2 * SSM_GROUPS * SSM_STATE
SSD_CHUNK = 128
SSM_NORM_GROUPS = SSM_GROUPS
N_GATES = 2
N_IN = 2 * D_CONF + D_INNER + CONV_DIM + SSM_HEADS + N_GATES * D_MODEL
N_EXPERT_GROUPS = 4
EXPERTS_PER_GROUP = 8
N_EXPERTS = N_EXPERT_GROUPS * EXPERTS_PER_GROUP
TOP_K_INNER = 2
D_EXPERT = D_MODEL // 4
EPS = 1e-6

kernel_name = "hybrid_conformer_ssd_hmoe_step"


def rms_norm(x, w):
    xf = x.astype(jnp.float32)
    y = xf * lax.rsqrt(jnp.mean(xf * xf, axis=-1, keepdims=True) + EPS)
    return (y * w.astype(jnp.float32)).astype(x.dtype)


def layer_norm(x, g, b):
    xf = x.astype(jnp.float32)
    mu = jnp.mean(xf, axis=-1, keepdims=True)
    var = jnp.mean(jnp.square(xf - mu), axis=-1, keepdims=True)
    y = (xf - mu) * lax.rsqrt(var + EPS)
    return (y * g.astype(jnp.float32) + b.astype(jnp.float32)).astype(x.dtype)


def causal_dwconv(x_full, w, b):
    c = x_full.shape[-1]
    y = lax.conv_general_dilated(x_full, w[:, None, :].astype(x_full.dtype), window_strides=(1,),
                                 padding='VALID', dimension_numbers=('NWC', 'WIO', 'NWC'),
                                 feature_group_count=c)
    return y + b.astype(y.dtype)


def ssd_chunk(h, x, dt, A, bm, cm):
    t = x.shape[1]
    cum = jnp.cumsum(dt * A, axis=1)
    seg = cum[:, :, None] - cum[:, None, :]
    causal = jnp.tril(jnp.ones((t, t), dtype=bool))[None, :, :, None, None]
    decay = jnp.exp(jnp.where(causal, seg, -jnp.inf))
    cb = jnp.einsum('btgn,bsgn->btsg', cm, bm)
    w = cb[..., None] * decay * dt[:, None]
    y = jnp.einsum('btsgh,bsghp->btghp', w, x)
    y = y + jnp.einsum('btgn,bghpn->btghp', cm, h) * jnp.exp(cum)[..., None]
    tail = jnp.exp(cum[:, -1:] - cum) * dt
    h_new = jnp.exp(cum[:, -1])[..., None, None] * h + jnp.einsum('bsgh,bsgn,bsghp->bghpn', tail, bm, x)
    return h_new, y


def ssd(x, dt, A, bm, cm, h0, lead):
    bsz, length = x.shape[0], x.shape[1]
    h, y_lead = ssd_chunk(h0, x[:, :lead], dt[:, :lead], A, bm[:, :lead], cm[:, :lead])
    rest = length - lead
    if rest == 0:
        return y_lead, h
    n = rest // SSD_CHUNK

    def to_chunks(a):
        a = a[:, lead:]
        return jnp.moveaxis(a.reshape((bsz, n, SSD_CHUNK) + a.shape[2:]), 1, 0)

    def step(hc, inp):
        xc, dtc, bc, cc = inp
        return ssd_chunk(hc, xc, dtc, A, bc, cc)

    h, ys = lax.scan(step, h, (to_chunks(x), to_chunks(dt), to_chunks(bm), to_chunks(cm)))
    ys = jnp.moveaxis(ys, 0, 1).reshape((bsz, rest) + ys.shape[3:])
    return jnp.concatenate([y_lead, ys], axis=1), h


def token_mixer(xn, conf_buf, ssm_buf, h0, lead, w_in, conf_dw_w, conf_dw_b, conf_ln_g, conf_ln_b,
                conf_out_w, conf_out_b, ssm_conv_w, ssm_conv_b, dt_bias, a_log, d_skip, ssm_norm_w,
                ssm_out_w, gate_b, w_o):
    bsz, length, _ = xn.shape
    f32 = jnp.float32
    hg = SSM_HEADS // SSM_GROUPS
    u = xn @ w_in
    s1 = D_CONF
    s2 = s1 + D_CONF
    s3 = s2 + D_INNER
    s4 = s3 + CONV_DIM
    s5 = s4 + SSM_HEADS
    glu_v, glu_g, z, xbc, dt_raw, gate_logit = jnp.split(u, [s1, s2, s3, s4, s5], axis=-1)
    a = glu_v * jax.nn.sigmoid(glu_g)
    a_full = jnp.concatenate([conf_buf.astype(a.dtype), a], axis=1)
    new_conf_buf = a_full[:, -(CONF_K - 1):]
    a = causal_dwconv(a_full, conf_dw_w, conf_dw_b)
    a = jax.nn.silu(layer_norm(a, conf_ln_g, conf_ln_b))
    y_a = a @ conf_out_w + conf_out_b
    xbc_full = jnp.concatenate([ssm_buf.astype(xbc.dtype), xbc], axis=1)
    new_ssm_buf = xbc_full[:, -(SSM_K - 1):]
    xbc = jax.nn.silu(causal_dwconv(xbc_full, ssm_conv_w, ssm_conv_b))
    xs, bm, cm = jnp.split(xbc, [D_INNER, D_INNER + SSM_GROUPS * SSM_STATE], axis=-1)
    xs_h = xs.astype(f32).reshape(bsz, length, SSM_GROUPS, hg, SSM_HEAD_DIM)
    dt = jax.nn.softplus(dt_raw.astype(f32) + dt_bias.astype(f32)).reshape(bsz, length, SSM_GROUPS, hg)
    A = -jnp.exp(a_log.astype(f32)).reshape(SSM_GROUPS, hg)
    bm = bm.astype(f32).reshape(bsz, length, SSM_GROUPS, SSM_STATE)
    cm = cm.astype(f32).reshape(bsz, length, SSM_GROUPS, SSM_STATE)
    h0g = h0.astype(f32).reshape(bsz, SSM_GROUPS, hg, SSM_HEAD_DIM, SSM_STATE)
    y, h = ssd(xs_h, dt, A, bm, cm, h0g, lead)
    y = y + d_skip.astype(f32).reshape(SSM_GROUPS, hg)[:, :, None] * xs_h
    y = y.reshape(bsz, length, D_INNER) * jax.nn.silu(z.astype(f32))
    yg = y.reshape(bsz, length, SSM_NORM_GROUPS, D_INNER // SSM_NORM_GROUPS)
    yg = yg * lax.rsqrt(jnp.mean(yg * yg, axis=-1, keepdims=True) + EPS)
    y = (yg.reshape(bsz, length, D_INNER) * ssm_norm_w.astype(f32)).astype(xn.dtype)
    y_b = y @ ssm_out_w
    gates = jax.nn.sigmoid(gate_logit + gate_b).reshape(bsz, length, N_GATES, D_MODEL)
    merged = gates[:, :, 0] * y_a + gates[:, :, 1] * y_b
    out = merged @ w_o
    new_h = h.reshape(bsz, SSM_HEADS, SSM_HEAD_DIM, SSM_STATE).astype(h0.dtype)
    return out, new_conf_buf, new_ssm_buf, new_h


def hier_moe(xn, rg_w, rg_b, re_w, re_b, w1, w3, w2):
    shp = xn.shape
    x = xn.reshape(-1, D_MODEL)
    t = x.shape[0]
    f32 = jnp.float32
    g_logit = (x @ rg_w + rg_b).astype(f32)
    g_prob = jax.nn.softmax(g_logit, axis=-1)
    g_sel = jnp.argmax(g_logit, axis=-1)
    e_logit = (x @ re_w + re_b).astype(f32).reshape(t, N_EXPERT_GROUPS, EXPERTS_PER_GROUP)
    e_in = jnp.take_along_axis(e_logit, g_sel[:, None, None], axis=1)[:, 0]
    vals, idx = lax.top_k(e_in, TOP_K_INNER)
    w = jax.nn.softmax(vals, axis=-1) * jnp.take_along_axis(g_prob, g_sel[:, None], axis=1)
    ids = g_sel[:, None] * EXPERTS_PER_GROUP + idx
    combine = jnp.sum(jax.nn.one_hot(ids, N_EXPERTS, dtype=f32) * w[..., None], axis=1)
    combine = combine.astype(x.dtype).reshape(t, N_EXPERT_GROUPS, EXPERTS_PER_GROUP)
    out = jnp.zeros_like(x)
    for gi in range(N_EXPERT_GROUPS):
        sl = slice(gi * EXPERTS_PER_GROUP, (gi + 1) * EXPERTS_PER_GROUP)
        h1 = jnp.einsum('td,edf->tef', x, w1[sl])
        h3 = jnp.einsum('td,edf->tef', x, w3[sl])
        act = jax.nn.silu(h1) * h3 * combine[:, gi, :, None]
        out = out + jnp.einsum('tef,efd->td', act, w2[sl])
    return out.reshape(shp)


def setup_inputs(seed: int = 0) -> dict:
    key = jax.random.key(seed)
    ks = jax.random.split(key, 32)
    nrm = jax.random.normal
    f32 = jnp.float32
    L = DEPTH
    dt0 = jnp.exp(jax.random.uniform(ks[14], (L, SSM_HEADS), minval=math.log(1e-3), maxval=math.log(1e-1)))
    return {
        "x_prompt": nrm(ks[0], (BATCH, SEQ, D_MODEL), f32),
        "x_sample": nrm(ks[1], (DEC_BATCH, DEC_SEQ, D_MODEL), f32),
        "state_conf_conv": 0.5 * nrm(ks[2], (L, DEC_BATCH, CONF_K - 1, D_CONF), f32),
        "state_ssm_conv": 0.5 * nrm(ks[3], (L, DEC_BATCH, SSM_K - 1, CONV_DIM), f32),
        "state_ssm": 0.1 * nrm(ks[4], (L, DEC_BATCH, SSM_HEADS, SSM_HEAD_DIM, SSM_STATE), f32),
        "meta_tokens": nrm(ks[5], (N_META, D_MODEL), f32),
        "norm_mix_w": 1.0 + 0.05 * nrm(ks[6], (L, D_MODEL), f32),
        "w_in": nrm(ks[7], (L, D_MODEL, N_IN), f32) * D_MODEL ** -0.5,
        "conf_dw_w": nrm(ks[8], (L, CONF_K, D_CONF), f32) * CONF_K ** -0.5,
        "conf_dw_b": 0.02 * nrm(ks[9], (L, D_CONF), f32),
        "conf_ln_g": 1.0 + 0.05 * nrm(ks[10], (L, D_CONF), f32),
        "conf_ln_b": 0.02 * nrm(ks[11], (L, D_CONF), f32),
        "conf_out_w": nrm(ks[12], (L, D_CONF, D_MODEL), f32) * D_CONF ** -0.5,
        "conf_out_b": 0.02 * nrm(ks[13], (L, D_MODEL), f32),
        "ssm_conv_w": nrm(ks[15], (L, SSM_K, CONV_DIM), f32) * SSM_K ** -0.5,
        "ssm_conv_b": 0.02 * nrm(ks[16], (L, CONV_DIM), f32),
        "dt_bias": dt0 + jnp.log(-jnp.expm1(-dt0)),
        "a_log": jnp.log(jax.random.uniform(ks[17], (L, SSM_HEADS), minval=1.0, maxval=16.0)),
        "d_skip": 1.0 + 0.05 * nrm(ks[18], (L, SSM_HEADS), f32),
        "ssm_norm_w": 1.0 + 0.05 * nrm(ks[19], (L, D_INNER), f32),
        "ssm_out_w": nrm(ks[20], (L, D_INNER, D_MODEL), f32) * D_INNER ** -0.5,
        "gate_b": 0.02 * nrm(ks[21], (L, N_GATES * D_MODEL), f32),
        "w_o": nrm(ks[22], (L, D_MODEL, D_MODEL), f32) * D_MODEL ** -0.5,
        "norm_ffn_w": 1.0 + 0.05 * nrm(ks[23], (L, D_MODEL), f32),
        "router_group_w": nrm(ks[24], (L, D_MODEL, N_EXPERT_GROUPS), f32) * D_MODEL ** -0.5,
        "router_group_b": 0.01 * nrm(ks[25], (L, N_EXPERT_GROUPS), f32),
        "router_expert_w": nrm(ks[26], (L, D_MODEL, N_EXPERTS), f32) * D_MODEL ** -0.5,
        "router_expert_b": 0.01 * nrm(ks[27], (L, N_EXPERTS), f32),
        "exp_w1": nrm(ks[28], (L, N_EXPERTS, D_MODEL, D_EXPERT), f32) * D_MODEL ** -0.5,
        "exp_w3": nrm(ks[29], (L, N_EXPERTS, D_MODEL, D_EXPERT), f32) * D_MODEL ** -0.5,
        "exp_w2": nrm(ks[30], (L, N_EXPERTS, D_EXPERT, D_MODEL), f32) * D_EXPERT ** -0.5,
        "norm_final_w": 1.0 + 0.05 * nrm(ks[31], (D_MODEL,), f32),
    }


def reference(x_prompt, x_sample, state_conf_conv, state_ssm_conv, state_ssm, meta_tokens, norm_mix_w,
              w_in, conf_dw_w, conf_dw_b, conf_ln_g, conf_ln_b, conf_out_w, conf_out_b, ssm_conv_w,
              ssm_conv_b, dt_bias, a_log, d_skip, ssm_norm_w, ssm_out_w, gate_b, w_o, norm_ffn_w,
              router_group_w, router_group_b, router_expert_w, router_expert_b, exp_w1, exp_w3, exp_w2,
              norm_final_w):
    bp = x_prompt.shape[0]
    dt_ = x_prompt.dtype
    meta = jnp.broadcast_to(meta_tokens[None].astype(dt_), (bp, N_META, D_MODEL))
    xp = jnp.concatenate([meta, x_prompt], axis=1)
    xs = x_sample
    pc, pm, ps, sc, sm, ss = [], [], [], [], [], []
    for l in range(DEPTH):
        wts = (w_in[l], conf_dw_w[l], conf_dw_b[l], conf_ln_g[l], conf_ln_b[l], conf_out_w[l], conf_out_b[l],
               ssm_conv_w[l], ssm_conv_b[l], dt_bias[l], a_log[l], d_skip[l], ssm_norm_w[l], ssm_out_w[l],
               gate_b[l], w_o[l])
        moe_w = (router_group_w[l], router_group_b[l], router_expert_w[l], router_expert_b[l],
                 exp_w1[l], exp_w3[l], exp_w2[l])
        zc = jnp.zeros((bp, CONF_K - 1, D_CONF), dt_)
        zm = jnp.zeros((bp, SSM_K - 1, CONV_DIM), dt_)
        zh = jnp.zeros((bp, SSM_HEADS, SSM_HEAD_DIM, SSM_STATE), dt_)
        o, c1, c2, c3 = token_mixer(rms_norm(xp, norm_mix_w[l]), zc, zm, zh, N_META, *wts)
        xp = xp + o
        xp = xp + hier_moe(rms_norm(xp, norm_ffn_w[l]), *moe_w)
        pc.append(c1)
        pm.append(c2)
        ps.append(c3)
        o, c1, c2, c3 = token_mixer(rms_norm(xs, norm_mix_w[l]), state_conf_conv[l], state_ssm_conv[l],
                                    state_ssm[l], xs.shape[1], *wts)
        xs = xs + o
        xs = xs + hier_moe(rms_norm(xs, norm_ffn_w[l]), *moe_w)
        sc.append(c1)
        sm.append(c2)
        ss.append(c3)
    y_prompt = rms_norm(xp, norm_final_w)[:, N_META:]
    y_sample = rms_norm(xs, norm_final_w)
    new_conf_prompt = jnp.stack(pc)
    new_sconv_prompt = jnp.stack(pm)
    new_ssm_prompt = jnp.stack(ps)
    new_conf_sample = jnp.stack(sc)
    new_sconv_sample = jnp.stack(sm)
    new_ssm_sample = jnp.stack(ss)
    return (y_prompt, y_sample, new_conf_prompt, new_sconv_prompt, new_ssm_prompt,
            new_conf_sample, new_sconv_sample, new_ssm_sample)
```

```python
import functools

import jax
import jax.numpy as jnp
from jax import lax
from jax.experimental import pallas as pl
from jax.experimental.pallas import tpu as pltpu

F32 = jnp.float32
BF16 = jnp.bfloat16
EPS = 1e-6

LANES = 128
SUBLANES = 8
TILE = 128
VMEM_LIMIT = 56 * 1024 * 1024

N_META = 16
CONF_K = 31
SSM_K = 4
HEAD_DIM = 64
N_STATE = 128
N_GROUPS = 8
HEADS_PER_GROUP = 4
GROUP_W = HEADS_PER_GROUP * HEAD_DIM
N_EXPERTS = 32
EXPERTS_PER_GROUP = 8
N_EXPERT_GROUPS = 4
SAMPLE_ROWS = 8
CONF_PAD = 32
SCONV_PAD = 8


def _rms(x, w):
    return x * lax.rsqrt(jnp.mean(x * x, axis=-1, keepdims=True) + EPS) * w


def _silu(x):
    return x * jax.nn.sigmoid(x)


def _dot(a, b):
    return jnp.dot(a, b, preferred_element_type=F32)


def _dot_nt(a, b):
    return lax.dot_general(a, b, (((1,), (1,)), ((), ())), preferred_element_type=F32)


def _dot_tn(a, b):
    return lax.dot_general(a, b, (((0,), (0,)), ((), ())), preferred_element_type=F32)


def _const_spec(shape):
    zeros = (0,) * len(shape)
    return pl.BlockSpec(shape, lambda *_: zeros, pipeline_mode=pl.Buffered(1))


def _dwconv(src_ref, base, rows, w_ref, b_ref, taps, width, emit):
    for c in range(width // LANES):
        cs = slice(c * LANES, (c + 1) * LANES)
        acc = jnp.broadcast_to(b_ref[:, cs], (rows, LANES))
        for k in range(taps):
            acc = acc + w_ref[k:k + 1, cs] * src_ref[pl.ds(base + k, rows), cs]
        emit(cs, acc)


def _conf_tail(conv_ref, ln_g_ref, ln_b_ref, wout_ref, bout_ref, ya_ref):
    c = conv_ref[...]
    mu = jnp.mean(c, axis=-1, keepdims=True)
    d = c - mu
    var = jnp.mean(d * d, axis=-1, keepdims=True)
    y = d * lax.rsqrt(var + EPS) * ln_g_ref[...] + ln_b_ref[...]
    y = _silu(y)
    ya_ref[...] = _dot(y.astype(BF16), wout_ref[...]) + bout_ref[...]


def _conf_glu(x, nw_ref, wvg_ref):
    d = x.shape[-1]
    xn = _rms(x, nw_ref[...]).astype(BF16)
    vg = _dot(xn, wvg_ref[...])
    return vg[:, :d] * jax.nn.sigmoid(vg[:, d:])


def _conf_prompt_kernel(head_ref, x_ref, nw_ref, wvg_ref, dww_ref, dwb_ref, lng_ref, lnb_ref, wout_ref, bout_ref,
                        ya_ref, nc_ref, afull_ref, conv_ref):
    j = pl.program_id(1)
    d = x_ref.shape[-1]

    @pl.when(j == 0)
    def _():
        afull_ref[0:CONF_PAD, :] = jnp.zeros((CONF_PAD, d), F32)

    x = jnp.where(j == 0, head_ref[...], x_ref[...])
    afull_ref[CONF_PAD:CONF_PAD + TILE, :] = _conf_glu(x, nw_ref, wvg_ref)

    def emit(cs, v):
        conv_ref[:, cs] = v

    _dwconv(afull_ref, CONF_PAD - (CONF_K - 1), TILE, dww_ref, dwb_ref, CONF_K, d, emit)
    nc_ref[...] = afull_ref[pl.ds(CONF_PAD + TILE - (CONF_K - 1), CONF_K - 1), :]
    afull_ref[0:CONF_PAD, :] = afull_ref[TILE:TILE + CONF_PAD, :]
    _conf_tail(conv_ref, lng_ref, lnb_ref, wout_ref, bout_ref, ya_ref)


def _conf_sample_kernel(x_ref, st_ref, nw_ref, wvg_ref, dww_ref, dwb_ref, lng_ref, lnb_ref, wout_ref, bout_ref,
                        ya_ref, nc_ref, a_ref, cs_ref, conv_ref, *, n_new):
    d = x_ref.shape[-1]
    n_seq = TILE // SAMPLE_ROWS
    a_ref[...] = _conf_glu(x_ref[...], nw_ref, wvg_ref)

    def seq_body(s, carry):
        r0 = pl.multiple_of(s * SAMPLE_ROWS, SAMPLE_ROWS)
        cs_ref[0:CONF_PAD, :] = st_ref[s]
        cs_ref[CONF_PAD:CONF_PAD + SAMPLE_ROWS, :] = a_ref[pl.ds(r0, SAMPLE_ROWS), :]

        def emit(cs, v):
            conv_ref[pl.ds(r0, SAMPLE_ROWS), cs] = v

        _dwconv(cs_ref, CONF_PAD - (CONF_K - 1), SAMPLE_ROWS, dww_ref, dwb_ref, CONF_K, d, emit)
        nc_ref[s] = cs_ref[pl.ds(CONF_PAD + n_new - (CONF_K - 1), CONF_K - 1), :]
        return carry

    lax.fori_loop(0, n_seq, seq_body, 0)
    _conf_tail(conv_ref, lng_ref, lnb_ref, wout_ref, bout_ref, ya_ref)


def _conf_weights(p):
    d = p["norm_w"].shape[-1]
    return [p["norm_w"], p["w_vg"], p["conf_dw_w"], p["conf_dw_b"], p["conf_ln_g"], p["conf_ln_b"],
            p["conf_out_w"], p["conf_out_b"]], [
        _const_spec((1, d)), _const_spec((d, 2 * d)), _const_spec((CONF_PAD, d)), _const_spec((1, d)),
        _const_spec((1, d)), _const_spec((1, d)), _const_spec((d, d)), _const_spec((1, d))]


def _conf_prompt(head, x_prompt, p):
    b, l, d = x_prompt.shape
    nt = l // TILE + 1
    weights, wspecs = _conf_weights(p)
    xmap = lambda i, j: (i, jnp.maximum(j - 1, 0), 0)
    return pl.pallas_call(
        _conf_prompt_kernel,
        grid=(b, nt),
        in_specs=[_const_spec((TILE, d)), pl.BlockSpec((None, TILE, d), xmap)] + wspecs,
        out_specs=[pl.BlockSpec((None, TILE, d), xmap),
                   pl.BlockSpec((None, CONF_K - 1, d), lambda i, j: (i, 0, 0))],
        out_shape=[jax.ShapeDtypeStruct((b, l, d), F32), jax.ShapeDtypeStruct((b, CONF_K - 1, d), F32)],
        scratch_shapes=[pltpu.VMEM((CONF_PAD + TILE, d), F32), pltpu.VMEM((TILE, d), F32)],
        compiler_params=pltpu.CompilerParams(dimension_semantics=("arbitrary", "arbitrary"),
                                             vmem_limit_bytes=VMEM_LIMIT),
        name="conf_prompt",
    )(head, x_prompt, *weights)


def _conf_sample(x_rows, state_pad, p, n_new):
    rows, d = x_rows.shape
    n_seq = TILE // SAMPLE_ROWS
    weights, wspecs = _conf_weights(p)
    return pl.pallas_call(
        functools.partial(_conf_sample_kernel, n_new=n_new),
        grid=(rows // TILE,),
        in_specs=[pl.BlockSpec((TILE, d), lambda i: (i, 0)),
                  pl.BlockSpec((n_seq, CONF_PAD, d), lambda i: (i, 0, 0))] + wspecs,
        out_specs=[pl.BlockSpec((TILE, d), lambda i: (i, 0)),
                   pl.BlockSpec((n_seq, CONF_K - 1, d), lambda i: (i, 0, 0))],
        out_shape=[jax.ShapeDtypeStruct((rows, d), F32),
                   jax.ShapeDtypeStruct((rows // SAMPLE_ROWS, CONF_K - 1, d), F32)],
        scratch_shapes=[pltpu.VMEM((TILE, d), F32), pltpu.VMEM((CONF_PAD + SAMPLE_ROWS, d), F32),
                        pltpu.VMEM((TILE, d), F32)],
        compiler_params=pltpu.CompilerParams(dimension_semantics=("arbitrary",), vmem_limit_bytes=VMEM_LIMIT),
        name="conf_sample",
    )(x_rows, state_pad, *weights)


def _ssm_project(x, nw_ref, wzx_ref, wdt_ref, d_inner):
    xn = _rms(x, nw_ref[...]).astype(BF16)
    z = _dot(xn, wzx_ref[:, 0:d_inner])
    xbc = _dot(xn, wzx_ref[:, d_inner:])
    dtr = _dot(xn, wdt_ref[...])
    return z, xbc, dtr


def _seg_cumsum(x, row, seg_len, reverse):
    pos = row % seg_len
    n = x.shape[0]
    step = 1
    while step < seg_len:
        if reverse:
            x = x + jnp.where(pos < seg_len - step, pltpu.roll(x, n - step, axis=0), 0.0)
        else:
            x = x + jnp.where(pos >= step, pltpu.roll(x, step, axis=0), 0.0)
        step *= 2
    return x


def _expand_heads(m, g, lane):
    rows = m.shape[0]
    cols = [jnp.broadcast_to(m[:, HEADS_PER_GROUP * g + i:HEADS_PER_GROUP * g + i + 1], (rows, LANES))
            for i in range(HEADS_PER_GROUP)]
    lo = jnp.where(lane < HEAD_DIM, cols[0], cols[1])
    hi = jnp.where(lane < HEAD_DIM, cols[2], cols[3])
    return jnp.concatenate([lo, hi], axis=1)


def _ssd_tile(z_ref, xc_ref, dtr_ref, valid, seg_len, n_seq, h_in, h_out, dtb_ref, alog_ref, dskip_ref,
              normw_ref, wout_ref, yb_ref):
    d_inner = z_ref.shape[-1]
    d_model = wout_ref.shape[-1]
    row = lax.broadcasted_iota(jnp.int32, (TILE, TILE), 0)
    col = lax.broadcasted_iota(jnp.int32, (TILE, TILE), 1)
    causal = (col <= row) & ((col // seg_len) == (row // seg_len))
    lane = lax.broadcasted_iota(jnp.int32, (TILE, LANES), 1)
    lane_g = lax.broadcasted_iota(jnp.int32, (TILE, GROUP_W), 1)

    xdt = dtr_ref[...] + dtb_ref[...]
    dt = (jnp.maximum(xdt, 0.0) + jnp.log1p(jnp.exp(-jnp.abs(xdt)))) * valid
    da = dt * (-jnp.exp(alog_ref[...]))
    cum = _seg_cumsum(da, row, seg_len, reverse=False)
    tail = jnp.exp(_seg_cumsum(da, row, seg_len, reverse=True) - da) * dt
    ecum = jnp.exp(cum)
    cum_t = cum.T
    dt_t = dt.T

    acc = jnp.zeros((TILE, d_model), F32)
    for g in range(N_GROUPS):
        xs = xc_ref[:, g * GROUP_W:(g + 1) * GROUP_W]
        bm = xc_ref[:, d_inner + g * N_STATE:d_inner + (g + 1) * N_STATE]
        cm = xc_ref[:, d_inner + (N_GROUPS + g) * N_STATE:d_inner + (N_GROUPS + g + 1) * N_STATE]
        bm16 = bm.astype(BF16)
        cm16 = cm.astype(BF16)
        cb = _dot_nt(cm16, bm16)
        yg = jnp.zeros((TILE, GROUP_W), F32)
        for i in range(HEADS_PER_GROUP):
            h = HEADS_PER_GROUP * g + i
            seg = cum[:, h:h + 1] - cum_t[h:h + 1, :]
            decay = jnp.exp(jnp.where(causal, seg, -jnp.inf))
            w = (cb * decay * dt_t[h:h + 1, :]).astype(BF16)
            xm = jnp.where((lane_g // HEAD_DIM) == i, xs, 0.0).astype(BF16)
            yg = yg + _dot(w, xm)
        ecum_g = _expand_heads(ecum, g, lane)
        xw16 = (xs * _expand_heads(tail, g, lane)).astype(BF16)
        ystate = []
        for s in range(n_seq):
            rs = slice(s * seg_len, (s + 1) * seg_len)
            last = (s + 1) * seg_len - 1
            hst = h_in(s, g)
            ystate.append(_dot_nt(cm16[rs], hst.astype(BF16)))
            upd = _dot_tn(xw16[rs], bm16[rs])
            dec = jnp.concatenate(
                [jnp.broadcast_to(jnp.exp(cum_t[HEADS_PER_GROUP * g + i:HEADS_PER_GROUP * g + i + 1, last:last + 1]),
                                  (HEAD_DIM, N_STATE)) for i in range(HEADS_PER_GROUP)], axis=0)
            h_out(s, g, hst * dec + upd)
        if n_seq * seg_len < TILE:
            ystate.append(jnp.zeros((TILE - n_seq * seg_len, GROUP_W), F32))
        ys = ystate[0] if len(ystate) == 1 else jnp.concatenate(ystate, axis=0)
        yg = yg + ys * ecum_g
        zg = z_ref[:, g * GROUP_W:(g + 1) * GROUP_W]
        v = (yg + dskip_ref[:, g * GROUP_W:(g + 1) * GROUP_W] * xs) * _silu(zg)
        v = v * lax.rsqrt(jnp.mean(v * v, axis=-1, keepdims=True) + EPS) * normw_ref[:, g * GROUP_W:(g + 1) * GROUP_W]
        acc = acc + _dot(v.astype(BF16), wout_ref[g * GROUP_W:(g + 1) * GROUP_W, :])
    yb_ref[...] = acc


def _ssm_prompt_kernel(head_ref, x_ref, nw_ref, wzx_ref, wdt_ref, cw_ref, cb_ref, dtb_ref, alog_ref, dskip_ref,
                       normw_ref, wout_ref, yb_ref, ns_ref, hs_ref, z_ref, cfull_ref, xc_ref, dtr_ref):
    j = pl.program_id(1)
    d_inner = z_ref.shape[-1]
    conv_dim = xc_ref.shape[-1]

    @pl.when(j == 0)
    def _():
        cfull_ref[0:SCONV_PAD, :] = jnp.zeros((SCONV_PAD, conv_dim), F32)
        hs_ref[...] = jnp.zeros(hs_ref.shape, F32)

    x = jnp.where(j == 0, head_ref[...], x_ref[...])
    z, xbc, dtr = _ssm_project(x, nw_ref, wzx_ref, wdt_ref, d_inner)
    z_ref[...] = z
    dtr_ref[...] = dtr
    cfull_ref[SCONV_PAD:SCONV_PAD + TILE, :] = xbc

    def emit(cs, v):
        xc_ref[:, cs] = _silu(v)

    _dwconv(cfull_ref, SCONV_PAD - (SSM_K - 1), TILE, cw_ref, cb_ref, SSM_K, conv_dim, emit)
    ns_ref[...] = cfull_ref[pl.ds(SCONV_PAD + TILE - (SSM_K - 1), SSM_K - 1), :]
    cfull_ref[0:SCONV_PAD, :] = cfull_ref[TILE:TILE + SCONV_PAD, :]

    rowv = lax.broadcasted_iota(jnp.int32, (TILE, 1), 0)
    valid = jnp.where((j == 0) & (rowv < TILE - N_META), 0.0, 1.0)

    def h_in(s, g):
        return hs_ref[HEADS_PER_GROUP * g:HEADS_PER_GROUP * (g + 1)].reshape(GROUP_W, N_STATE)

    def h_out(s, g, v):
        hs_ref[HEADS_PER_GROUP * g:HEADS_PER_GROUP * (g + 1)] = v.reshape(HEADS_PER_GROUP, HEAD_DIM, N_STATE)

    _ssd_tile(z_ref, xc_ref, dtr_ref, valid, TILE, 1, h_in, h_out, dtb_ref, alog_ref, dskip_ref, normw_ref,
              wout_ref, yb_ref)


def _ssm_sample_proj_kernel(x_ref, st_ref, nw_ref, wzx_ref, wdt_ref, cw_ref, cb_ref,
                            z_ref, xc_ref, dtr_ref, ns_ref, xbc_ref, cs_ref, *, n_new):
    d_inner = z_ref.shape[-1]
    conv_dim = xc_ref.shape[-1]
    n_seq = TILE // SAMPLE_ROWS
    z, xbc, dtr = _ssm_project(x_ref[...], nw_ref, wzx_ref, wdt_ref, d_inner)
    z_ref[...] = z
    dtr_ref[...] = dtr
    xbc_ref[...] = xbc

    def seq_body(s, carry):
        r0 = pl.multiple_of(s * SAMPLE_ROWS, SAMPLE_ROWS)
        cs_ref[0:SCONV_PAD, :] = st_ref[s]
        cs_ref[SCONV_PAD:SCONV_PAD + SAMPLE_ROWS, :] = xbc_ref[pl.ds(r0, SAMPLE_ROWS), :]

        def emit(cs, v):
            xc_ref[pl.ds(r0, SAMPLE_ROWS), cs] = _silu(v)

        _dwconv(cs_ref, SCONV_PAD - (SSM_K - 1), SAMPLE_ROWS, cw_ref, cb_ref, SSM_K, conv_dim, emit)
        ns_ref[s] = cs_ref[pl.ds(SCONV_PAD + n_new - (SSM_K - 1), SSM_K - 1), :]
        return carry

    lax.fori_loop(0, n_seq, seq_body, 0)


def _ssm_sample_ssd_kernel(z_in, xc_in, dtr_in, hin_ref, dtb_ref, alog_ref, dskip_ref, normw_ref, wout_ref,
                           yb_ref, hout_ref, z_ref, xc_ref, dtr_ref, ybfull_ref, *, n_new, n_seq):
    rows = n_seq * SAMPLE_ROWS
    for src, dst in ((z_in, z_ref), (xc_in, xc_ref), (dtr_in, dtr_ref)):
        dst[0:rows, :] = src[...]
        dst[rows:TILE, :] = jnp.zeros((TILE - rows, dst.shape[-1]), F32)
    rowv = lax.broadcasted_iota(jnp.int32, (TILE, 1), 0)
    valid = jnp.where((rowv < rows) & ((rowv % SAMPLE_ROWS) < n_new), 1.0, 0.0)

    def h_in(s, g):
        return hin_ref[s, HEADS_PER_GROUP * g:HEADS_PER_GROUP * (g + 1)].reshape(GROUP_W, N_STATE)

    def h_out(s, g, v):
        hout_ref[s, HEADS_PER_GROUP * g:HEADS_PER_GROUP * (g + 1)] = v.reshape(HEADS_PER_GROUP, HEAD_DIM, N_STATE)

    _ssd_tile(z_ref, xc_ref, dtr_ref, valid, SAMPLE_ROWS, n_seq, h_in, h_out, dtb_ref, alog_ref, dskip_ref,
              normw_ref, wout_ref, ybfull_ref)
    yb_ref[...] = ybfull_ref[0:rows, :]


def _ssm_prompt(head, x_prompt, p):
    b, l, d = x_prompt.shape
    nt = l // TILE + 1
    d_inner = p["ssm_norm_w"].shape[-1]
    conv_dim = p["ssm_conv_b"].shape[-1]
    n_heads = d_inner // HEAD_DIM
    xmap = lambda i, j: (i, jnp.maximum(j - 1, 0), 0)
    weights = [p["norm_w"], p["w_zx"], p["w_dt"], p["ssm_conv_w"], p["ssm_conv_b"], p["dt_bias"], p["a_log"],
               p["d_skip"], p["ssm_norm_w"], p["ssm_out_w"]]
    wspecs = [_const_spec(w.shape) for w in weights]
    return pl.pallas_call(
        _ssm_prompt_kernel,
        grid=(b, nt),
        in_specs=[_const_spec((TILE, d)), pl.BlockSpec((None, TILE, d), xmap)] + wspecs,
        out_specs=[pl.BlockSpec((None, TILE, d), xmap),
                   pl.BlockSpec((None, SSM_K - 1, conv_dim), lambda i, j: (i, 0, 0)),
                   pl.BlockSpec((None, n_heads, HEAD_DIM, N_STATE), lambda i, j: (i, 0, 0, 0))],
        out_shape=[jax.ShapeDtypeStruct((b, l, d), F32), jax.ShapeDtypeStruct((b, SSM_K - 1, conv_dim), F32),
                   jax.ShapeDtypeStruct((b, n_heads, HEAD_DIM, N_STATE), F32)],
        scratch_shapes=[pltpu.VMEM((TILE, d_inner), F32), pltpu.VMEM((SCONV_PAD + TILE, conv_dim), F32),
                        pltpu.VMEM((TILE, conv_dim), F32), pltpu.VMEM((TILE, LANES), F32)],
        compiler_params=pltpu.CompilerParams(dimension_semantics=("arbitrary", "arbitrary"),
                                             vmem_limit_bytes=VMEM_LIMIT),
        name="ssm_prompt",
    )(head, x_prompt, *weights)


def _ssm_sample(x_rows, sconv_pad, h0, p, n_new):
    rows, d = x_rows.shape
    d_inner = p["ssm_norm_w"].shape[-1]
    conv_dim = p["ssm_conv_b"].shape[-1]
    n_heads = d_inner // HEAD_DIM
    n_seq_proj = TILE // SAMPLE_ROWS
    weights = [p["norm_w"], p["w_zx"], p["w_dt"], p["ssm_conv_w"], p["ssm_conv_b"]]
    z, xc, dtr, new_sconv = pl.pallas_call(
        functools.partial(_ssm_sample_proj_kernel, n_new=n_new),
        grid=(rows // TILE,),
        in_specs=[pl.BlockSpec((TILE, d), lambda i: (i, 0)),
                  pl.BlockSpec((n_seq_proj, SCONV_PAD, conv_dim), lambda i: (i, 0, 0))]
        + [_const_spec(w.shape) for w in weights],
        out_specs=[pl.BlockSpec((TILE, d_inner), lambda i: (i, 0)), pl.BlockSpec((TILE, conv_dim), lambda i: (i, 0)),
                   pl.BlockSpec((TILE, LANES), lambda i: (i, 0)),
                   pl.BlockSpec((n_seq_proj, SSM_K - 1, conv_dim), lambda i: (i, 0, 0))],
        out_shape=[jax.ShapeDtypeStruct((rows, d_inner), F32), jax.ShapeDtypeStruct((rows, conv_dim), F32),
                   jax.ShapeDtypeStruct((rows, LANES), F32),
                   jax.ShapeDtypeStruct((rows // SAMPLE_ROWS, SSM_K - 1, conv_dim), F32)],
        scratch_shapes=[pltpu.VMEM((TILE, conv_dim), F32), pltpu.VMEM((SCONV_PAD + SAMPLE_ROWS, conv_dim), F32)],
        compiler_params=pltpu.CompilerParams(dimension_semantics=("arbitrary",), vmem_limit_bytes=VMEM_LIMIT),
        name="ssm_sample_proj",
    )(x_rows, sconv_pad, *weights)

    n_seq = 4
    r = n_seq * SAMPLE_ROWS
    weights = [p["dt_bias"], p["a_log"], p["d_skip"], p["ssm_norm_w"], p["ssm_out_w"]]
    yb, h_new = pl.pallas_call(
        functools.partial(_ssm_sample_ssd_kernel, n_new=n_new, n_seq=n_seq),
        grid=(rows // r,),
        in_specs=[pl.BlockSpec((r, d_inner), lambda i: (i, 0)), pl.BlockSpec((r, conv_dim), lambda i: (i, 0)),
                  pl.BlockSpec((r, LANES), lambda i: (i, 0)),
                  pl.BlockSpec((n_seq, n_heads, HEAD_DIM, N_STATE), lambda i: (i, 0, 0, 0))]
        + [_const_spec(w.shape) for w in weights],
        out_specs=[pl.BlockSpec((r, d), lambda i: (i, 0)),
                   pl.BlockSpec((n_seq, n_heads, HEAD_DIM, N_STATE), lambda i: (i, 0, 0, 0))],
        out_shape=[jax.ShapeDtypeStruct((rows, d), F32), jax.ShapeDtypeStruct(h0.shape, F32)],
        scratch_shapes=[pltpu.VMEM((TILE, d_inner), F32), pltpu.VMEM((TILE, conv_dim), F32),
                        pltpu.VMEM((TILE, LANES), F32), pltpu.VMEM((TILE, d), F32)],
        compiler_params=pltpu.CompilerParams(dimension_semantics=("arbitrary",), vmem_limit_bytes=VMEM_LIMIT),
        name="ssm_sample_ssd",
    )(z, xc, dtr, h0, *weights)
    return yb, new_sconv, h_new


def _merge_kernel(x_ref, ya_ref, yb_ref, nw_ref, wg_ref, gb_ref, wo_ref, nfw_ref, rw_ref, rb_ref,
                  x1_ref, xn2_ref, comb_ref):
    d = x_ref.shape[-1]
    x = x_ref[...]
    xn = _rms(x, nw_ref[...]).astype(BF16)
    gates = jax.nn.sigmoid(_dot(xn, wg_ref[...]) + gb_ref[...])
    merged = gates[:, :d] * ya_ref[...] + gates[:, d:] * yb_ref[...]
    x1 = x + _dot(merged.astype(BF16), wo_ref[...])
    x1_ref[...] = x1
    xn2 = _rms(x1, nfw_ref[...])
    xn2_ref[...] = xn2.astype(BF16)

    logits = jnp.dot(xn2, rw_ref[...], preferred_element_type=F32, precision=lax.Precision.HIGHEST) + rb_ref[...]
    rows = logits.shape[0]
    lane = lax.broadcasted_iota(jnp.int32, (rows, LANES), 1)
    neg = -jnp.inf
    is_g = (lane >= N_EXPERTS) & (lane < N_EXPERTS + N_EXPERT_GROUPS)
    gl = jnp.where(is_g, logits, neg)
    gmax = jnp.max(gl, axis=-1, keepdims=True)
    gsel = jnp.min(jnp.where(gl == gmax, lane, LANES), axis=-1, keepdims=True) - N_EXPERTS
    gprob = 1.0 / jnp.sum(jnp.exp(gl - gmax), axis=-1, keepdims=True)
    el = jnp.where((lane < N_EXPERTS) & ((lane // EXPERTS_PER_GROUP) == gsel), logits, neg)
    m1 = jnp.max(el, axis=-1, keepdims=True)
    i1 = jnp.min(jnp.where(el == m1, lane, LANES), axis=-1, keepdims=True)
    el2 = jnp.where(lane == i1, neg, el)
    m2 = jnp.max(el2, axis=-1, keepdims=True)
    i2 = jnp.min(jnp.where(el2 == m2, lane, LANES), axis=-1, keepdims=True)
    e2 = jnp.exp(m2 - m1)
    den = 1.0 + e2
    comb_ref[...] = jnp.where(lane == i1, (1.0 / den) * gprob, jnp.where(lane == i2, (e2 / den) * gprob, 0.0))


def _merge(x, ya, yb, p, tm):
    rows, d = x.shape
    weights = [p["norm_w"], p["w_gate"], p["gate_b"], p["w_o"], p["norm_ffn_w"], p["router_w"], p["router_b"]]
    row_spec = pl.BlockSpec((tm, d), lambda i: (i, 0))
    return pl.pallas_call(
        _merge_kernel,
        grid=(rows // tm,),
        in_specs=[row_spec, row_spec, row_spec] + [_const_spec(w.shape) for w in weights],
        out_specs=[row_spec, row_spec, pl.BlockSpec((tm, LANES), lambda i: (i, 0))],
        out_shape=[jax.ShapeDtypeStruct((rows, d), F32), jax.ShapeDtypeStruct((rows, d), BF16),
                   jax.ShapeDtypeStruct((rows, LANES), F32)],
        compiler_params=pltpu.CompilerParams(dimension_semantics=("arbitrary",), vmem_limit_bytes=VMEM_LIMIT),
        name="merge_router",
    )(x, ya, yb, *weights)


def _moe_kernel(xn_ref, comb_ref, x1_ref, w1_ref, w3_ref, w2_ref, nw_ref, y_ref, acc_ref):
    e = pl.program_id(1)

    @pl.when(e == 0)
    def _():
        acc_ref[...] = jnp.zeros(acc_ref.shape, F32)

    xn = xn_ref[...]
    lane = lax.broadcasted_iota(jnp.int32, comb_ref.shape, 1)
    c = jnp.sum(jnp.where(lane == e, comb_ref[...], 0.0), axis=-1, keepdims=True)
    h1 = _dot(xn, w1_ref[...])
    h3 = _dot(xn, w3_ref[...])
    act = _silu(h1) * h3 * c
    acc_ref[...] += _dot(act.astype(BF16), w2_ref[...])

    @pl.when(e == pl.num_programs(1) - 1)
    def _():
        y_ref[...] = _rms(x1_ref[...] + acc_ref[...], nw_ref[...])


def _moe(xn2, comb, x1, p, tm):
    rows, d = x1.shape
    n_e, _, d_e = p["exp_w1"].shape
    row = lambda i, e: (i, 0)
    return pl.pallas_call(
        _moe_kernel,
        grid=(rows // tm, n_e),
        in_specs=[pl.BlockSpec((tm, d), row), pl.BlockSpec((tm, LANES), row), pl.BlockSpec((tm, d), row),
                  pl.BlockSpec((None, d, d_e), lambda i, e: (e, 0, 0)),
                  pl.BlockSpec((None, d, d_e), lambda i, e: (e, 0, 0)),
                  pl.BlockSpec((None, d_e, d), lambda i, e: (e, 0, 0)),
                  _const_spec((1, d))],
        out_specs=pl.BlockSpec((tm, d), row),
        out_shape=jax.ShapeDtypeStruct((rows, d), F32),
        scratch_shapes=[pltpu.VMEM((tm, d), F32)],
        compiler_params=pltpu.CompilerParams(dimension_semantics=("arbitrary", "arbitrary"),
                                             vmem_limit_bytes=VMEM_LIMIT),
        name="moe",
    )(xn2, comb, x1, p["exp_w1"], p["exp_w3"], p["exp_w2"], p["norm_final_w"])


def _prep_params(norm_mix_w, w_in, conf_dw_w, conf_dw_b, conf_ln_g, conf_ln_b, conf_out_w, conf_out_b, ssm_conv_w,
                 ssm_conv_b, dt_bias, a_log, d_skip, ssm_norm_w, ssm_out_w, gate_b, w_o, norm_ffn_w,
                 router_group_w, router_group_b, router_expert_w, router_expert_b, exp_w1, exp_w3, exp_w2,
                 norm_final_w):
    d = norm_mix_w.shape[-1]
    d_inner = ssm_norm_w.shape[-1]
    conv_dim = ssm_conv_b.shape[-1]
    n_heads = dt_bias.shape[-1]
    s1, s2 = d, 2 * d
    s3 = s2 + d_inner
    s4 = s3 + conv_dim
    s5 = s4 + n_heads
    w = w_in[0]
    row = lambda v: v.reshape(1, -1).astype(F32)
    pad_lanes = lambda v: jnp.pad(v, ((0, 0), (0, LANES - v.shape[-1])))
    return {
        "norm_w": row(norm_mix_w[0]),
        "w_vg": w[:, :s2].astype(BF16),
        "w_zx": w[:, s2:s4].astype(BF16),
        "w_dt": pad_lanes(w[:, s4:s5]).astype(BF16),
        "w_gate": w[:, s5:].astype(BF16),
        "conf_dw_w": jnp.pad(conf_dw_w[0], ((0, CONF_PAD - CONF_K), (0, 0))),
        "conf_dw_b": row(conf_dw_b[0]),
        "conf_ln_g": row(conf_ln_g[0]),
        "conf_ln_b": row(conf_ln_b[0]),
        "conf_out_w": conf_out_w[0].astype(BF16),
        "conf_out_b": row(conf_out_b[0]),
        "ssm_conv_w": jnp.pad(ssm_conv_w[0], ((0, SUBLANES - SSM_K), (0, 0))),
        "ssm_conv_b": row(ssm_conv_b[0]),
        "dt_bias": pad_lanes(row(dt_bias[0])),
        "a_log": pad_lanes(row(a_log[0])),
        "d_skip": row(jnp.repeat(d_skip[0], HEAD_DIM)),
        "ssm_norm_w": row(ssm_norm_w[0]),
        "ssm_out_w": ssm_out_w[0].astype(BF16),
        "gate_b": row(gate_b[0]),
        "w_o": w_o[0].astype(BF16),
        "norm_ffn_w": row(norm_ffn_w[0]),
        "router_w": pad_lanes(jnp.concatenate([router_expert_w[0], router_group_w[0]], axis=1)),
        "router_b": pad_lanes(row(jnp.concatenate([router_expert_b[0], router_group_b[0]]))),
        "exp_w1": exp_w1[0].astype(BF16),
        "exp_w3": exp_w3[0].astype(BF16),
        "exp_w2": exp_w2[0].astype(BF16),
        "norm_final_w": row(norm_final_w),
    }


def kernel(x_prompt, x_sample, state_conf_conv, state_ssm_conv, state_ssm, meta_tokens, norm_mix_w, w_in, conf_dw_w, conf_dw_b, conf_ln_g, conf_ln_b, conf_out_w, conf_out_b, ssm_conv_w, ssm_conv_b, dt_bias, a_log, d_skip, ssm_norm_w, ssm_out_w, gate_b, w_o, norm_ffn_w, router_group_w, router_group_b, router_expert_w, router_expert_b, exp_w1, exp_w3, exp_w2, norm_final_w):
    assert norm_mix_w.shape[0] == 1, "single-layer trunk"
    b, l, d = x_prompt.shape
    nb, n_new, _ = x_sample.shape
    assert l % TILE == 0 and n_new <= SAMPLE_ROWS and (nb * SAMPLE_ROWS) % TILE == 0
    p = _prep_params(norm_mix_w, w_in, conf_dw_w, conf_dw_b, conf_ln_g, conf_ln_b, conf_out_w, conf_out_b,
                     ssm_conv_w, ssm_conv_b, dt_bias, a_log, d_skip, ssm_norm_w, ssm_out_w, gate_b, w_o,
                     norm_ffn_w, router_group_w, router_group_b, router_expert_w, router_expert_b, exp_w1, exp_w3,
                     exp_w2, norm_final_w)

    head = jnp.concatenate([jnp.zeros((TILE - N_META, d), F32), meta_tokens.astype(F32)], axis=0)
    ya_p, new_conf_p = _conf_prompt(head, x_prompt, p)
    yb_p, new_sconv_p, new_ssm_p = _ssm_prompt(head, x_prompt, p)

    xs_rows = jnp.pad(x_sample, ((0, 0), (0, SAMPLE_ROWS - n_new), (0, 0))).reshape(nb * SAMPLE_ROWS, d)
    conf_pad = jnp.pad(state_conf_conv[0], ((0, 0), (CONF_PAD - (CONF_K - 1), 0), (0, 0)))
    sconv_pad = jnp.pad(state_ssm_conv[0], ((0, 0), (SCONV_PAD - (SSM_K - 1), 0), (0, 0)))
    ya_s, new_conf_s = _conf_sample(xs_rows, conf_pad, p, n_new)
    yb_s, new_sconv_s, new_ssm_s = _ssm_sample(xs_rows, sconv_pad, state_ssm[0], p, n_new)
    unpad = lambda v: v.reshape(nb, SAMPLE_ROWS, d)[:, :n_new].reshape(nb * n_new, d)

    outs = []
    for x, ya, yb in ((x_prompt.reshape(b * l, d), ya_p.reshape(b * l, d), yb_p.reshape(b * l, d)),
                      (x_sample.reshape(nb * n_new, d), unpad(ya_s), unpad(yb_s))):
        tm = 512 if x.shape[0] % 512 == 0 else TILE
        x1, xn2, comb = _merge(x, ya, yb, p, tm)
        outs.append(_moe(xn2, comb, x1, p, tm))
    y_prompt = outs[0].reshape(b, l, d)
    y_sample = outs[1].reshape(nb, n_new, d)
    return (y_prompt, y_sample, new_conf_p[None], new_sconv_p[None], new_ssm_p[None],
            new_conf_s[None], new_sconv_s[None], new_ssm_s[None])
```

```python
import functools

import jax
import jax.numpy as jnp
from jax import lax
from jax.experimental import pallas as pl
from jax.experimental.pallas import tpu as pltpu

F32 = jnp.float32
BF16 = jnp.bfloat16
EPS = 1e-6

LANES = 128
SUBLANES = 8
TILE = 128
VMEM_LIMIT = 56 * 1024 * 1024

N_META = 16
CONF_K = 31
SSM_K = 4
HEAD_DIM = 64
N_STATE = 128
N_GROUPS = 8
HEADS_PER_GROUP = 4
GROUP_W = HEADS_PER_GROUP * HEAD_DIM
N_EXPERTS = 32
EXPERTS_PER_GROUP = 8
N_EXPERT_GROUPS = 4
SAMPLE_ROWS = 8
CONF_PAD = 32
SCONV_PAD = 8


def _rms(x, w):
    return x * lax.rsqrt(jnp.mean(x * x, axis=-1, keepdims=True) + EPS) * w


def _silu(x):
    return x * jax.nn.sigmoid(x)


def _dot(a, b):
    return jnp.dot(a, b, preferred_element_type=F32)


def _dot_nt(a, b):
    return lax.dot_general(a, b, (((1,), (1,)), ((), ())), preferred_element_type=F32)


def _dot_tn(a, b):
    return lax.dot_general(a, b, (((0,), (0,)), ((), ())), preferred_element_type=F32)


def _const_spec(shape):
    zeros = (0,) * len(shape)
    return pl.BlockSpec(shape, lambda *_: zeros, pipeline_mode=pl.Buffered(1))


def _dwconv(src_ref, base, rows, w_ref, b_ref, taps, width, emit):
    for c in range(width // LANES):
        cs = slice(c * LANES, (c + 1) * LANES)
        acc = jnp.broadcast_to(b_ref[:, cs], (rows, LANES))
        for r in range(min(SUBLANES, taps)):
            qs = range((taps - r + SUBLANES - 1) // SUBLANES)
            slab = src_ref[pl.ds(base + r, rows + SUBLANES * (len(qs) - 1)), cs]
            part = w_ref[r:r + 1, cs] * slab[0:rows]
            for q in qs[1:]:
                k = SUBLANES * q + r
                part = part + w_ref[k:k + 1, cs] * slab[SUBLANES * q:SUBLANES * q + rows]
            acc = acc + part
        emit(cs, acc)


def _conf_tail(conv_ref, ln_g_ref, ln_b_ref, wout_ref, bout_ref, ya_ref):
    c = conv_ref[...]
    mu = jnp.mean(c, axis=-1, keepdims=True)
    d = c - mu
    var = jnp.mean(d * d, axis=-1, keepdims=True)
    y = d * lax.rsqrt(var + EPS) * ln_g_ref[...] + ln_b_ref[...]
    y = _silu(y)
    ya_ref[...] = _dot(y.astype(BF16), wout_ref[...]) + bout_ref[...]


def _conf_glu(x, nw_ref, wvg_ref):
    d = x.shape[-1]
    xn = _rms(x, nw_ref[...]).astype(BF16)
    vg = _dot(xn, wvg_ref[...])
    return vg[:, :d] * jax.nn.sigmoid(vg[:, d:])


def _conf_prompt_kernel(head_ref, x_ref, nw_ref, wvg_ref, dww_ref, dwb_ref, lng_ref, lnb_ref, wout_ref, bout_ref,
                        ya_ref, nc_ref, afull_ref, conv_ref):
    j = pl.program_id(1)
    d = x_ref.shape[-1]

    @pl.when(j == 0)
    def _():
        afull_ref[0:CONF_PAD, :] = jnp.zeros((CONF_PAD, d), F32)

    x = jnp.where(j == 0, head_ref[...], x_ref[...])
    afull_ref[CONF_PAD:CONF_PAD + TILE, :] = _conf_glu(x, nw_ref, wvg_ref)

    def emit(cs, v):
        conv_ref[:, cs] = v

    _dwconv(afull_ref, CONF_PAD - (CONF_K - 1), TILE, dww_ref, dwb_ref, CONF_K, d, emit)
    nc_ref[...] = afull_ref[pl.ds(CONF_PAD + TILE - (CONF_K - 1), CONF_K - 1), :]
    afull_ref[0:CONF_PAD, :] = afull_ref[TILE:TILE + CONF_PAD, :]
    _conf_tail(conv_ref, lng_ref, lnb_ref, wout_ref, bout_ref, ya_ref)


def _conf_sample_kernel(x_ref, st_ref, nw_ref, wvg_ref, dww_ref, dwb_ref, lng_ref, lnb_ref, wout_ref, bout_ref,
                        ya_ref, nc_ref, a_ref, cs_ref, conv_ref, *, n_new):
    d = x_ref.shape[-1]
    n_seq = TILE // SAMPLE_ROWS
    a_ref[...] = _conf_glu(x_ref[...], nw_ref, wvg_ref)

    def seq_body(s, carry):
        r0 = pl.multiple_of(s * SAMPLE_ROWS, SAMPLE_ROWS)
        cs_ref[0:CONF_PAD, :] = st_ref[s]
        cs_ref[CONF_PAD:CONF_PAD + SAMPLE_ROWS, :] = a_ref[pl.ds(r0, SAMPLE_ROWS), :]

        def emit(cs, v):
            conv_ref[pl.ds(r0, SAMPLE_ROWS), cs] = v

        _dwconv(cs_ref, CONF_PAD - (CONF_K - 1), SAMPLE_ROWS, dww_ref, dwb_ref, CONF_K, d, emit)
        nc_ref[s] = cs_ref[pl.ds(CONF_PAD + n_new - (CONF_K - 1), CONF_K - 1), :]
        return carry

    lax.fori_loop(0, n_seq, seq_body, 0)
    _conf_tail(conv_ref, lng_ref, lnb_ref, wout_ref, bout_ref, ya_ref)


def _conf_weights(p):
    d = p["norm_w"].shape[-1]
    return [p["norm_w"], p["w_vg"], p["conf_dw_w"], p["conf_dw_b"], p["conf_ln_g"], p["conf_ln_b"],
            p["conf_out_w"], p["conf_out_b"]], [
        _const_spec((1, d)), _const_spec((d, 2 * d)), _const_spec((CONF_PAD, d)), _const_spec((1, d)),
        _const_spec((1, d)), _const_spec((1, d)), _const_spec((d, d)), _const_spec((1, d))]


def _conf_prompt(head, x_prompt, p):
    b, l, d = x_prompt.shape
    nt = l // TILE + 1
    weights, wspecs = _conf_weights(p)
    xmap = lambda i, j: (i, jnp.maximum(j - 1, 0), 0)
    return pl.pallas_call(
        _conf_prompt_kernel,
        grid=(b, nt),
        in_specs=[_const_spec((TILE, d)), pl.BlockSpec((None, TILE, d), xmap)] + wspecs,
        out_specs=[pl.BlockSpec((None, TILE, d), xmap),
                   pl.BlockSpec((None, CONF_K - 1, d), lambda i, j: (i, 0, 0))],
        out_shape=[jax.ShapeDtypeStruct((b, l, d), F32), jax.ShapeDtypeStruct((b, CONF_K - 1, d), F32)],
        scratch_shapes=[pltpu.VMEM((CONF_PAD + TILE, d), F32), pltpu.VMEM((TILE, d), F32)],
        compiler_params=pltpu.CompilerParams(dimension_semantics=("arbitrary", "arbitrary"),
                                             vmem_limit_bytes=VMEM_LIMIT),
        name="conf_prompt",
    )(head, x_prompt, *weights)


def _conf_sample(x_rows, state_pad, p, n_new):
    rows, d = x_rows.shape
    n_seq = TILE // SAMPLE_ROWS
    weights, wspecs = _conf_weights(p)
    return pl.pallas_call(
        functools.partial(_conf_sample_kernel, n_new=n_new),
        grid=(rows // TILE,),
        in_specs=[pl.BlockSpec((TILE, d), lambda i: (i, 0)),
                  pl.BlockSpec((n_seq, CONF_PAD, d), lambda i: (i, 0, 0))] + wspecs,
        out_specs=[pl.BlockSpec((TILE, d), lambda i: (i, 0)),
                   pl.BlockSpec((n_seq, CONF_K - 1, d), lambda i: (i, 0, 0))],
        out_shape=[jax.ShapeDtypeStruct((rows, d), F32),
                   jax.ShapeDtypeStruct((rows // SAMPLE_ROWS, CONF_K - 1, d), F32)],
        scratch_shapes=[pltpu.VMEM((TILE, d), F32), pltpu.VMEM((CONF_PAD + SAMPLE_ROWS, d), F32),
                        pltpu.VMEM((TILE, d), F32)],
        compiler_params=pltpu.CompilerParams(dimension_semantics=("arbitrary",), vmem_limit_bytes=VMEM_LIMIT),
        name="conf_sample",
    )(x_rows, state_pad, *weights)


def _ssm_project(x, nw_ref, wzx_ref, wdt_ref, d_inner):
    xn = _rms(x, nw_ref[...]).astype(BF16)
    z = _dot(xn, wzx_ref[:, 0:d_inner])
    xbc = _dot(xn, wzx_ref[:, d_inner:])
    dtr = _dot(xn, wdt_ref[...])
    return z, xbc, dtr


def _seg_cumsum(x, row, seg_len, reverse):
    pos = row % seg_len
    n = x.shape[0]
    step = 1
    while step < seg_len:
        if reverse:
            x = x + jnp.where(pos < seg_len - step, pltpu.roll(x, n - step, axis=0), 0.0)
        else:
            x = x + jnp.where(pos >= step, pltpu.roll(x, step, axis=0), 0.0)
        step *= 2
    return x


def _expand_heads(m, g, lane):
    rows = m.shape[0]
    cols = [jnp.broadcast_to(m[:, HEADS_PER_GROUP * g + i:HEADS_PER_GROUP * g + i + 1], (rows, LANES))
            for i in range(HEADS_PER_GROUP)]
    lo = jnp.where(lane < HEAD_DIM, cols[0], cols[1])
    hi = jnp.where(lane < HEAD_DIM, cols[2], cols[3])
    return jnp.concatenate([lo, hi], axis=1)


def _ssd_tile(z_ref, xc_ref, dtr_ref, valid, seg_len, n_seq, h_in, h_out, dtb_ref, alog_ref, dskip_ref,
              normw_ref, wout_ref, yb_ref):
    d_inner = z_ref.shape[-1]
    d_model = wout_ref.shape[-1]
    row = lax.broadcasted_iota(jnp.int32, (TILE, TILE), 0)
    col = lax.broadcasted_iota(jnp.int32, (TILE, TILE), 1)
    causal = (col <= row) & ((col // seg_len) == (row // seg_len))
    lane = lax.broadcasted_iota(jnp.int32, (TILE, LANES), 1)
    lane_g = lax.broadcasted_iota(jnp.int32, (TILE, GROUP_W), 1)

    xdt = dtr_ref[...] + dtb_ref[...]
    dt = (jnp.maximum(xdt, 0.0) + jnp.log1p(jnp.exp(-jnp.abs(xdt)))) * valid
    da = dt * (-jnp.exp(alog_ref[...]))
    cum = _seg_cumsum(da, row, seg_len, reverse=False)
    tail = jnp.exp(_seg_cumsum(da, row, seg_len, reverse=True) - da) * dt
    ecum = jnp.exp(cum)
    cum_t = cum.T
    dt_t = dt.T

    acc = jnp.zeros((TILE, d_model), F32)
    for g in range(N_GROUPS):
        xs = xc_ref[:, g * GROUP_W:(g + 1) * GROUP_W]
        bm = xc_ref[:, d_inner + g * N_STATE:d_inner + (g + 1) * N_STATE]
        cm = xc_ref[:, d_inner + (N_GROUPS + g) * N_STATE:d_inner + (N_GROUPS + g + 1) * N_STATE]
        bm16 = bm.astype(BF16)
        cm16 = cm.astype(BF16)
        cb = _dot_nt(cm16, bm16)
        yg = jnp.zeros((TILE, GROUP_W), F32)
        for i in range(HEADS_PER_GROUP):
            h = HEADS_PER_GROUP * g + i
            seg = cum[:, h:h + 1] - cum_t[h:h + 1, :]
            decay = jnp.exp(jnp.where(causal, seg, -jnp.inf))
            w = (cb * decay * dt_t[h:h + 1, :]).astype(BF16)
            xm = jnp.where((lane_g // HEAD_DIM) == i, xs, 0.0).astype(BF16)
            yg = yg + _dot(w, xm)
        ecum_g = _expand_heads(ecum, g, lane)
        xw16 = (xs * _expand_heads(tail, g, lane)).astype(BF16)
        ystate = []
        for s in range(n_seq):
            rs = slice(s * seg_len, (s + 1) * seg_len)
            last = (s + 1) * seg_len - 1
            hst = h_in(s, g)
            ystate.append(_dot_nt(cm16[rs], hst.astype(BF16)))
            upd = _dot_tn(xw16[rs], bm16[rs])
            dec = jnp.concatenate(
                [jnp.broadcast_to(jnp.exp(cum_t[HEADS_PER_GROUP * g + i:HEADS_PER_GROUP * g + i + 1, last:last + 1]),
                                  (HEAD_DIM, N_STATE)) for i in range(HEADS_PER_GROUP)], axis=0)
            h_out(s, g, hst * dec + upd)
        if n_seq * seg_len < TILE:
            ystate.append(jnp.zeros((TILE - n_seq * seg_len, GROUP_W), F32))
        ys = ystate[0] if len(ystate) == 1 else jnp.concatenate(ystate, axis=0)
        yg = yg + ys * ecum_g
        zg = z_ref[:, g * GROUP_W:(g + 1) * GROUP_W]
        v = (yg + dskip_ref[:, g * GROUP_W:(g + 1) * GROUP_W] * xs) * _silu(zg)
        v = v * lax.rsqrt(jnp.mean(v * v, axis=-1, keepdims=True) + EPS) * normw_ref[:, g * GROUP_W:(g + 1) * GROUP_W]
        acc = acc + _dot(v.astype(BF16), wout_ref[g * GROUP_W:(g + 1) * GROUP_W, :])
    yb_ref[...] = acc


def _ssm_prompt_kernel(head_ref, x_ref, nw_ref, wzx_ref, wdt_ref, cw_ref, cb_ref, dtb_ref, alog_ref, dskip_ref,
                       normw_ref, wout_ref, yb_ref, ns_ref, hs_ref, z_ref, cfull_ref, xc_ref, dtr_ref):
    j = pl.program_id(1)
    d_inner = z_ref.shape[-1]
    conv_dim = xc_ref.shape[-1]

    @pl.when(j == 0)
    def _():
        cfull_ref[0:SCONV_PAD, :] = jnp.zeros((SCONV_PAD, conv_dim), F32)
        hs_ref[...] = jnp.zeros(hs_ref.shape, F32)

    x = jnp.where(j == 0, head_ref[...], x_ref[...])
    z, xbc, dtr = _ssm_project(x, nw_ref, wzx_ref, wdt_ref, d_inner)
    z_ref[...] = z
    dtr_ref[...] = dtr
    cfull_ref[SCONV_PAD:SCONV_PAD + TILE, :] = xbc

    def emit(cs, v):
        xc_ref[:, cs] = _silu(v)

    _dwconv(cfull_ref, SCONV_PAD - (SSM_K - 1), TILE, cw_ref, cb_ref, SSM_K, conv_dim, emit)
    ns_ref[...] = cfull_ref[pl.ds(SCONV_PAD + TILE - (SSM_K - 1), SSM_K - 1), :]
    cfull_ref[0:SCONV_PAD, :] = cfull_ref[TILE:TILE + SCONV_PAD, :]

    rowv = lax.broadcasted_iota(jnp.int32, (TILE, 1), 0)
    valid = jnp.where((j == 0) & (rowv < TILE - N_META), 0.0, 1.0)

    def h_in(s, g):
        return hs_ref[HEADS_PER_GROUP * g:HEADS_PER_GROUP * (g + 1)].reshape(GROUP_W, N_STATE)

    def h_out(s, g, v):
        hs_ref[HEADS_PER_GROUP * g:HEADS_PER_GROUP * (g + 1)] = v.reshape(HEADS_PER_GROUP, HEAD_DIM, N_STATE)

    _ssd_tile(z_ref, xc_ref, dtr_ref, valid, TILE, 1, h_in, h_out, dtb_ref, alog_ref, dskip_ref, normw_ref,
              wout_ref, yb_ref)


def _ssm_sample_proj_kernel(x_ref, st_ref, nw_ref, wzx_ref, wdt_ref, cw_ref, cb_ref,
                            z_ref, xc_ref, dtr_ref, ns_ref, xbc_ref, cs_ref, *, n_new):
    d_inner = z_ref.shape[-1]
    conv_dim = xc_ref.shape[-1]
    n_seq = TILE // SAMPLE_ROWS
    z, xbc, dtr = _ssm_project(x_ref[...], nw_ref, wzx_ref, wdt_ref, d_inner)
    z_ref[...] = z
    dtr_ref[...] = dtr
    xbc_ref[...] = xbc

    def seq_body(s, carry):
        r0 = pl.multiple_of(s * SAMPLE_ROWS, SAMPLE_ROWS)
        cs_ref[0:SCONV_PAD, :] = st_ref[s]
        cs_ref[SCONV_PAD:SCONV_PAD + SAMPLE_ROWS, :] = xbc_ref[pl.ds(r0, SAMPLE_ROWS), :]

        def emit(cs, v):
            xc_ref[pl.ds(r0, SAMPLE_ROWS), cs] = _silu(v)

        _dwconv(cs_ref, SCONV_PAD - (SSM_K - 1), SAMPLE_ROWS, cw_ref, cb_ref, SSM_K, conv_dim, emit)
        ns_ref[s] = cs_ref[pl.ds(SCONV_PAD + n_new - (SSM_K - 1), SSM_K - 1), :]
        return carry

    lax.fori_loop(0, n_seq, seq_body, 0)


def _ssm_sample_ssd_kernel(z_in, xc_in, dtr_in, hin_ref, dtb_ref, alog_ref, dskip_ref, normw_ref, wout_ref,
                           yb_ref, hout_ref, z_ref, xc_ref, dtr_ref, ybfull_ref, *, n_new, n_seq):
    rows = n_seq * SAMPLE_ROWS
    for src, dst in ((z_in, z_ref), (xc_in, xc_ref), (dtr_in, dtr_ref)):
        dst[0:rows, :] = src[...]
        dst[rows:TILE, :] = jnp.zeros((TILE - rows, dst.shape[-1]), F32)
    rowv = lax.broadcasted_iota(jnp.int32, (TILE, 1), 0)
    valid = jnp.where((rowv < rows) & ((rowv % SAMPLE_ROWS) < n_new), 1.0, 0.0)

    def h_in(s, g):
        return hin_ref[s, HEADS_PER_GROUP * g:HEADS_PER_GROUP * (g + 1)].reshape(GROUP_W, N_STATE)

    def h_out(s, g, v):
        hout_ref[s, HEADS_PER_GROUP * g:HEADS_PER_GROUP * (g + 1)] = v.reshape(HEADS_PER_GROUP, HEAD_DIM, N_STATE)

    _ssd_tile(z_ref, xc_ref, dtr_ref, valid, SAMPLE_ROWS, n_seq, h_in, h_out, dtb_ref, alog_ref, dskip_ref,
              normw_ref, wout_ref, ybfull_ref)
    yb_ref[...] = ybfull_ref[0:rows, :]


def _ssm_prompt(head, x_prompt, p):
    b, l, d = x_prompt.shape
    nt = l // TILE + 1
    d_inner = p["ssm_norm_w"].shape[-1]
    conv_dim = p["ssm_conv_b"].shape[-1]
    n_heads = d_inner // HEAD_DIM
    xmap = lambda i, j: (i, jnp.maximum(j - 1, 0), 0)
    weights = [p["norm_w"], p["w_zx"], p["w_dt"], p["ssm_conv_w"], p["ssm_conv_b"], p["dt_bias"], p["a_log"],
               p["d_skip"], p["ssm_norm_w"], p["ssm_out_w"]]
    wspecs = [_const_spec(w.shape) for w in weights]
    return pl.pallas_call(
        _ssm_prompt_kernel,
        grid=(b, nt),
        in_specs=[_const_spec((TILE, d)), pl.BlockSpec((None, TILE, d), xmap)] + wspecs,
        out_specs=[pl.BlockSpec((None, TILE, d), xmap),
                   pl.BlockSpec((None, SSM_K - 1, conv_dim), lambda i, j: (i, 0, 0)),
                   pl.BlockSpec((None, n_heads, HEAD_DIM, N_STATE), lambda i, j: (i, 0, 0, 0))],
        out_shape=[jax.ShapeDtypeStruct((b, l, d), F32), jax.ShapeDtypeStruct((b, SSM_K - 1, conv_dim), F32),
                   jax.ShapeDtypeStruct((b, n_heads, HEAD_DIM, N_STATE), F32)],
        scratch_shapes=[pltpu.VMEM((TILE, d_inner), F32), pltpu.VMEM((SCONV_PAD + TILE, conv_dim), F32),
                        pltpu.VMEM((TILE, conv_dim), F32), pltpu.VMEM((TILE, LANES), F32)],
        compiler_params=pltpu.CompilerParams(dimension_semantics=("arbitrary", "arbitrary"),
                                             vmem_limit_bytes=VMEM_LIMIT),
        name="ssm_prompt",
    )(head, x_prompt, *weights)


def _ssm_sample(x_rows, sconv_pad, h0, p, n_new):
    rows, d = x_rows.shape
    d_inner = p["ssm_norm_w"].shape[-1]
    conv_dim = p["ssm_conv_b"].shape[-1]
    n_heads = d_inner // HEAD_DIM
    n_seq_proj = TILE // SAMPLE_ROWS
    weights = [p["norm_w"], p["w_zx"], p["w_dt"], p["ssm_conv_w"], p["ssm_conv_b"]]
    z, xc, dtr, new_sconv = pl.pallas_call(
        functools.partial(_ssm_sample_proj_kernel, n_new=n_new),
        grid=(rows // TILE,),
        in_specs=[pl.BlockSpec((TILE, d), lambda i: (i, 0)),
                  pl.BlockSpec((n_seq_proj, SCONV_PAD, conv_dim), lambda i: (i, 0, 0))]
        + [_const_spec(w.shape) for w in weights],
        out_specs=[pl.BlockSpec((TILE, d_inner), lambda i: (i, 0)), pl.BlockSpec((TILE, conv_dim), lambda i: (i, 0)),
                   pl.BlockSpec((TILE, LANES), lambda i: (i, 0)),
                   pl.BlockSpec((n_seq_proj, SSM_K - 1, conv_dim), lambda i: (i, 0, 0))],
        out_shape=[jax.ShapeDtypeStruct((rows, d_inner), F32), jax.ShapeDtypeStruct((rows, conv_dim), F32),
                   jax.ShapeDtypeStruct((rows, LANES), F32),
                   jax.ShapeDtypeStruct((rows // SAMPLE_ROWS, SSM_K - 1, conv_dim), F32)],
        scratch_shapes=[pltpu.VMEM((TILE, conv_dim), F32), pltpu.VMEM((SCONV_PAD + SAMPLE_ROWS, conv_dim), F32)],
        compiler_params=pltpu.CompilerParams(dimension_semantics=("arbitrary",), vmem_limit_bytes=VMEM_LIMIT),
        name="ssm_sample_proj",
    )(x_rows, sconv_pad, *weights)

    n_seq = 4
    r = n_seq * SAMPLE_ROWS
    weights = [p["dt_bias"], p["a_log"], p["d_skip"], p["ssm_norm_w"], p["ssm_out_w"]]
    yb, h_new = pl.pallas_call(
        functools.partial(_ssm_sample_ssd_kernel, n_new=n_new, n_seq=n_seq),
        grid=(rows // r,),
        in_specs=[pl.BlockSpec((r, d_inner), lambda i: (i, 0)), pl.BlockSpec((r, conv_dim), lambda i: (i, 0)),
                  pl.BlockSpec((r, LANES), lambda i: (i, 0)),
                  pl.BlockSpec((n_seq, n_heads, HEAD_DIM, N_STATE), lambda i: (i, 0, 0, 0))]
        + [_const_spec(w.shape) for w in weights],
        out_specs=[pl.BlockSpec((r, d), lambda i: (i, 0)),
                   pl.BlockSpec((n_seq, n_heads, HEAD_DIM, N_STATE), lambda i: (i, 0, 0, 0))],
        out_shape=[jax.ShapeDtypeStruct((rows, d), F32), jax.ShapeDtypeStruct(h0.shape, F32)],
        scratch_shapes=[pltpu.VMEM((TILE, d_inner), F32), pltpu.VMEM((TILE, conv_dim), F32),
                        pltpu.VMEM((TILE, LANES), F32), pltpu.VMEM((TILE, d), F32)],
        compiler_params=pltpu.CompilerParams(dimension_semantics=("arbitrary",), vmem_limit_bytes=VMEM_LIMIT),
        name="ssm_sample_ssd",
    )(z, xc, dtr, h0, *weights)
    return yb, new_sconv, h_new


def _merge_kernel(x_ref, ya_ref, yb_ref, nw_ref, wg_ref, gb_ref, wo_ref, nfw_ref, rwh_ref, rwl_ref, rb_ref,
                  x1_ref, xn2_ref, comb_ref):
    d = x_ref.shape[-1]
    x = x_ref[...]
    xn = _rms(x, nw_ref[...]).astype(BF16)
    gates = jax.nn.sigmoid(_dot(xn, wg_ref[...]) + gb_ref[...])
    merged = gates[:, :d] * ya_ref[...] + gates[:, d:] * yb_ref[...]
    x1 = x + _dot(merged.astype(BF16), wo_ref[...])
    x1_ref[...] = x1
    xn2 = _rms(x1, nfw_ref[...])
    xn2_ref[...] = xn2.astype(BF16)

    x_hi = xn2.astype(BF16)
    x_lo = (xn2 - x_hi.astype(F32)).astype(BF16)
    logits = _dot(x_hi, rwh_ref[...]) + _dot(x_lo, rwh_ref[...]) + _dot(x_hi, rwl_ref[...]) + rb_ref[...]
    rows = logits.shape[0]
    lane = lax.broadcasted_iota(jnp.int32, (rows, LANES), 1)
    neg = -jnp.inf
    is_g = (lane >= N_EXPERTS) & (lane < N_EXPERTS + N_EXPERT_GROUPS)
    gl = jnp.where(is_g, logits, neg)
    gmax = jnp.max(gl, axis=-1, keepdims=True)
    gsel = jnp.min(jnp.where(gl == gmax, lane, LANES), axis=-1, keepdims=True) - N_EXPERTS
    gprob = 1.0 / jnp.sum(jnp.exp(gl - gmax), axis=-1, keepdims=True)
    el = jnp.where((lane < N_EXPERTS) & ((lane // EXPERTS_PER_GROUP) == gsel), logits, neg)
    m1 = jnp.max(el, axis=-1, keepdims=True)
    i1 = jnp.min(jnp.where(el == m1, lane, LANES), axis=-1, keepdims=True)
    el2 = jnp.where(lane == i1, neg, el)
    m2 = jnp.max(el2, axis=-1, keepdims=True)
    i2 = jnp.min(jnp.where(el2 == m2, lane, LANES), axis=-1, keepdims=True)
    e2 = jnp.exp(m2 - m1)
    den = 1.0 + e2
    comb_ref[...] = jnp.where(lane == i1, (1.0 / den) * gprob, jnp.where(lane == i2, (e2 / den) * gprob, 0.0))


def _merge(x, ya, yb, p, tm):
    rows, d = x.shape
    weights = [p["norm_w"], p["w_gate"], p["gate_b"], p["w_o"], p["norm_ffn_w"], p["router_w_hi"],
               p["router_w_lo"], p["router_b"]]
    row_spec = pl.BlockSpec((tm, d), lambda i: (i, 0))
    return pl.pallas_call(
        _merge_kernel,
        grid=(rows // tm,),
        in_specs=[row_spec, row_spec, row_spec] + [_const_spec(w.shape) for w in weights],
        out_specs=[row_spec, row_spec, pl.BlockSpec((tm, LANES), lambda i: (i, 0))],
        out_shape=[jax.ShapeDtypeStruct((rows, d), F32), jax.ShapeDtypeStruct((rows, d), BF16),
                   jax.ShapeDtypeStruct((rows, LANES), F32)],
        compiler_params=pltpu.CompilerParams(dimension_semantics=("arbitrary",), vmem_limit_bytes=VMEM_LIMIT),
        name="merge_router",
    )(x, ya, yb, *weights)


def _moe_kernel(xn_ref, comb_ref, x1_ref, w1_ref, w3_ref, w2_ref, nw_ref, y_ref, acc_ref):
    g = pl.program_id(1)
    xn = xn_ref[...]
    comb = comb_ref[...]
    lane = lax.broadcasted_iota(jnp.int32, comb.shape, 1)
    acts = []
    for e in range(EXPERTS_PER_GROUP):
        c = jnp.sum(jnp.where(lane == g * EXPERTS_PER_GROUP + e, comb, 0.0), axis=-1, keepdims=True)
        act = _silu(_dot(xn, w1_ref[e])) * _dot(xn, w3_ref[e]) * c
        acts.append(act.astype(BF16))
    contrib = _dot(jnp.concatenate(acts, axis=1), w2_ref[...])

    @pl.when(g == 0)
    def _():
        acc_ref[...] = contrib

    @pl.when(g > 0)
    def _():
        acc_ref[...] += contrib

    @pl.when(g == pl.num_programs(1) - 1)
    def _():
        y_ref[...] = _rms(x1_ref[...] + acc_ref[...], nw_ref[...])


def _moe(xn2, comb, x1, p, tm):
    rows, d = x1.shape
    n_e, _, d_e = p["exp_w1"].shape
    n_g = n_e // EXPERTS_PER_GROUP
    w2 = p["exp_w2"].reshape(n_g, EXPERTS_PER_GROUP * d_e, d)
    row = lambda i, g: (i, 0)
    return pl.pallas_call(
        _moe_kernel,
        grid=(rows // tm, n_g),
        in_specs=[pl.BlockSpec((tm, d), row), pl.BlockSpec((tm, LANES), row), pl.BlockSpec((tm, d), row),
                  pl.BlockSpec((EXPERTS_PER_GROUP, d, d_e), lambda i, g: (g, 0, 0)),
                  pl.BlockSpec((EXPERTS_PER_GROUP, d, d_e), lambda i, g: (g, 0, 0)),
                  pl.BlockSpec((None, EXPERTS_PER_GROUP * d_e, d), lambda i, g: (g, 0, 0)),
                  _const_spec((1, d))],
        out_specs=pl.BlockSpec((tm, d), row),
        out_shape=jax.ShapeDtypeStruct((rows, d), F32),
        scratch_shapes=[pltpu.VMEM((tm, d), F32)],
        compiler_params=pltpu.CompilerParams(dimension_semantics=("arbitrary", "arbitrary"),
                                             vmem_limit_bytes=VMEM_LIMIT),
        name="moe",
    )(xn2, comb, x1, p["exp_w1"], p["exp_w3"], w2, p["norm_final_w"])


def _prep_params(norm_mix_w, w_in, conf_dw_w, conf_dw_b, conf_ln_g, conf_ln_b, conf_out_w, conf_out_b, ssm_conv_w,
                 ssm_conv_b, dt_bias, a_log, d_skip, ssm_norm_w, ssm_out_w, gate_b, w_o, norm_ffn_w,
                 router_group_w, router_group_b, router_expert_w, router_expert_b, exp_w1, exp_w3, exp_w2,
                 norm_final_w):
    d = norm_mix_w.shape[-1]
    d_inner = ssm_norm_w.shape[-1]
    conv_dim = ssm_conv_b.shape[-1]
    n_heads = dt_bias.shape[-1]
    s1, s2 = d, 2 * d
    s3 = s2 + d_inner
    s4 = s3 + conv_dim
    s5 = s4 + n_heads
    w = w_in[0]
    row = lambda v: v.reshape(1, -1).astype(F32)
    pad_lanes = lambda v: jnp.pad(v, ((0, 0), (0, LANES - v.shape[-1])))
    router_w = pad_lanes(jnp.concatenate([router_expert_w[0], router_group_w[0]], axis=1))
    router_w_hi = router_w.astype(BF16)
    return {
        "norm_w": row(norm_mix_w[0]),
        "w_vg": w[:, :s2].astype(BF16),
        "w_zx": w[:, s2:s4].astype(BF16),
        "w_dt": pad_lanes(w[:, s4:s5]).astype(BF16),
        "w_gate": w[:, s5:].astype(BF16),
        "conf_dw_w": jnp.pad(conf_dw_w[0], ((0, CONF_PAD - CONF_K), (0, 0))),
        "conf_dw_b": row(conf_dw_b[0]),
        "conf_ln_g": row(conf_ln_g[0]),
        "conf_ln_b": row(conf_ln_b[0]),
        "conf_out_w": conf_out_w[0].astype(BF16),
        "conf_out_b": row(conf_out_b[0]),
        "ssm_conv_w": jnp.pad(ssm_conv_w[0], ((0, SUBLANES - SSM_K), (0, 0))),
        "ssm_conv_b": row(ssm_conv_b[0]),
        "dt_bias": pad_lanes(row(dt_bias[0])),
        "a_log": pad_lanes(row(a_log[0])),
        "d_skip": row(jnp.repeat(d_skip[0], HEAD_DIM)),
        "ssm_norm_w": row(ssm_norm_w[0]),
        "ssm_out_w": ssm_out_w[0].astype(BF16),
        "gate_b": row(gate_b[0]),
        "w_o": w_o[0].astype(BF16),
        "norm_ffn_w": row(norm_ffn_w[0]),
        "router_w_hi": router_w_hi,
        "router_w_lo": (router_w - router_w_hi.astype(F32)).astype(BF16),
        "router_b": pad_lanes(row(jnp.concatenate([router_expert_b[0], router_group_b[0]]))),
        "exp_w1": exp_w1[0].astype(BF16),
        "exp_w3": exp_w3[0].astype(BF16),
        "exp_w2": exp_w2[0].astype(BF16),
        "norm_final_w": row(norm_final_w),
    }


def kernel(x_prompt, x_sample, state_conf_conv, state_ssm_conv, state_ssm, meta_tokens, norm_mix_w, w_in, conf_dw_w, conf_dw_b, conf_ln_g, conf_ln_b, conf_out_w, conf_out_b, ssm_conv_w, ssm_conv_b, dt_bias, a_log, d_skip, ssm_norm_w, ssm_out_w, gate_b, w_o, norm_ffn_w, router_group_w, router_group_b, router_expert_w, router_expert_b, exp_w1, exp_w3, exp_w2, norm_final_w):
    assert norm_mix_w.shape[0] == 1, "single-layer trunk"
    b, l, d = x_prompt.shape
    nb, n_new, _ = x_sample.shape
    assert l % TILE == 0 and n_new <= SAMPLE_ROWS and (nb * SAMPLE_ROWS) % TILE == 0
    p = _prep_params(norm_mix_w, w_in, conf_dw_w, conf_dw_b, conf_ln_g, conf_ln_b, conf_out_w, conf_out_b,
                     ssm_conv_w, ssm_conv_b, dt_bias, a_log, d_skip, ssm_norm_w, ssm_out_w, gate_b, w_o,
                     norm_ffn_w, router_group_w, router_group_b, router_expert_w, router_expert_b, exp_w1, exp_w3,
                     exp_w2, norm_final_w)

    head = jnp.concatenate([jnp.zeros((TILE - N_META, d), F32), meta_tokens.astype(F32)], axis=0)
    ya_p, new_conf_p = _conf_prompt(head, x_prompt, p)
    yb_p, new_sconv_p, new_ssm_p = _ssm_prompt(head, x_prompt, p)

    xs_rows = jnp.pad(x_sample, ((0, 0), (0, SAMPLE_ROWS - n_new), (0, 0))).reshape(nb * SAMPLE_ROWS, d)
    conf_pad = jnp.pad(state_conf_conv[0], ((0, 0), (CONF_PAD - (CONF_K - 1), 0), (0, 0)))
    sconv_pad = jnp.pad(state_ssm_conv[0], ((0, 0), (SCONV_PAD - (SSM_K - 1), 0), (0, 0)))
    ya_s, new_conf_s = _conf_sample(xs_rows, conf_pad, p, n_new)
    yb_s, new_sconv_s, new_ssm_s = _ssm_sample(xs_rows, sconv_pad, state_ssm[0], p, n_new)
    unpad = lambda v: v.reshape(nb, SAMPLE_ROWS, d)[:, :n_new].reshape(nb * n_new, d)

    outs = []
    for x, ya, yb in ((x_prompt.reshape(b * l, d), ya_p.reshape(b * l, d), yb_p.reshape(b * l, d)),
                      (x_sample.reshape(nb * n_new, d), unpad(ya_s), unpad(yb_s))):
        tm = 512 if x.shape[0] % 512 == 0 else TILE
        assert x.shape[0] % tm == 0
        x1, xn2, comb = _merge(x, ya, yb, p, tm)
        outs.append(_moe(xn2, comb, x1, p, tm))
    y_prompt = outs[0].reshape(b, l, d)
    y_sample = outs[1].reshape(nb, n_new, d)
    return (y_prompt, y_sample, new_conf_p[None], new_sconv_p[None], new_ssm_p[None],
            new_conf_s[None], new_sconv_s[None], new_ssm_s[None])
```

```python
import functools

import jax
import jax.numpy as jnp
from jax import lax
from jax.experimental import pallas as pl
from jax.experimental.pallas import tpu as pltpu

F32 = jnp.float32
BF16 = jnp.bfloat16
EPS = 1e-6

LANES = 128
SUBLANES = 8
TILE = 128
VMEM_LIMIT = 56 * 1024 * 1024

N_META = 16
CONF_K = 31
SSM_K = 4
HEAD_DIM = 64
N_STATE = 128
N_GROUPS = 8
HEADS_PER_GROUP = 4
GROUP_W = HEADS_PER_GROUP * HEAD_DIM
N_EXPERTS = 32
EXPERTS_PER_GROUP = 8
N_EXPERT_GROUPS = 4
SAMPLE_ROWS = 8
CONF_PAD = 32
SCONV_PAD = 8
XBC_GROUP_W = GROUP_W + 2 * N_STATE
ZX_GROUP_W = GROUP_W + XBC_GROUP_W
GSEL_LANE = 64


def _rms(x, w):
    return x * lax.rsqrt(jnp.mean(x * x, axis=-1, keepdims=True) + EPS) * w


def _sigmoid(x):
    return 0.5 * jnp.tanh(0.5 * x) + 0.5


def _silu(x):
    return x * _sigmoid(x)


def _dot(a, b):
    return jnp.dot(a, b, preferred_element_type=F32)


def _dot_nt(a, b):
    return lax.dot_general(a, b, (((1,), (1,)), ((), ())), preferred_element_type=F32)


def _dot_tn(a, b):
    return lax.dot_general(a, b, (((0,), (0,)), ((), ())), preferred_element_type=F32)


def _const_spec(shape):
    zeros = (0,) * len(shape)
    return pl.BlockSpec(shape, lambda *_: zeros, pipeline_mode=pl.Buffered(1))


def _dwconv(src_ref, base, rows, w_ref, b_ref, taps, width, emit, col0=0):
    for c in range(col0 // LANES, (col0 + width) // LANES):
        cs = slice(c * LANES, (c + 1) * LANES)
        acc = jnp.broadcast_to(b_ref[:, cs], (rows, LANES))
        for r in range(min(SUBLANES, taps)):
            qs = range((taps - r + SUBLANES - 1) // SUBLANES)
            slab = src_ref[pl.ds(base + r, rows + SUBLANES * (len(qs) - 1)), cs]
            part = w_ref[r:r + 1, cs] * slab[0:rows]
            for q in qs[1:]:
                k = SUBLANES * q + r
                part = part + w_ref[k:k + 1, cs] * slab[SUBLANES * q:SUBLANES * q + rows]
            acc = acc + part
        emit(cs, acc)


def _conf_tail(conv_ref, ln_g_ref, ln_b_ref, wout_ref, bout_ref, ya_ref):
    c = conv_ref[...]
    mu = jnp.mean(c, axis=-1, keepdims=True)
    d = c - mu
    var = jnp.mean(d * d, axis=-1, keepdims=True)
    y = d * lax.rsqrt(var + EPS) * ln_g_ref[...] + ln_b_ref[...]
    y = _silu(y)
    ya_ref[...] = _dot(y.astype(BF16), wout_ref[...]) + bout_ref[...]


def _conf_glu(x, nw_ref, wvg_ref):
    d = x.shape[-1]
    xn = _rms(x, nw_ref[...]).astype(BF16)
    vg = _dot(xn, wvg_ref[...])
    return vg[:, :d] * _sigmoid(vg[:, d:])


def _conf_prompt_kernel(head_ref, x_ref, nw_ref, wvg_ref, dww_ref, dwb_ref, lng_ref, lnb_ref, wout_ref, bout_ref,
                        ya_ref, nc_ref, afull_ref, conv_ref):
    j = pl.program_id(1)
    d = x_ref.shape[-1]

    @pl.when(j == 0)
    def _():
        afull_ref[0:CONF_PAD, :] = jnp.zeros((CONF_PAD, d), F32)

    x = jnp.where(j == 0, head_ref[...], x_ref[...])
    afull_ref[CONF_PAD:CONF_PAD + TILE, :] = _conf_glu(x, nw_ref, wvg_ref)

    def emit(cs, v):
        conv_ref[:, cs] = v

    _dwconv(afull_ref, CONF_PAD - (CONF_K - 1), TILE, dww_ref, dwb_ref, CONF_K, d, emit)
    nc_ref[...] = afull_ref[pl.ds(CONF_PAD + TILE - (CONF_K - 1), CONF_K - 1), :]
    afull_ref[0:CONF_PAD, :] = afull_ref[TILE:TILE + CONF_PAD, :]
    _conf_tail(conv_ref, lng_ref, lnb_ref, wout_ref, bout_ref, ya_ref)


def _conf_sample_kernel(x_ref, st_ref, nw_ref, wvg_ref, dww_ref, dwb_ref, lng_ref, lnb_ref, wout_ref, bout_ref,
                        ya_ref, nc_ref, a_ref, cs_ref, conv_ref, *, n_new):
    d = x_ref.shape[-1]
    n_seq = TILE // SAMPLE_ROWS
    a_ref[...] = _conf_glu(x_ref[...], nw_ref, wvg_ref)

    def seq_body(s, carry):
        r0 = pl.multiple_of(s * SAMPLE_ROWS, SAMPLE_ROWS)
        cs_ref[0:CONF_PAD, :] = st_ref[s]
        cs_ref[CONF_PAD:CONF_PAD + SAMPLE_ROWS, :] = a_ref[pl.ds(r0, SAMPLE_ROWS), :]

        def emit(cs, v):
            conv_ref[pl.ds(r0, SAMPLE_ROWS), cs] = v

        _dwconv(cs_ref, CONF_PAD - (CONF_K - 1), SAMPLE_ROWS, dww_ref, dwb_ref, CONF_K, d, emit)
        nc_ref[s] = cs_ref[pl.ds(CONF_PAD + n_new - (CONF_K - 1), CONF_K - 1), :]
        return carry

    lax.fori_loop(0, n_seq, seq_body, 0)
    _conf_tail(conv_ref, lng_ref, lnb_ref, wout_ref, bout_ref, ya_ref)


def _conf_weights(p):
    d = p["norm_w"].shape[-1]
    return [p["norm_w"], p["w_vg"], p["conf_dw_w"], p["conf_dw_b"], p["conf_ln_g"], p["conf_ln_b"],
            p["conf_out_w"], p["conf_out_b"]], [
        _const_spec((1, d)), _const_spec((d, 2 * d)), _const_spec((CONF_PAD, d)), _const_spec((1, d)),
        _const_spec((1, d)), _const_spec((1, d)), _const_spec((d, d)), _const_spec((1, d))]


def _conf_prompt(head, x_prompt, p):
    b, l, d = x_prompt.shape
    nt = l // TILE + 1
    weights, wspecs = _conf_weights(p)
    xmap = lambda i, j: (i, jnp.maximum(j - 1, 0), 0)
    return pl.pallas_call(
        _conf_prompt_kernel,
        grid=(b, nt),
        in_specs=[_const_spec((TILE, d)), pl.BlockSpec((None, TILE, d), xmap)] + wspecs,
        out_specs=[pl.BlockSpec((None, TILE, d), xmap),
                   pl.BlockSpec((None, CONF_K - 1, d), lambda i, j: (i, 0, 0))],
        out_shape=[jax.ShapeDtypeStruct((b, l, d), F32), jax.ShapeDtypeStruct((b, CONF_K - 1, d), F32)],
        scratch_shapes=[pltpu.VMEM((CONF_PAD + TILE, d), F32), pltpu.VMEM((TILE, d), F32)],
        compiler_params=pltpu.CompilerParams(dimension_semantics=("arbitrary", "arbitrary"),
                                             vmem_limit_bytes=VMEM_LIMIT),
        name="conf_prompt",
    )(head, x_prompt, *weights)


def _conf_sample(x_rows, state_pad, p, n_new):
    rows, d = x_rows.shape
    n_seq = TILE // SAMPLE_ROWS
    weights, wspecs = _conf_weights(p)
    return pl.pallas_call(
        functools.partial(_conf_sample_kernel, n_new=n_new),
        grid=(rows // TILE,),
        in_specs=[pl.BlockSpec((TILE, d), lambda i: (i, 0)),
                  pl.BlockSpec((n_seq, CONF_PAD, d), lambda i: (i, 0, 0))] + wspecs,
        out_specs=[pl.BlockSpec((TILE, d), lambda i: (i, 0)),
                   pl.BlockSpec((n_seq, CONF_K - 1, d), lambda i: (i, 0, 0))],
        out_shape=[jax.ShapeDtypeStruct((rows, d), F32),
                   jax.ShapeDtypeStruct((rows // SAMPLE_ROWS, CONF_K - 1, d), F32)],
        scratch_shapes=[pltpu.VMEM((TILE, d), F32), pltpu.VMEM((CONF_PAD + SAMPLE_ROWS, d), F32),
                        pltpu.VMEM((TILE, d), F32)],
        compiler_params=pltpu.CompilerParams(dimension_semantics=("arbitrary",), vmem_limit_bytes=VMEM_LIMIT),
        name="conf_sample",
    )(x_rows, state_pad, *weights)


def _project_group(g, xn, wzx_ref, z_ref, pre_ref, row0):
    zx = _dot(xn, wzx_ref[:, g * ZX_GROUP_W:(g + 1) * ZX_GROUP_W])
    z_ref[:, g * GROUP_W:(g + 1) * GROUP_W] = zx[:, :GROUP_W]
    pre_ref[row0:row0 + xn.shape[0], g * XBC_GROUP_W:(g + 1) * XBC_GROUP_W] = zx[:, GROUP_W:]


def _seg_cumsum(x, row, seg_len, reverse):
    pos = row % seg_len
    n = x.shape[0]
    step = 1
    while step < seg_len:
        if reverse:
            x = x + jnp.where(pos < seg_len - step, pltpu.roll(x, n - step, axis=0), 0.0)
        else:
            x = x + jnp.where(pos >= step, pltpu.roll(x, step, axis=0), 0.0)
        step *= 2
    return x


def _expand_heads(m, g, lane):
    rows = m.shape[0]
    cols = [jnp.broadcast_to(m[:, HEADS_PER_GROUP * g + i:HEADS_PER_GROUP * g + i + 1], (rows, LANES))
            for i in range(HEADS_PER_GROUP)]
    lo = jnp.where(lane < HEAD_DIM, cols[0], cols[1])
    hi = jnp.where(lane < HEAD_DIM, cols[2], cols[3])
    return jnp.concatenate([lo, hi], axis=1)


def _ssd_prelude(dtr, valid, seg_len, dtb_ref, alog_ref):
    row = lax.broadcasted_iota(jnp.int32, (TILE, TILE), 0)
    col = lax.broadcasted_iota(jnp.int32, (TILE, TILE), 1)
    xdt = dtr + dtb_ref[...]
    dt = (jnp.maximum(xdt, 0.0) + jnp.log1p(jnp.exp(-jnp.abs(xdt)))) * valid
    da = dt * (-jnp.exp(alog_ref[...]))
    cum = _seg_cumsum(da, row, seg_len, reverse=False)
    return {
        "causal": (col <= row) & ((col // seg_len) == (row // seg_len)),
        "cum": cum,
        "cum_t": cum.T,
        "dt_t": dt.T,
        "ecum": jnp.exp(cum),
        "tail": jnp.exp(_seg_cumsum(da, row, seg_len, reverse=True) - da) * dt,
    }


def _ssd_group(g, pre, z_ref, xc_ref, seg_len, n_seq, h_in, h_out, dskip_ref, normw_ref, after_first_dot=None):
    lane = lax.broadcasted_iota(jnp.int32, (TILE, LANES), 1)
    lane_g = lax.broadcasted_iota(jnp.int32, (TILE, GROUP_W), 1)
    cum, cum_t, dt_t = pre["cum"], pre["cum_t"], pre["dt_t"]
    c0 = g * XBC_GROUP_W
    gs = slice(g * GROUP_W, (g + 1) * GROUP_W)
    xs = xc_ref[:, c0:c0 + GROUP_W]
    zg = z_ref[:, gs]
    bm16 = xc_ref[:, c0 + GROUP_W:c0 + GROUP_W + N_STATE].astype(BF16)
    cm16 = xc_ref[:, c0 + GROUP_W + N_STATE:c0 + XBC_GROUP_W].astype(BF16)
    cb = _dot_nt(cm16, bm16)
    extra = after_first_dot() if after_first_dot is not None else None
    seqs = [(s, slice(s * seg_len, (s + 1) * seg_len), (s + 1) * seg_len - 1) for s in range(n_seq)]
    ystate = [_dot_nt(cm16[rs], h_in(s, g).astype(BF16)) for s, rs, _ in seqs]
    if n_seq * seg_len < TILE:
        ystate.append(jnp.zeros((TILE - n_seq * seg_len, GROUP_W), F32))
    ys = ystate[0] if len(ystate) == 1 else jnp.concatenate(ystate, axis=0)
    ws, xms = [], []
    for i in range(HEADS_PER_GROUP):
        h = HEADS_PER_GROUP * g + i
        seg = cum[:, h:h + 1] - cum_t[h:h + 1, :]
        decay = jnp.exp(jnp.where(pre["causal"], seg, -jnp.inf))
        ws.append((cb * decay * dt_t[h:h + 1, :]).astype(BF16))
        xms.append(jnp.where((lane_g // HEAD_DIM) == i, xs, 0.0).astype(BF16))
    yg = _dot(jnp.concatenate(ws, axis=1), jnp.concatenate(xms, axis=0))
    yg = yg + ys * _expand_heads(pre["ecum"], g, lane)
    xw16 = (xs * _expand_heads(pre["tail"], g, lane)).astype(BF16)

    def update_state():
        for s, rs, last in seqs:
            upd = _dot_tn(xw16[rs], bm16[rs])
            dec = jnp.concatenate(
                [jnp.broadcast_to(jnp.exp(cum_t[HEADS_PER_GROUP * g + i:HEADS_PER_GROUP * g + i + 1, last:last + 1]),
                                  (HEAD_DIM, N_STATE)) for i in range(HEADS_PER_GROUP)], axis=0)
            h_out(s, g, h_in(s, g) * dec + upd)

    v = (yg + dskip_ref[:, gs] * xs) * _silu(zg)
    v = v * lax.rsqrt(jnp.mean(v * v, axis=-1, keepdims=True) + EPS) * normw_ref[:, gs]
    return v.astype(BF16), update_state, extra


def _out_proj_group(g, v16, wout_ref):
    return _dot(v16, wout_ref[g * GROUP_W:(g + 1) * GROUP_W, :])


def _ssm_prompt_kernel(head_ref, x_ref, nw_ref, wzx_ref, wdt_ref, cw_ref, cb_ref, dtb_ref, alog_ref, dskip_ref,
                       normw_ref, wout_ref, yb_ref, ns_ref, hs_ref, z_ref, cfull_ref, xc_ref, dtr_ref):
    s = pl.program_id(1)
    n_tiles = pl.num_programs(1) - 1
    conv_dim = xc_ref.shape[-1]

    @pl.when(s == 0)
    def _():
        cfull_ref[0:SCONV_PAD, :] = jnp.zeros((SCONV_PAD, conv_dim), F32)
        hs_ref[...] = jnp.zeros(hs_ref.shape, F32)
        z_ref[...] = jnp.zeros(z_ref.shape, F32)
        xc_ref[...] = jnp.zeros(xc_ref.shape, F32)
        dtr_ref[...] = jnp.zeros(dtr_ref.shape, F32)

    x = jnp.where(s == 0, head_ref[...], x_ref[...])
    xn = _rms(x, nw_ref[...]).astype(BF16)
    rowv = lax.broadcasted_iota(jnp.int32, (TILE, 1), 0)
    valid = jnp.where((s == 0) | ((s == 1) & (rowv < TILE - N_META)), 0.0, 1.0)
    pre = _ssd_prelude(dtr_ref[...], valid, TILE, dtb_ref, alog_ref)
    dtr_ref[...] = _dot(xn, wdt_ref[...])

    def h_in(q, g):
        return hs_ref[HEADS_PER_GROUP * g:HEADS_PER_GROUP * (g + 1)].reshape(GROUP_W, N_STATE)

    def h_out(q, g, v):
        hs_ref[HEADS_PER_GROUP * g:HEADS_PER_GROUP * (g + 1)] = v.reshape(HEADS_PER_GROUP, HEAD_DIM, N_STATE)

    def emit(cs, v):
        xc_ref[:, cs] = _silu(v)

    acc = jnp.zeros(yb_ref.shape, F32)
    prev = None
    for g in range(N_GROUPS):
        def independent_dots(g=g, prev=prev):
            _project_group(g, xn, wzx_ref, z_ref, cfull_ref, SCONV_PAD)
            if prev is None:
                return None
            prev_v16, prev_update = prev
            term = _out_proj_group(g - 1, prev_v16, wout_ref)
            prev_update()
            return term

        v16, update, term = _ssd_group(g, pre, z_ref, xc_ref, TILE, 1, h_in, h_out, dskip_ref, normw_ref,
                                       independent_dots)
        if term is not None:
            acc = acc + term
        _dwconv(cfull_ref, SCONV_PAD - (SSM_K - 1), TILE, cw_ref, cb_ref, SSM_K, XBC_GROUP_W, emit,
                col0=g * XBC_GROUP_W)
        prev = (v16, update)
    yb_ref[...] = acc + _out_proj_group(N_GROUPS - 1, prev[0], wout_ref)
    prev[1]()

    @pl.when(s == n_tiles - 1)
    def _():
        ns_ref[...] = cfull_ref[pl.ds(SCONV_PAD + TILE - (SSM_K - 1), SSM_K - 1), :]

    cfull_ref[0:SCONV_PAD, :] = cfull_ref[TILE:TILE + SCONV_PAD, :]


def _ssm_sample_proj_kernel(x_ref, st_ref, nw_ref, wzx_ref, wdt_ref, cw_ref, cb_ref,
                            z_ref, xc_ref, dtr_ref, ns_ref, xbc_ref, cs_ref, *, n_new):
    conv_dim = xc_ref.shape[-1]
    n_seq = TILE // SAMPLE_ROWS
    xn = _rms(x_ref[...], nw_ref[...]).astype(BF16)
    for g in range(N_GROUPS):
        _project_group(g, xn, wzx_ref, z_ref, xbc_ref, 0)
    dtr_ref[...] = _dot(xn, wdt_ref[...])

    def seq_body(s, carry):
        r0 = pl.multiple_of(s * SAMPLE_ROWS, SAMPLE_ROWS)
        cs_ref[0:SCONV_PAD, :] = st_ref[s]
        cs_ref[SCONV_PAD:SCONV_PAD + SAMPLE_ROWS, :] = xbc_ref[pl.ds(r0, SAMPLE_ROWS), :]

        def emit(cs, v):
            xc_ref[pl.ds(r0, SAMPLE_ROWS), cs] = _silu(v)

        _dwconv(cs_ref, SCONV_PAD - (SSM_K - 1), SAMPLE_ROWS, cw_ref, cb_ref, SSM_K, conv_dim, emit)
        ns_ref[s] = cs_ref[pl.ds(SCONV_PAD + n_new - (SSM_K - 1), SSM_K - 1), :]
        return carry

    lax.fori_loop(0, n_seq, seq_body, 0)


def _ssm_sample_ssd_kernel(z_in, xc_in, dtr_in, hin_ref, dtb_ref, alog_ref, dskip_ref, normw_ref, wout_ref,
                           yb_ref, hout_ref, z_ref, xc_ref, dtr_ref, *, n_new, n_seq):
    rows = n_seq * SAMPLE_ROWS
    for src, dst in ((z_in, z_ref), (xc_in, xc_ref), (dtr_in, dtr_ref)):
        dst[0:rows, :] = src[...]
        dst[rows:TILE, :] = jnp.zeros((TILE - rows, dst.shape[-1]), F32)
    rowv = lax.broadcasted_iota(jnp.int32, (TILE, 1), 0)
    valid = jnp.where((rowv < rows) & ((rowv % SAMPLE_ROWS) < n_new), 1.0, 0.0)

    def h_in(s, g):
        return hin_ref[s, HEADS_PER_GROUP * g:HEADS_PER_GROUP * (g + 1)].reshape(GROUP_W, N_STATE)

    def h_out(s, g, v):
        hout_ref[s, HEADS_PER_GROUP * g:HEADS_PER_GROUP * (g + 1)] = v.reshape(HEADS_PER_GROUP, HEAD_DIM, N_STATE)

    pre = _ssd_prelude(dtr_ref[...], valid, SAMPLE_ROWS, dtb_ref, alog_ref)
    acc = jnp.zeros((TILE, yb_ref.shape[-1]), F32)
    for g in range(N_GROUPS):
        v16, update, _ = _ssd_group(g, pre, z_ref, xc_ref, SAMPLE_ROWS, n_seq, h_in, h_out, dskip_ref, normw_ref)
        acc = acc + _out_proj_group(g, v16, wout_ref)
        update()
    yb_ref[...] = acc[0:rows, :]


def _ssm_prompt(head, x_prompt, p):
    b, l, d = x_prompt.shape
    nb = l // TILE
    d_inner = p["ssm_norm_w"].shape[-1]
    conv_dim = p["ssm_conv_b"].shape[-1]
    n_heads = d_inner // HEAD_DIM
    xmap = lambda i, j: (i, jnp.clip(j - 1, 0, nb - 1), 0)
    ymap = lambda i, j: (i, jnp.maximum(j - 2, 0), 0)
    weights = [p["norm_w"], p["w_zx"], p["w_dt"], p["ssm_conv_w"], p["ssm_conv_b"], p["dt_bias"], p["a_log"],
               p["d_skip"], p["ssm_norm_w"], p["ssm_out_w"]]
    wspecs = [_const_spec(w.shape) for w in weights]
    return pl.pallas_call(
        _ssm_prompt_kernel,
        grid=(b, nb + 2),
        in_specs=[_const_spec((TILE, d)), pl.BlockSpec((None, TILE, d), xmap)] + wspecs,
        out_specs=[pl.BlockSpec((None, TILE, d), ymap),
                   pl.BlockSpec((None, SSM_K - 1, conv_dim), lambda i, j: (i, 0, 0)),
                   pl.BlockSpec((None, n_heads, HEAD_DIM, N_STATE), lambda i, j: (i, 0, 0, 0))],
        out_shape=[jax.ShapeDtypeStruct((b, l, d), F32), jax.ShapeDtypeStruct((b, SSM_K - 1, conv_dim), F32),
                   jax.ShapeDtypeStruct((b, n_heads, HEAD_DIM, N_STATE), F32)],
        scratch_shapes=[pltpu.VMEM((TILE, d_inner), F32), pltpu.VMEM((SCONV_PAD + TILE, conv_dim), F32),
                        pltpu.VMEM((TILE, conv_dim), F32), pltpu.VMEM((TILE, LANES), F32)],
        compiler_params=pltpu.CompilerParams(dimension_semantics=("arbitrary", "arbitrary"),
                                             vmem_limit_bytes=VMEM_LIMIT),
        name="ssm_prompt",
    )(head, x_prompt, *weights)


def _ssm_sample(x_rows, sconv_pad, h0, p, n_new):
    rows, d = x_rows.shape
    d_inner = p["ssm_norm_w"].shape[-1]
    conv_dim = p["ssm_conv_b"].shape[-1]
    n_heads = d_inner // HEAD_DIM
    n_seq_proj = TILE // SAMPLE_ROWS
    weights = [p["norm_w"], p["w_zx"], p["w_dt"], p["ssm_conv_w"], p["ssm_conv_b"]]
    z, xc, dtr, new_sconv = pl.pallas_call(
        functools.partial(_ssm_sample_proj_kernel, n_new=n_new),
        grid=(rows // TILE,),
        in_specs=[pl.BlockSpec((TILE, d), lambda i: (i, 0)),
                  pl.BlockSpec((n_seq_proj, SCONV_PAD, conv_dim), lambda i: (i, 0, 0))]
        + [_const_spec(w.shape) for w in weights],
        out_specs=[pl.BlockSpec((TILE, d_inner), lambda i: (i, 0)), pl.BlockSpec((TILE, conv_dim), lambda i: (i, 0)),
                   pl.BlockSpec((TILE, LANES), lambda i: (i, 0)),
                   pl.BlockSpec((n_seq_proj, SSM_K - 1, conv_dim), lambda i: (i, 0, 0))],
        out_shape=[jax.ShapeDtypeStruct((rows, d_inner), F32), jax.ShapeDtypeStruct((rows, conv_dim), F32),
                   jax.ShapeDtypeStruct((rows, LANES), F32),
                   jax.ShapeDtypeStruct((rows // SAMPLE_ROWS, SSM_K - 1, conv_dim), F32)],
        scratch_shapes=[pltpu.VMEM((TILE, conv_dim), F32), pltpu.VMEM((SCONV_PAD + SAMPLE_ROWS, conv_dim), F32)],
        compiler_params=pltpu.CompilerParams(dimension_semantics=("arbitrary",), vmem_limit_bytes=VMEM_LIMIT),
        name="ssm_sample_proj",
    )(x_rows, sconv_pad, *weights)

    n_seq = 4
    r = n_seq * SAMPLE_ROWS
    weights = [p["dt_bias"], p["a_log"], p["d_skip"], p["ssm_norm_w"], p["ssm_out_w"]]
    yb, h_new = pl.pallas_call(
        functools.partial(_ssm_sample_ssd_kernel, n_new=n_new, n_seq=n_seq),
        grid=(rows // r,),
        in_specs=[pl.BlockSpec((r, d_inner), lambda i: (i, 0)), pl.BlockSpec((r, conv_dim), lambda i: (i, 0)),
                  pl.BlockSpec((r, LANES), lambda i: (i, 0)),
                  pl.BlockSpec((n_seq, n_heads, HEAD_DIM, N_STATE), lambda i: (i, 0, 0, 0))]
        + [_const_spec(w.shape) for w in weights],
        out_specs=[pl.BlockSpec((r, d), lambda i: (i, 0)),
                   pl.BlockSpec((n_seq, n_heads, HEAD_DIM, N_STATE), lambda i: (i, 0, 0, 0))],
        out_shape=[jax.ShapeDtypeStruct((rows, d), F32), jax.ShapeDtypeStruct(h0.shape, F32)],
        scratch_shapes=[pltpu.VMEM((TILE, d_inner), F32), pltpu.VMEM((TILE, conv_dim), F32),
                        pltpu.VMEM((TILE, LANES), F32)],
        compiler_params=pltpu.CompilerParams(dimension_semantics=("arbitrary",), vmem_limit_bytes=VMEM_LIMIT),
        name="ssm_sample_ssd",
    )(z, xc, dtr, h0, *weights)
    return yb, new_sconv, h_new


def _merge_kernel(x_ref, ya_ref, yb_ref, nw_ref, wg_ref, gb_ref, wo_ref, nfw_ref, rwh_ref, rwl_ref, rb_ref,
                  x1_ref, xn2_ref, comb_ref):
    d = x_ref.shape[-1]
    x = x_ref[...]
    xn = _rms(x, nw_ref[...]).astype(BF16)
    gates = _sigmoid(_dot(xn, wg_ref[...]) + gb_ref[...])
    merged = gates[:, :d] * ya_ref[...] + gates[:, d:] * yb_ref[...]
    x1 = x + _dot(merged.astype(BF16), wo_ref[...])
    x1_ref[...] = x1
    xn2 = _rms(x1, nfw_ref[...])
    xn2_ref[...] = xn2.astype(BF16)

    x_hi = xn2.astype(BF16)
    x_lo = (xn2 - x_hi.astype(F32)).astype(BF16)
    logits = _dot(x_hi, rwh_ref[...]) + _dot(x_lo, rwh_ref[...]) + _dot(x_hi, rwl_ref[...]) + rb_ref[...]
    rows = logits.shape[0]
    lane = lax.broadcasted_iota(jnp.int32, (rows, LANES), 1)
    neg = -jnp.inf
    is_g = (lane >= N_EXPERTS) & (lane < N_EXPERTS + N_EXPERT_GROUPS)
    gl = jnp.where(is_g, logits, neg)
    gmax = jnp.max(gl, axis=-1, keepdims=True)
    gsel = jnp.min(jnp.where(gl == gmax, lane, LANES), axis=-1, keepdims=True) - N_EXPERTS
    gprob = 1.0 / jnp.sum(jnp.exp(gl - gmax), axis=-1, keepdims=True)
    el = jnp.where((lane < N_EXPERTS) & ((lane // EXPERTS_PER_GROUP) == gsel), logits, neg)
    m1 = jnp.max(el, axis=-1, keepdims=True)
    i1 = jnp.min(jnp.where(el == m1, lane, LANES), axis=-1, keepdims=True)
    el2 = jnp.where(lane == i1, neg, el)
    m2 = jnp.max(el2, axis=-1, keepdims=True)
    i2 = jnp.min(jnp.where(el2 == m2, lane, LANES), axis=-1, keepdims=True)
    e2 = jnp.exp(m2 - m1)
    den = 1.0 + e2
    comb = jnp.where(lane == i1, (1.0 / den) * gprob, jnp.where(lane == i2, (e2 / den) * gprob, 0.0))
    comb_ref[...] = jnp.where(lane == GSEL_LANE, gsel.astype(F32), comb)


def _merge(x, ya, yb, p, tm):
    rows, d = x.shape
    weights = [p["norm_w"], p["w_gate"], p["gate_b"], p["w_o"], p["norm_ffn_w"], p["router_w_hi"],
               p["router_w_lo"], p["router_b"]]
    row_spec = pl.BlockSpec((tm, d), lambda i: (i, 0))
    return pl.pallas_call(
        _merge_kernel,
        grid=(rows // tm,),
        in_specs=[row_spec, row_spec, row_spec] + [_const_spec(w.shape) for w in weights],
        out_specs=[row_spec, row_spec, pl.BlockSpec((tm, LANES), lambda i: (i, 0))],
        out_shape=[jax.ShapeDtypeStruct((rows, d), F32), jax.ShapeDtypeStruct((rows, d), BF16),
                   jax.ShapeDtypeStruct((rows, LANES), F32)],
        compiler_params=pltpu.CompilerParams(dimension_semantics=("arbitrary",), vmem_limit_bytes=VMEM_LIMIT),
        name="merge_router",
    )(x, ya, yb, *weights)


def _moe_kernel(xn_ref, comb_ref, x1_ref, w1_ref, w3_ref, w2_ref, nw_ref, y_ref, rt_ref, *, chunk):
    g = pl.program_id(1)
    tm = xn_ref.shape[0]
    comb = comb_ref[...]
    lane = lax.broadcasted_iota(jnp.int32, (tm, LANES), 1)
    gsel = comb[:, GSEL_LANE:GSEL_LANE + 1]

    @pl.when(g == 0)
    def _():
        row = lax.broadcasted_iota(jnp.int32, (tm, LANES), 0)
        onehot = jnp.where(lane.astype(F32) == gsel, 1.0, 0.0)
        incl = _seg_cumsum(onehot, row, tm, reverse=False)
        rank = jnp.sum(onehot * (incl - onehot), axis=-1, keepdims=True)
        packed = jnp.where(lane == 0, rank, jnp.where(lane == 1, gsel, 0.0))
        rt_ref[0:SUBLANES, :] = packed.T[0:SUBLANES, :]
        rt_ref[SUBLANES:2 * SUBLANES, 0:LANES] = jnp.broadcast_to(incl[tm - 1:tm, :], (SUBLANES, LANES))
        y_ref[...] = jnp.zeros(y_ref.shape, F32)

    gf = g.astype(F32)
    lane_row = lax.broadcasted_iota(jnp.int32, (1, LANES), 1)
    count = jnp.sum(jnp.where(lane_row == g, rt_ref[SUBLANES:SUBLANES + 1, 0:LANES], 0.0), axis=-1, keepdims=True)
    n_chunks = (count[0, 0].astype(jnp.int32) + chunk - 1) // chunk
    w_hi = jnp.where(lane < N_EXPERTS, comb, 0.0).astype(BF16)
    w_lo = (jnp.where(lane < N_EXPERTS, comb, 0.0) - w_hi.astype(F32)).astype(BF16)
    rowi = lax.broadcasted_iota(jnp.int32, (chunk, tm), 0).astype(F32)
    lane_c = lax.broadcasted_iota(jnp.int32, (chunk, LANES), 1)

    def chunk_body(c, carry):
        base = (c * chunk).astype(F32)
        sel = jnp.where((rt_ref[1:2, :] == gf) & ((rt_ref[0:1, :] - base) == rowi), 1.0, 0.0).astype(BF16)
        xc = _dot(sel, xn_ref[...]).astype(BF16)
        wc = _dot(sel, w_hi) + _dot(sel, w_lo)
        acts = []
        for e in range(EXPERTS_PER_GROUP):
            ce = jnp.sum(jnp.where(lane_c == g * EXPERTS_PER_GROUP + e, wc, 0.0), axis=-1, keepdims=True)
            acts.append((_silu(_dot(xc, w1_ref[e])) * _dot(xc, w3_ref[e]) * ce).astype(BF16))
        y = _dot(jnp.concatenate(acts, axis=1), w2_ref[...])
        y_hi = y.astype(BF16)
        y_lo = (y - y_hi.astype(F32)).astype(BF16)
        y_ref[...] += _dot_tn(sel, y_hi) + _dot_tn(sel, y_lo)
        return carry

    lax.fori_loop(0, n_chunks, chunk_body, 0)

    @pl.when(g == pl.num_programs(1) - 1)
    def _():
        y_ref[...] = _rms(x1_ref[...] + y_ref[...], nw_ref[...])


def _moe(xn2, comb, x1, p, tm):
    rows, d = x1.shape
    n_e, _, d_e = p["exp_w1"].shape
    n_g = n_e // EXPERTS_PER_GROUP
    w2 = p["exp_w2"].reshape(n_g, EXPERTS_PER_GROUP * d_e, d)
    pack = 2 * SUBLANES
    chunk = -(-(tm * 9 // (8 * n_g)) // pack) * pack
    row = lambda i, g: (i, 0)
    return pl.pallas_call(
        functools.partial(_moe_kernel, chunk=chunk),
        grid=(rows // tm, n_g),
        in_specs=[pl.BlockSpec((tm, d), row), pl.BlockSpec((tm, LANES), row),
                  pl.BlockSpec((tm, d), row, pipeline_mode=pl.Buffered(1)),
                  pl.BlockSpec((EXPERTS_PER_GROUP, d, d_e), lambda i, g: (g, 0, 0)),
                  pl.BlockSpec((EXPERTS_PER_GROUP, d, d_e), lambda i, g: (g, 0, 0)),
                  pl.BlockSpec((None, EXPERTS_PER_GROUP * d_e, d), lambda i, g: (g, 0, 0)),
                  _const_spec((1, d))],
        out_specs=pl.BlockSpec((tm, d), row),
        out_shape=jax.ShapeDtypeStruct((rows, d), F32),
        scratch_shapes=[pltpu.VMEM((2 * SUBLANES, tm), F32)],
        compiler_params=pltpu.CompilerParams(dimension_semantics=("arbitrary", "arbitrary"),
                                             vmem_limit_bytes=VMEM_LIMIT),
        name="moe",
    )(xn2, comb, x1, p["exp_w1"], p["exp_w3"], w2, p["norm_final_w"])


def _group_xbc_cols(v, d_inner):
    lead = v.shape[:-1]
    xs = v[..., :d_inner].reshape(lead + (N_GROUPS, GROUP_W))
    bm = v[..., d_inner:d_inner + N_GROUPS * N_STATE].reshape(lead + (N_GROUPS, N_STATE))
    cm = v[..., d_inner + N_GROUPS * N_STATE:].reshape(lead + (N_GROUPS, N_STATE))
    return jnp.concatenate([xs, bm, cm], axis=-1).reshape(lead + (-1,))


def _ungroup_xbc_cols(v, d_inner):
    lead = v.shape[:-1]
    v = v.reshape(lead + (N_GROUPS, XBC_GROUP_W))
    parts = [v[..., :GROUP_W], v[..., GROUP_W:GROUP_W + N_STATE], v[..., GROUP_W + N_STATE:]]
    return jnp.concatenate([q.reshape(lead + (-1,)) for q in parts], axis=-1)


def _prep_params(norm_mix_w, w_in, conf_dw_w, conf_dw_b, conf_ln_g, conf_ln_b, conf_out_w, conf_out_b, ssm_conv_w,
                 ssm_conv_b, dt_bias, a_log, d_skip, ssm_norm_w, ssm_out_w, gate_b, w_o, norm_ffn_w,
                 router_group_w, router_group_b, router_expert_w, router_expert_b, exp_w1, exp_w3, exp_w2,
                 norm_final_w):
    d = norm_mix_w.shape[-1]
    d_inner = ssm_norm_w.shape[-1]
    conv_dim = ssm_conv_b.shape[-1]
    n_heads = dt_bias.shape[-1]
    s1, s2 = d, 2 * d
    s3 = s2 + d_inner
    s4 = s3 + conv_dim
    s5 = s4 + n_heads
    w = w_in[0]
    row = lambda v: v.reshape(1, -1).astype(F32)
    pad_lanes = lambda v: jnp.pad(v, ((0, 0), (0, LANES - v.shape[-1])))
    router_w = pad_lanes(jnp.concatenate([router_expert_w[0], router_group_w[0]], axis=1))
    router_w_hi = router_w.astype(BF16)
    return {
        "norm_w": row(norm_mix_w[0]),
        "w_vg": w[:, :s2].astype(BF16),
        "w_zx": jnp.concatenate([w[:, s2:s3].reshape(d, N_GROUPS, GROUP_W),
                                 _group_xbc_cols(w[:, s3:s4], d_inner).reshape(d, N_GROUPS, XBC_GROUP_W)],
                                axis=-1).reshape(d, -1).astype(BF16),
        "w_dt": pad_lanes(w[:, s4:s5]).astype(BF16),
        "w_gate": w[:, s5:].astype(BF16),
        "conf_dw_w": jnp.pad(conf_dw_w[0], ((0, CONF_PAD - CONF_K), (0, 0))),
        "conf_dw_b": row(conf_dw_b[0]),
        "conf_ln_g": row(conf_ln_g[0]),
        "conf_ln_b": row(conf_ln_b[0]),
        "conf_out_w": conf_out_w[0].astype(BF16),
        "conf_out_b": row(conf_out_b[0]),
        "ssm_conv_w": jnp.pad(_group_xbc_cols(ssm_conv_w[0], d_inner), ((0, SUBLANES - SSM_K), (0, 0))),
        "ssm_conv_b": row(_group_xbc_cols(ssm_conv_b[0], d_inner)),
        "dt_bias": pad_lanes(row(dt_bias[0])),
        "a_log": pad_lanes(row(a_log[0])),
        "d_skip": row(jnp.repeat(d_skip[0], HEAD_DIM)),
        "ssm_norm_w": row(ssm_norm_w[0]),
        "ssm_out_w": ssm_out_w[0].astype(BF16),
        "gate_b": row(gate_b[0]),
        "w_o": w_o[0].astype(BF16),
        "norm_ffn_w": row(norm_ffn_w[0]),
        "router_w_hi": router_w_hi,
        "router_w_lo": (router_w - router_w_hi.astype(F32)).astype(BF16),
        "router_b": pad_lanes(row(jnp.concatenate([router_expert_b[0], router_group_b[0]]))),
        "exp_w1": exp_w1[0].astype(BF16),
        "exp_w3": exp_w3[0].astype(BF16),
        "exp_w2": exp_w2[0].astype(BF16),
        "norm_final_w": row(norm_final_w),
    }


def kernel(x_prompt, x_sample, state_conf_conv, state_ssm_conv, state_ssm, meta_tokens, norm_mix_w, w_in, conf_dw_w, conf_dw_b, conf_ln_g, conf_ln_b, conf_out_w, conf_out_b, ssm_conv_w, ssm_conv_b, dt_bias, a_log, d_skip, ssm_norm_w, ssm_out_w, gate_b, w_o, norm_ffn_w, router_group_w, router_group_b, router_expert_w, router_expert_b, exp_w1, exp_w3, exp_w2, norm_final_w):
    assert norm_mix_w.shape[0] == 1, "single-layer trunk"
    b, l, d = x_prompt.shape
    nb, n_new, _ = x_sample.shape
    assert l % TILE == 0 and n_new <= SAMPLE_ROWS and (nb * SAMPLE_ROWS) % TILE == 0
    p = _prep_params(norm_mix_w, w_in, conf_dw_w, conf_dw_b, conf_ln_g, conf_ln_b, conf_out_w, conf_out_b,
                     ssm_conv_w, ssm_conv_b, dt_bias, a_log, d_skip, ssm_norm_w, ssm_out_w, gate_b, w_o,
                     norm_ffn_w, router_group_w, router_group_b, router_expert_w, router_expert_b, exp_w1, exp_w3,
                     exp_w2, norm_final_w)

    head = jnp.concatenate([jnp.zeros((TILE - N_META, d), F32), meta_tokens.astype(F32)], axis=0)
    ya_p, new_conf_p = _conf_prompt(head, x_prompt, p)
    yb_p, new_sconv_p, new_ssm_p = _ssm_prompt(head, x_prompt, p)

    xs_rows = jnp.pad(x_sample, ((0, 0), (0, SAMPLE_ROWS - n_new), (0, 0))).reshape(nb * SAMPLE_ROWS, d)
    conf_pad = jnp.pad(state_conf_conv[0], ((0, 0), (CONF_PAD - (CONF_K - 1), 0), (0, 0)))
    d_inner = ssm_norm_w.shape[-1]
    sconv_pad = jnp.pad(_group_xbc_cols(state_ssm_conv[0], d_inner), ((0, 0), (SCONV_PAD - (SSM_K - 1), 0), (0, 0)))
    ya_s, new_conf_s = _conf_sample(xs_rows, conf_pad, p, n_new)
    yb_s, new_sconv_s, new_ssm_s = _ssm_sample(xs_rows, sconv_pad, state_ssm[0], p, n_new)
    unpad = lambda v: v.reshape(nb, SAMPLE_ROWS, d)[:, :n_new].reshape(nb * n_new, d)

    outs = []
    for x, ya, yb in ((x_prompt.reshape(b * l, d), ya_p.reshape(b * l, d), yb_p.reshape(b * l, d)),
                      (x_sample.reshape(nb * n_new, d), unpad(ya_s), unpad(yb_s))):
        tm = next(t for t in (1024, 512, TILE) if x.shape[0] % t == 0)
        x1, xn2, comb = _merge(x, ya, yb, p, min(tm, 512))
        outs.append(_moe(xn2, comb, x1, p, tm))
    y_prompt = outs[0].reshape(b, l, d)
    y_sample = outs[1].reshape(nb, n_new, d)
    return (y_prompt, y_sample, new_conf_p[None], _ungroup_xbc_cols(new_sconv_p, d_inner)[None], new_ssm_p[None],
            new_conf_s[None], _ungroup_xbc_cols(new_sconv_s, d_inner)[None], new_ssm_s[None])
```

```python
import functools

import jax
import jax.numpy as jnp
from jax import lax
from jax.experimental import pallas as pl
from jax.experimental.pallas import tpu as pltpu

F32 = jnp.float32
BF16 = jnp.bfloat16
EPS = 1e-6

LANES = 128
SUBLANES = 8
TILE = 128
VMEM_LIMIT = 56 * 1024 * 1024

N_META = 16
CONF_K = 31
SSM_K = 4
HEAD_DIM = 64
N_STATE = 128
N_GROUPS = 8
HEADS_PER_GROUP = 4
GROUP_W = HEADS_PER_GROUP * HEAD_DIM
N_EXPERTS = 32
EXPERTS_PER_GROUP = 8
N_EXPERT_GROUPS = 4
SAMPLE_ROWS = 8
CONF_PAD = 32
SCONV_PAD = 8
XBC_GROUP_W = GROUP_W + 2 * N_STATE
ZX_GROUP_W = GROUP_W + XBC_GROUP_W
GSEL_LANE = 64


def _rms(x, w):
    return x * lax.rsqrt(jnp.mean(x * x, axis=-1, keepdims=True) + EPS) * w


def _sigmoid(x):
    return 0.5 * jnp.tanh(0.5 * x) + 0.5


def _silu(x):
    return x * _sigmoid(x)


def _dot(a, b):
    return jnp.dot(a, b, preferred_element_type=F32)


def _dot_nt(a, b):
    return lax.dot_general(a, b, (((1,), (1,)), ((), ())), preferred_element_type=F32)


def _dot_tn(a, b):
    return lax.dot_general(a, b, (((0,), (0,)), ((), ())), preferred_element_type=F32)


def _const_spec(shape):
    zeros = (0,) * len(shape)
    return pl.BlockSpec(shape, lambda *_: zeros, pipeline_mode=pl.Buffered(1))


def _dwconv(src_ref, base, rows, w_ref, b_ref, taps, width, emit, col0=0):
    for c in range(col0 // LANES, (col0 + width) // LANES):
        cs = slice(c * LANES, (c + 1) * LANES)
        acc = jnp.broadcast_to(b_ref[:, cs], (rows, LANES))
        for r in range(min(SUBLANES, taps)):
            qs = range((taps - r + SUBLANES - 1) // SUBLANES)
            slab = src_ref[pl.ds(base + r, rows + SUBLANES * (len(qs) - 1)), cs]
            part = w_ref[r:r + 1, cs] * slab[0:rows]
            for q in qs[1:]:
                k = SUBLANES * q + r
                part = part + w_ref[k:k + 1, cs] * slab[SUBLANES * q:SUBLANES * q + rows]
            acc = acc + part
        emit(cs, acc)


def _conf_tail(conv_ref, ln_g_ref, ln_b_ref, wout_ref, bout_ref, ya_ref):
    c = conv_ref[...]
    mu = jnp.mean(c, axis=-1, keepdims=True)
    d = c - mu
    var = jnp.mean(d * d, axis=-1, keepdims=True)
    y = d * lax.rsqrt(var + EPS) * ln_g_ref[...] + ln_b_ref[...]
    y = _silu(y)
    ya_ref[...] = _dot(y.astype(BF16), wout_ref[...]) + bout_ref[...]


def _conf_glu(x, nw_ref, wvg_ref):
    d = x.shape[-1]
    xn = _rms(x, nw_ref[...]).astype(BF16)
    vg = _dot(xn, wvg_ref[...])
    return vg[:, :d] * _sigmoid(vg[:, d:])


def _conf_prompt_kernel(head_ref, x_ref, nw_ref, wvg_ref, dww_ref, dwb_ref, lng_ref, lnb_ref, wout_ref, bout_ref,
                        ya_ref, nc_ref, afull_ref, conv_ref):
    j = pl.program_id(1)
    d = x_ref.shape[-1]

    @pl.when(j == 0)
    def _():
        afull_ref[0:CONF_PAD, :] = jnp.zeros((CONF_PAD, d), F32)

    x = jnp.where(j == 0, head_ref[...], x_ref[...])
    afull_ref[CONF_PAD:CONF_PAD + TILE, :] = _conf_glu(x, nw_ref, wvg_ref)

    def emit(cs, v):
        conv_ref[:, cs] = v

    _dwconv(afull_ref, CONF_PAD - (CONF_K - 1), TILE, dww_ref, dwb_ref, CONF_K, d, emit)
    nc_ref[...] = afull_ref[pl.ds(CONF_PAD + TILE - (CONF_K - 1), CONF_K - 1), :]
    afull_ref[0:CONF_PAD, :] = afull_ref[TILE:TILE + CONF_PAD, :]
    _conf_tail(conv_ref, lng_ref, lnb_ref, wout_ref, bout_ref, ya_ref)


def _conf_sample_kernel(x_ref, st_ref, nw_ref, wvg_ref, dww_ref, dwb_ref, lng_ref, lnb_ref, wout_ref, bout_ref,
                        ya_ref, nc_ref, a_ref, cs_ref, conv_ref, *, n_new):
    d = x_ref.shape[-1]
    n_seq = TILE // SAMPLE_ROWS
    a_ref[...] = _conf_glu(x_ref[...], nw_ref, wvg_ref)

    def seq_body(s, carry):
        r0 = pl.multiple_of(s * SAMPLE_ROWS, SAMPLE_ROWS)
        cs_ref[0:CONF_PAD, :] = st_ref[s]
        cs_ref[CONF_PAD:CONF_PAD + SAMPLE_ROWS, :] = a_ref[pl.ds(r0, SAMPLE_ROWS), :]

        def emit(cs, v):
            conv_ref[pl.ds(r0, SAMPLE_ROWS), cs] = v

        _dwconv(cs_ref, CONF_PAD - (CONF_K - 1), SAMPLE_ROWS, dww_ref, dwb_ref, CONF_K, d, emit)
        nc_ref[s] = cs_ref[pl.ds(CONF_PAD + n_new - (CONF_K - 1), CONF_K - 1), :]
        return carry

    lax.fori_loop(0, n_seq, seq_body, 0)
    _conf_tail(conv_ref, lng_ref, lnb_ref, wout_ref, bout_ref, ya_ref)


def _conf_weights(p):
    d = p["norm_w"].shape[-1]
    return [p["norm_w"], p["w_vg"], p["conf_dw_w"], p["conf_dw_b"], p["conf_ln_g"], p["conf_ln_b"],
            p["conf_out_w"], p["conf_out_b"]], [
        _const_spec((1, d)), _const_spec((d, 2 * d)), _const_spec((CONF_PAD, d)), _const_spec((1, d)),
        _const_spec((1, d)), _const_spec((1, d)), _const_spec((d, d)), _const_spec((1, d))]


def _conf_prompt(head, x_prompt, p):
    b, l, d = x_prompt.shape
    nt = l // TILE + 1
    weights, wspecs = _conf_weights(p)
    xmap = lambda i, j: (i, jnp.maximum(j - 1, 0), 0)
    return pl.pallas_call(
        _conf_prompt_kernel,
        grid=(b, nt),
        in_specs=[_const_spec((TILE, d)), pl.BlockSpec((None, TILE, d), xmap)] + wspecs,
        out_specs=[pl.BlockSpec((None, TILE, d), xmap),
                   pl.BlockSpec((None, CONF_K - 1, d), lambda i, j: (i, 0, 0))],
        out_shape=[jax.ShapeDtypeStruct((b, l, d), F32), jax.ShapeDtypeStruct((b, CONF_K - 1, d), F32)],
        scratch_shapes=[pltpu.VMEM((CONF_PAD + TILE, d), F32), pltpu.VMEM((TILE, d), F32)],
        compiler_params=pltpu.CompilerParams(dimension_semantics=("arbitrary", "arbitrary"),
                                             vmem_limit_bytes=VMEM_LIMIT),
        name="conf_prompt",
    )(head, x_prompt, *weights)


def _conf_sample(x_rows, state_pad, p, n_new):
    rows, d = x_rows.shape
    n_seq = TILE // SAMPLE_ROWS
    weights, wspecs = _conf_weights(p)
    return pl.pallas_call(
        functools.partial(_conf_sample_kernel, n_new=n_new),
        grid=(rows // TILE,),
        in_specs=[pl.BlockSpec((TILE, d), lambda i: (i, 0)),
                  pl.BlockSpec((n_seq, CONF_PAD, d), lambda i: (i, 0, 0))] + wspecs,
        out_specs=[pl.BlockSpec((TILE, d), lambda i: (i, 0)),
                   pl.BlockSpec((n_seq, CONF_K - 1, d), lambda i: (i, 0, 0))],
        out_shape=[jax.ShapeDtypeStruct((rows, d), F32),
                   jax.ShapeDtypeStruct((rows // SAMPLE_ROWS, CONF_K - 1, d), F32)],
        scratch_shapes=[pltpu.VMEM((TILE, d), F32), pltpu.VMEM((CONF_PAD + SAMPLE_ROWS, d), F32),
                        pltpu.VMEM((TILE, d), F32)],
        compiler_params=pltpu.CompilerParams(dimension_semantics=("arbitrary",), vmem_limit_bytes=VMEM_LIMIT),
        name="conf_sample",
    )(x_rows, state_pad, *weights)


def _project_group(g, xn, wzx_ref, z_ref, pre_ref, row0):
    zx = _dot(xn, wzx_ref[:, g * ZX_GROUP_W:(g + 1) * ZX_GROUP_W])
    z_ref[:, g * GROUP_W:(g + 1) * GROUP_W] = zx[:, :GROUP_W]
    pre_ref[row0:row0 + xn.shape[0], g * XBC_GROUP_W:(g + 1) * XBC_GROUP_W] = zx[:, GROUP_W:]


def _seg_cumsum(x, row, seg_len, reverse):
    pos = row % seg_len
    n = x.shape[0]
    step = 1
    while step < seg_len:
        if reverse:
            x = x + jnp.where(pos < seg_len - step, pltpu.roll(x, n - step, axis=0), 0.0)
        else:
            x = x + jnp.where(pos >= step, pltpu.roll(x, step, axis=0), 0.0)
        step *= 2
    return x


def _expand_heads(m, g, lane):
    rows = m.shape[0]
    cols = [jnp.broadcast_to(m[:, HEADS_PER_GROUP * g + i:HEADS_PER_GROUP * g + i + 1], (rows, LANES))
            for i in range(HEADS_PER_GROUP)]
    lo = jnp.where(lane < HEAD_DIM, cols[0], cols[1])
    hi = jnp.where(lane < HEAD_DIM, cols[2], cols[3])
    return jnp.concatenate([lo, hi], axis=1)


def _ssd_prelude(dtr, valid, seg_len, dtb_ref, alog_ref):
    row = lax.broadcasted_iota(jnp.int32, (TILE, TILE), 0)
    col = lax.broadcasted_iota(jnp.int32, (TILE, TILE), 1)
    same_seq = (col // seg_len) == (row // seg_len)
    causal = (col <= row) & same_seq
    xdt = dtr + dtb_ref[...]
    e = jnp.exp(-jnp.abs(xdt))
    u = 1.0 + e
    dt = (jnp.maximum(xdt, 0.0) + jnp.where(u == 1.0, e, jnp.log(u) * e / (u - 1.0))) * valid
    da = dt * (-jnp.exp(alog_ref[...]))
    d1 = da.astype(BF16)
    r1 = da - d1.astype(F32)
    d2 = r1.astype(BF16)
    d3 = (r1 - d2.astype(F32)).astype(BF16)
    lower = jnp.where(causal, 1.0, 0.0).astype(BF16)
    upper = jnp.where((col > row) & same_seq, 1.0, 0.0).astype(BF16)
    cum = _dot(lower, d1) + _dot(lower, d2) + _dot(lower, d3)
    rest = _dot(upper, d1) + _dot(upper, d2) + _dot(upper, d3)
    return {
        "causal": causal,
        "cum": cum,
        "cum_t": cum.T,
        "dt_t": dt.T,
        "ecum": jnp.exp(cum),
        "tail": jnp.exp(rest) * dt,
    }


_PRE_KEYS = ("cum", "cum_t", "dt_t", "ecum", "tail")


def _ssd_group(g, pre, z_ref, xc_ref, seg_len, n_seq, h_in, h_out, dskip_ref, normw_ref, after_first_dot=None):
    lane = lax.broadcasted_iota(jnp.int32, (TILE, LANES), 1)
    lane_g = lax.broadcasted_iota(jnp.int32, (TILE, GROUP_W), 1)
    cum, cum_t, dt_t = pre["cum"], pre["cum_t"], pre["dt_t"]
    c0 = g * XBC_GROUP_W
    gs = slice(g * GROUP_W, (g + 1) * GROUP_W)
    xs = xc_ref[:, c0:c0 + GROUP_W]
    zg = z_ref[:, gs]
    bm16 = xc_ref[:, c0 + GROUP_W:c0 + GROUP_W + N_STATE].astype(BF16)
    cm16 = xc_ref[:, c0 + GROUP_W + N_STATE:c0 + XBC_GROUP_W].astype(BF16)
    cb = _dot_nt(cm16, bm16)
    extra = after_first_dot() if after_first_dot is not None else None
    seqs = [(s, slice(s * seg_len, (s + 1) * seg_len), (s + 1) * seg_len - 1) for s in range(n_seq)]
    ystate = [_dot_nt(cm16[rs], h_in(s, g).astype(BF16)) for s, rs, _ in seqs]
    if n_seq * seg_len < TILE:
        ystate.append(jnp.zeros((TILE - n_seq * seg_len, GROUP_W), F32))
    ys = ystate[0] if len(ystate) == 1 else jnp.concatenate(ystate, axis=0)
    ws, xms = [], []
    for i in range(HEADS_PER_GROUP):
        h = HEADS_PER_GROUP * g + i
        seg = cum[:, h:h + 1] - cum_t[h:h + 1, :]
        decay = jnp.exp(jnp.where(pre["causal"], seg, -jnp.inf))
        ws.append((cb * decay * dt_t[h:h + 1, :]).astype(BF16))
        xms.append(jnp.where((lane_g // HEAD_DIM) == i, xs, 0.0).astype(BF16))
    yg = _dot(jnp.concatenate(ws, axis=1), jnp.concatenate(xms, axis=0))
    yg = yg + ys * _expand_heads(pre["ecum"], g, lane)
    xw16 = (xs * _expand_heads(pre["tail"], g, lane)).astype(BF16)

    def update_state():
        for s, rs, last in seqs:
            upd = _dot_tn(xw16[rs], bm16[rs])
            dec = jnp.concatenate(
                [jnp.broadcast_to(jnp.exp(cum_t[HEADS_PER_GROUP * g + i:HEADS_PER_GROUP * g + i + 1, last:last + 1]),
                                  (HEAD_DIM, N_STATE)) for i in range(HEADS_PER_GROUP)], axis=0)
            h_out(s, g, h_in(s, g) * dec + upd)

    v = (yg + dskip_ref[:, gs] * xs) * _silu(zg)
    v = v * lax.rsqrt(jnp.mean(v * v, axis=-1, keepdims=True) + EPS) * normw_ref[:, gs]
    return v.astype(BF16), update_state, extra


def _out_proj_group(g, v16, wout_ref):
    return _dot(v16, wout_ref[g * GROUP_W:(g + 1) * GROUP_W, :])


def _ssm_prompt_kernel(head_ref, x_ref, nw_ref, wzx_ref, wdt_ref, cw_ref, cb_ref, dtb_ref, alog_ref, dskip_ref,
                       normw_ref, wout_ref, yb_ref, ns_ref, hs_ref, z_ref, cfull_ref, xc_ref, pre_ref):
    s = pl.program_id(1)
    n_tiles = pl.num_programs(1) - 1
    conv_dim = xc_ref.shape[-1]

    @pl.when(s == 0)
    def _():
        cfull_ref[0:SCONV_PAD, :] = jnp.zeros((SCONV_PAD, conv_dim), F32)
        hs_ref[...] = jnp.zeros(hs_ref.shape, F32)
        z_ref[...] = jnp.zeros(z_ref.shape, F32)
        xc_ref[...] = jnp.zeros(xc_ref.shape, F32)
        pre_ref[...] = jnp.zeros(pre_ref.shape, F32)

    x = jnp.where(s == 0, head_ref[...], x_ref[...])
    xn = _rms(x, nw_ref[...]).astype(BF16)
    pre = {k: pre_ref[i] for i, k in enumerate(_PRE_KEYS)}
    row = lax.broadcasted_iota(jnp.int32, (TILE, TILE), 0)
    col = lax.broadcasted_iota(jnp.int32, (TILE, TILE), 1)
    pre["causal"] = col <= row
    rowv = lax.broadcasted_iota(jnp.int32, (TILE, 1), 0)
    valid = jnp.where((s == 0) & (rowv < TILE - N_META), 0.0, 1.0)
    dtr = _dot(xn, wdt_ref[...])

    def h_in(q, g):
        return hs_ref[HEADS_PER_GROUP * g:HEADS_PER_GROUP * (g + 1)].reshape(GROUP_W, N_STATE)

    def h_out(q, g, v):
        hs_ref[HEADS_PER_GROUP * g:HEADS_PER_GROUP * (g + 1)] = v.reshape(HEADS_PER_GROUP, HEAD_DIM, N_STATE)

    def emit(cs, v):
        xc_ref[:, cs] = _silu(v)

    acc = jnp.zeros(yb_ref.shape, F32)
    prev = None
    for g in range(N_GROUPS):
        def independent_dots(g=g, prev=prev):
            _project_group(g, xn, wzx_ref, z_ref, cfull_ref, SCONV_PAD)
            if prev is None:
                return None
            prev_v16, prev_update = prev
            term = _out_proj_group(g - 1, prev_v16, wout_ref)
            prev_update()
            return term

        v16, update, term = _ssd_group(g, pre, z_ref, xc_ref, TILE, 1, h_in, h_out, dskip_ref, normw_ref,
                                       independent_dots)
        if term is not None:
            acc = acc + term
        _dwconv(cfull_ref, SCONV_PAD - (SSM_K - 1), TILE, cw_ref, cb_ref, SSM_K, XBC_GROUP_W, emit,
                col0=g * XBC_GROUP_W)
        prev = (v16, update)
    yb_ref[...] = acc + _out_proj_group(N_GROUPS - 1, prev[0], wout_ref)
    prev[1]()
    pre_next = _ssd_prelude(dtr, valid, TILE, dtb_ref, alog_ref)
    for i, k in enumerate(_PRE_KEYS):
        pre_ref[i] = pre_next[k]

    @pl.when(s == n_tiles - 1)
    def _():
        ns_ref[...] = cfull_ref[pl.ds(SCONV_PAD + TILE - (SSM_K - 1), SSM_K - 1), :]

    cfull_ref[0:SCONV_PAD, :] = cfull_ref[TILE:TILE + SCONV_PAD, :]


def _ssm_sample_proj_kernel(x_ref, st_ref, nw_ref, wzx_ref, wdt_ref, cw_ref, cb_ref,
                            z_ref, xc_ref, dtr_ref, ns_ref, xbc_ref, cs_ref, *, n_new):
    conv_dim = xc_ref.shape[-1]
    n_seq = TILE // SAMPLE_ROWS
    xn = _rms(x_ref[...], nw_ref[...]).astype(BF16)
    for g in range(N_GROUPS):
        _project_group(g, xn, wzx_ref, z_ref, xbc_ref, 0)
    dtr_ref[...] = _dot(xn, wdt_ref[...])

    def seq_body(s, carry):
        r0 = pl.multiple_of(s * SAMPLE_ROWS, SAMPLE_ROWS)
        cs_ref[0:SCONV_PAD, :] = st_ref[s]
        cs_ref[SCONV_PAD:SCONV_PAD + SAMPLE_ROWS, :] = xbc_ref[pl.ds(r0, SAMPLE_ROWS), :]

        def emit(cs, v):
            xc_ref[pl.ds(r0, SAMPLE_ROWS), cs] = _silu(v)

        _dwconv(cs_ref, SCONV_PAD - (SSM_K - 1), SAMPLE_ROWS, cw_ref, cb_ref, SSM_K, conv_dim, emit)
        ns_ref[s] = cs_ref[pl.ds(SCONV_PAD + n_new - (SSM_K - 1), SSM_K - 1), :]
        return carry

    lax.fori_loop(0, n_seq, seq_body, 0)


def _ssm_sample_ssd_kernel(z_in, xc_in, dtr_in, hin_ref, dtb_ref, alog_ref, dskip_ref, normw_ref, wout_ref,
                           yb_ref, hout_ref, z_ref, xc_ref, dtr_ref, *, n_new, n_seq):
    rows = n_seq * SAMPLE_ROWS
    for src, dst in ((z_in, z_ref), (xc_in, xc_ref), (dtr_in, dtr_ref)):
        dst[0:rows, :] = src[...]
        dst[rows:TILE, :] = jnp.zeros((TILE - rows, dst.shape[-1]), F32)
    rowv = lax.broadcasted_iota(jnp.int32, (TILE, 1), 0)
    valid = jnp.where((rowv < rows) & ((rowv % SAMPLE_ROWS) < n_new), 1.0, 0.0)

    def h_in(s, g):
        return hin_ref[s, HEADS_PER_GROUP * g:HEADS_PER_GROUP * (g + 1)].reshape(GROUP_W, N_STATE)

    def h_out(s, g, v):
        hout_ref[s, HEADS_PER_GROUP * g:HEADS_PER_GROUP * (g + 1)] = v.reshape(HEADS_PER_GROUP, HEAD_DIM, N_STATE)

    pre = _ssd_prelude(dtr_ref[...], valid, SAMPLE_ROWS, dtb_ref, alog_ref)
    acc = jnp.zeros((TILE, yb_ref.shape[-1]), F32)
    for g in range(N_GROUPS):
        v16, update, _ = _ssd_group(g, pre, z_ref, xc_ref, SAMPLE_ROWS, n_seq, h_in, h_out, dskip_ref, normw_ref)
        acc = acc + _out_proj_group(g, v16, wout_ref)
        update()
    yb_ref[...] = acc[0:rows, :]


def _ssm_prompt(head, x_prompt, p):
    b, l, d = x_prompt.shape
    nb = l // TILE
    d_inner = p["ssm_norm_w"].shape[-1]
    conv_dim = p["ssm_conv_b"].shape[-1]
    n_heads = d_inner // HEAD_DIM
    xmap = lambda i, j: (i, jnp.clip(j - 1, 0, nb - 1), 0)
    ymap = lambda i, j: (i, jnp.maximum(j - 2, 0), 0)
    weights = [p["norm_w"], p["w_zx"], p["w_dt"], p["ssm_conv_w"], p["ssm_conv_b"], p["dt_bias"], p["a_log"],
               p["d_skip"], p["ssm_norm_w"], p["ssm_out_w"]]
    wspecs = [_const_spec(w.shape) for w in weights]
    return pl.pallas_call(
        _ssm_prompt_kernel,
        grid=(b, nb + 2),
        in_specs=[_const_spec((TILE, d)), pl.BlockSpec((None, TILE, d), xmap)] + wspecs,
        out_specs=[pl.BlockSpec((None, TILE, d), ymap),
                   pl.BlockSpec((None, SSM_K - 1, conv_dim), lambda i, j: (i, 0, 0)),
                   pl.BlockSpec((None, n_heads, HEAD_DIM, N_STATE), lambda i, j: (i, 0, 0, 0))],
        out_shape=[jax.ShapeDtypeStruct((b, l, d), F32), jax.ShapeDtypeStruct((b, SSM_K - 1, conv_dim), F32),
                   jax.ShapeDtypeStruct((b, n_heads, HEAD_DIM, N_STATE), F32)],
        scratch_shapes=[pltpu.VMEM((TILE, d_inner), F32), pltpu.VMEM((SCONV_PAD + TILE, conv_dim), F32),
                        pltpu.VMEM((TILE, conv_dim), F32), pltpu.VMEM((len(_PRE_KEYS), TILE, LANES), F32)],
        compiler_params=pltpu.CompilerParams(dimension_semantics=("arbitrary", "arbitrary"),
                                             vmem_limit_bytes=VMEM_LIMIT),
        name="ssm_prompt",
    )(head, x_prompt, *weights)


def _ssm_sample(x_rows, sconv_pad, h0, p, n_new):
    rows, d = x_rows.shape
    d_inner = p["ssm_norm_w"].shape[-1]
    conv_dim = p["ssm_conv_b"].shape[-1]
    n_heads = d_inner // HEAD_DIM
    n_seq_proj = TILE // SAMPLE_ROWS
    weights = [p["norm_w"], p["w_zx"], p["w_dt"], p["ssm_conv_w"], p["ssm_conv_b"]]
    z, xc, dtr, new_sconv = pl.pallas_call(
        functools.partial(_ssm_sample_proj_kernel, n_new=n_new),
        grid=(rows // TILE,),
        in_specs=[pl.BlockSpec((TILE, d), lambda i: (i, 0)),
                  pl.BlockSpec((n_seq_proj, SCONV_PAD, conv_dim), lambda i: (i, 0, 0))]
        + [_const_spec(w.shape) for w in weights],
        out_specs=[pl.BlockSpec((TILE, d_inner), lambda i: (i, 0)), pl.BlockSpec((TILE, conv_dim), lambda i: (i, 0)),
                   pl.BlockSpec((TILE, LANES), lambda i: (i, 0)),
                   pl.BlockSpec((n_seq_proj, SSM_K - 1, conv_dim), lambda i: (i, 0, 0))],
        out_shape=[jax.ShapeDtypeStruct((rows, d_inner), F32), jax.ShapeDtypeStruct((rows, conv_dim), F32),
                   jax.ShapeDtypeStruct((rows, LANES), F32),
                   jax.ShapeDtypeStruct((rows // SAMPLE_ROWS, SSM_K - 1, conv_dim), F32)],
        scratch_shapes=[pltpu.VMEM((TILE, conv_dim), F32), pltpu.VMEM((SCONV_PAD + SAMPLE_ROWS, conv_dim), F32)],
        compiler_params=pltpu.CompilerParams(dimension_semantics=("arbitrary",), vmem_limit_bytes=VMEM_LIMIT),
        name="ssm_sample_proj",
    )(x_rows, sconv_pad, *weights)

    n_seq = 8 if rows % (8 * SAMPLE_ROWS) == 0 else 4
    r = n_seq * SAMPLE_ROWS
    weights = [p["dt_bias"], p["a_log"], p["d_skip"], p["ssm_norm_w"], p["ssm_out_w"]]
    yb, h_new = pl.pallas_call(
        functools.partial(_ssm_sample_ssd_kernel, n_new=n_new, n_seq=n_seq),
        grid=(rows // r,),
        in_specs=[pl.BlockSpec((r, d_inner), lambda i: (i, 0)), pl.BlockSpec((r, conv_dim), lambda i: (i, 0)),
                  pl.BlockSpec((r, LANES), lambda i: (i, 0)),
                  pl.BlockSpec((n_seq, n_heads, HEAD_DIM, N_STATE), lambda i: (i, 0, 0, 0))]
        + [_const_spec(w.shape) for w in weights],
        out_specs=[pl.BlockSpec((r, d), lambda i: (i, 0)),
                   pl.BlockSpec((n_seq, n_heads, HEAD_DIM, N_STATE), lambda i: (i, 0, 0, 0))],
        out_shape=[jax.ShapeDtypeStruct((rows, d), F32), jax.ShapeDtypeStruct(h0.shape, F32)],
        scratch_shapes=[pltpu.VMEM((TILE, d_inner), F32), pltpu.VMEM((TILE, conv_dim), F32),
                        pltpu.VMEM((TILE, LANES), F32)],
        compiler_params=pltpu.CompilerParams(dimension_semantics=("arbitrary",), vmem_limit_bytes=VMEM_LIMIT),
        name="ssm_sample_ssd",
    )(z, xc, dtr, h0, *weights)
    return yb, new_sconv, h_new


def _merge_kernel(x_ref, ya_ref, yb_ref, nw_ref, wg_ref, gb_ref, wo_ref, nfw_ref, rwh_ref, rwl_ref, rb_ref,
                  x1_ref, xn2_ref, comb_ref):
    d = x_ref.shape[-1]
    half = x_ref.shape[0] // 2
    parts = [slice(0, half), slice(half, 2 * half)]
    xs = [x_ref[rs, :] for rs in parts]
    xns = [_rms(x, nw_ref[...]).astype(BF16) for x in xs]
    glog = [_dot(xn, wg_ref[...]) for xn in xns]
    merged = []
    for rs, gl_ in zip(parts, glog):
        gates = _sigmoid(gl_ + gb_ref[...])
        merged.append((gates[:, :d] * ya_ref[rs, :] + gates[:, d:] * yb_ref[rs, :]).astype(BF16))
    outs = [_dot(m, wo_ref[...]) for m in merged]
    logit_parts = []
    for rs, x, o in zip(parts, xs, outs):
        x1 = x + o
        x1_ref[rs, :] = x1
        xn2 = _rms(x1, nfw_ref[...])
        x_hi = xn2.astype(BF16)
        xn2_ref[rs, :] = x_hi
        x_lo = (xn2 - x_hi.astype(F32)).astype(BF16)
        logit_parts.append(_dot(x_hi, rwh_ref[...]) + _dot(x_lo, rwh_ref[...]) + _dot(x_hi, rwl_ref[...]))

    logits = jnp.concatenate(logit_parts, axis=0) + rb_ref[...]
    rows = logits.shape[0]
    lane = lax.broadcasted_iota(jnp.int32, (rows, LANES), 1)
    neg = -jnp.inf
    is_g = (lane >= N_EXPERTS) & (lane < N_EXPERTS + N_EXPERT_GROUPS)
    gl = jnp.where(is_g, logits, neg)
    gmax = jnp.max(gl, axis=-1, keepdims=True)
    gsel = jnp.min(jnp.where(gl == gmax, lane, LANES), axis=-1, keepdims=True) - N_EXPERTS
    gprob = 1.0 / jnp.sum(jnp.exp(gl - gmax), axis=-1, keepdims=True)
    el = jnp.where((lane < N_EXPERTS) & ((lane // EXPERTS_PER_GROUP) == gsel), logits, neg)
    m1 = jnp.max(el, axis=-1, keepdims=True)
    i1 = jnp.min(jnp.where(el == m1, lane, LANES), axis=-1, keepdims=True)
    el2 = jnp.where(lane == i1, neg, el)
    m2 = jnp.max(el2, axis=-1, keepdims=True)
    i2 = jnp.min(jnp.where(el2 == m2, lane, LANES), axis=-1, keepdims=True)
    e2 = jnp.exp(m2 - m1)
    den = 1.0 + e2
    comb = jnp.where(lane == i1, (1.0 / den) * gprob, jnp.where(lane == i2, (e2 / den) * gprob, 0.0))
    comb_ref[...] = jnp.where(lane == GSEL_LANE, gsel.astype(F32), comb)


def _merge(x, ya, yb, p, tm):
    rows, d = x.shape
    weights = [p["norm_w"], p["w_gate"], p["gate_b"], p["w_o"], p["norm_ffn_w"], p["router_w_hi"],
               p["router_w_lo"], p["router_b"]]
    row_spec = pl.BlockSpec((tm, d), lambda i: (i, 0))
    return pl.pallas_call(
        _merge_kernel,
        grid=(rows // tm,),
        in_specs=[row_spec, row_spec, row_spec] + [_const_spec(w.shape) for w in weights],
        out_specs=[row_spec, row_spec, pl.BlockSpec((tm, LANES), lambda i: (i, 0))],
        out_shape=[jax.ShapeDtypeStruct((rows, d), F32), jax.ShapeDtypeStruct((rows, d), BF16),
                   jax.ShapeDtypeStruct((rows, LANES), F32)],
        compiler_params=pltpu.CompilerParams(dimension_semantics=("arbitrary",), vmem_limit_bytes=VMEM_LIMIT),
        name="merge_router",
    )(x, ya, yb, *weights)


def _moe_kernel(xn_ref, comb_ref, x1_ref, w1_ref, w3_ref, w2_ref, nw_ref, y_ref, rt_ref, *, chunk):
    g = pl.program_id(1)
    tm = xn_ref.shape[0]
    comb = comb_ref[...]
    lane = lax.broadcasted_iota(jnp.int32, (tm, LANES), 1)
    gsel = comb[:, GSEL_LANE:GSEL_LANE + 1]

    @pl.when(g == 0)
    def _():
        row = lax.broadcasted_iota(jnp.int32, (tm, LANES), 0)
        onehot = jnp.where(lane.astype(F32) == gsel, 1.0, 0.0)
        incl = _seg_cumsum(onehot, row, tm, reverse=False)
        rank = jnp.sum(onehot * (incl - onehot), axis=-1, keepdims=True)
        packed = jnp.where(lane == 0, rank, jnp.where(lane == 1, gsel, 0.0))
        rt_ref[0:SUBLANES, :] = packed.T[0:SUBLANES, :]
        rt_ref[SUBLANES:2 * SUBLANES, 0:LANES] = jnp.broadcast_to(incl[tm - 1:tm, :], (SUBLANES, LANES))
        y_ref[...] = jnp.zeros(y_ref.shape, F32)

    gf = g.astype(F32)
    lane_row = lax.broadcasted_iota(jnp.int32, (1, LANES), 1)
    count = jnp.sum(jnp.where(lane_row == g, rt_ref[SUBLANES:SUBLANES + 1, 0:LANES], 0.0), axis=-1, keepdims=True)
    n_chunks = (count[0, 0].astype(jnp.int32) + chunk - 1) // chunk
    w_hi = jnp.where(lane < N_EXPERTS, comb, 0.0).astype(BF16)
    w_lo = (jnp.where(lane < N_EXPERTS, comb, 0.0) - w_hi.astype(F32)).astype(BF16)
    rowi = lax.broadcasted_iota(jnp.int32, (chunk, tm), 0).astype(F32)
    lane_c = lax.broadcasted_iota(jnp.int32, (chunk, LANES), 1)

    def chunk_body(c, carry):
        base = (c * chunk).astype(F32)
        sel = jnp.where((rt_ref[1:2, :] == gf) & ((rt_ref[0:1, :] - base) == rowi), 1.0, 0.0).astype(BF16)
        xc = _dot(sel, xn_ref[...]).astype(BF16)
        wc = _dot(sel, w_hi) + _dot(sel, w_lo)
        acts = []
        for e in range(EXPERTS_PER_GROUP):
            ce = jnp.sum(jnp.where(lane_c == g * EXPERTS_PER_GROUP + e, wc, 0.0), axis=-1, keepdims=True)
            acts.append((_silu(_dot(xc, w1_ref[e])) * _dot(xc, w3_ref[e]) * ce).astype(BF16))
        y = _dot(jnp.concatenate(acts, axis=1), w2_ref[...])
        y_ref[...] += _dot_tn(sel, y.astype(BF16))
        return carry

    lax.fori_loop(0, n_chunks, chunk_body, 0)

    @pl.when(g == pl.num_programs(1) - 1)
    def _():
        y_ref[...] = _rms(x1_ref[...] + y_ref[...], nw_ref[...])


def _moe(xn2, comb, x1, p, tm):
    rows, d = x1.shape
    n_e, _, d_e = p["exp_w1"].shape
    n_g = n_e // EXPERTS_PER_GROUP
    w2 = p["exp_w2"].reshape(n_g, EXPERTS_PER_GROUP * d_e, d)
    pack = 2 * SUBLANES
    chunk = -(-(tm * 9 // (8 * n_g)) // pack) * pack
    row = lambda i, g: (i, 0)
    return pl.pallas_call(
        functools.partial(_moe_kernel, chunk=chunk),
        grid=(rows // tm, n_g),
        in_specs=[pl.BlockSpec((tm, d), row), pl.BlockSpec((tm, LANES), row),
                  pl.BlockSpec((tm, d), row, pipeline_mode=pl.Buffered(1)),
                  pl.BlockSpec((EXPERTS_PER_GROUP, d, d_e), lambda i, g: (g, 0, 0)),
                  pl.BlockSpec((EXPERTS_PER_GROUP, d, d_e), lambda i, g: (g, 0, 0)),
                  pl.BlockSpec((None, EXPERTS_PER_GROUP * d_e, d), lambda i, g: (g, 0, 0)),
                  _const_spec((1, d))],
        out_specs=pl.BlockSpec((tm, d), row),
        out_shape=jax.ShapeDtypeStruct((rows, d), F32),
        scratch_shapes=[pltpu.VMEM((2 * SUBLANES, tm), F32)],
        compiler_params=pltpu.CompilerParams(dimension_semantics=("arbitrary", "arbitrary"),
                                             vmem_limit_bytes=VMEM_LIMIT),
        name="moe",
    )(xn2, comb, x1, p["exp_w1"], p["exp_w3"], w2, p["norm_final_w"])


def _group_xbc_cols(v, d_inner):
    lead = v.shape[:-1]
    xs = v[..., :d_inner].reshape(lead + (N_GROUPS, GROUP_W))
    bm = v[..., d_inner:d_inner + N_GROUPS * N_STATE].reshape(lead + (N_GROUPS, N_STATE))
    cm = v[..., d_inner + N_GROUPS * N_STATE:].reshape(lead + (N_GROUPS, N_STATE))
    return jnp.concatenate([xs, bm, cm], axis=-1).reshape(lead + (-1,))


def _ungroup_xbc_cols(v, d_inner):
    lead = v.shape[:-1]
    v = v.reshape(lead + (N_GROUPS, XBC_GROUP_W))
    parts = [v[..., :GROUP_W], v[..., GROUP_W:GROUP_W + N_STATE], v[..., GROUP_W + N_STATE:]]
    return jnp.concatenate([q.reshape(lead + (-1,)) for q in parts], axis=-1)


def _prep_params(norm_mix_w, w_in, conf_dw_w, conf_dw_b, conf_ln_g, conf_ln_b, conf_out_w, conf_out_b, ssm_conv_w,
                 ssm_conv_b, dt_bias, a_log, d_skip, ssm_norm_w, ssm_out_w, gate_b, w_o, norm_ffn_w,
                 router_group_w, router_group_b, router_expert_w, router_expert_b, exp_w1, exp_w3, exp_w2,
                 norm_final_w):
    d = norm_mix_w.shape[-1]
    d_inner = ssm_norm_w.shape[-1]
    conv_dim = ssm_conv_b.shape[-1]
    n_heads = dt_bias.shape[-1]
    s1, s2 = d, 2 * d
    s3 = s2 + d_inner
    s4 = s3 + conv_dim
    s5 = s4 + n_heads
    w = w_in[0]
    row = lambda v: v.reshape(1, -1).astype(F32)
    pad_lanes = lambda v: jnp.pad(v, ((0, 0), (0, LANES - v.shape[-1])))
    router_w = pad_lanes(jnp.concatenate([router_expert_w[0], router_group_w[0]], axis=1))
    router_w_hi = router_w.astype(BF16)
    return {
        "norm_w": row(norm_mix_w[0]),
        "w_vg": w[:, :s2].astype(BF16),
        "w_zx": jnp.concatenate([w[:, s2:s3].reshape(d, N_GROUPS, GROUP_W),
                                 _group_xbc_cols(w[:, s3:s4], d_inner).reshape(d, N_GROUPS, XBC_GROUP_W)],
                                axis=-1).reshape(d, -1).astype(BF16),
        "w_dt": pad_lanes(w[:, s4:s5]).astype(BF16),
        "w_gate": w[:, s5:].astype(BF16),
        "conf_dw_w": jnp.pad(conf_dw_w[0], ((0, CONF_PAD - CONF_K), (0, 0))),
        "conf_dw_b": row(conf_dw_b[0]),
        "conf_ln_g": row(conf_ln_g[0]),
        "conf_ln_b": row(conf_ln_b[0]),
        "conf_out_w": conf_out_w[0].astype(BF16),
        "conf_out_b": row(conf_out_b[0]),
        "ssm_conv_w": jnp.pad(_group_xbc_cols(ssm_conv_w[0], d_inner), ((0, SUBLANES - SSM_K), (0, 0))),
        "ssm_conv_b": row(_group_xbc_cols(ssm_conv_b[0], d_inner)),
        "dt_bias": pad_lanes(row(dt_bias[0])),
        "a_log": pad_lanes(row(a_log[0])),
        "d_skip": row(jnp.repeat(d_skip[0], HEAD_DIM)),
        "ssm_norm_w": row(ssm_norm_w[0]),
        "ssm_out_w": ssm_out_w[0].astype(BF16),
        "gate_b": row(gate_b[0]),
        "w_o": w_o[0].astype(BF16),
        "norm_ffn_w": row(norm_ffn_w[0]),
        "router_w_hi": router_w_hi,
        "router_w_lo": (router_w - router_w_hi.astype(F32)).astype(BF16),
        "router_b": pad_lanes(row(jnp.concatenate([router_expert_b[0], router_group_b[0]]))),
        "exp_w1": exp_w1[0].astype(BF16),
        "exp_w3": exp_w3[0].astype(BF16),
        "exp_w2": exp_w2[0].astype(BF16),
        "norm_final_w": row(norm_final_w),
    }


def kernel(x_prompt, x_sample, state_conf_conv, state_ssm_conv, state_ssm, meta_tokens, norm_mix_w, w_in, conf_dw_w, conf_dw_b, conf_ln_g, conf_ln_b, conf_out_w, conf_out_b, ssm_conv_w, ssm_conv_b, dt_bias, a_log, d_skip, ssm_norm_w, ssm_out_w, gate_b, w_o, norm_ffn_w, router_group_w, router_group_b, router_expert_w, router_expert_b, exp_w1, exp_w3, exp_w2, norm_final_w):
    assert norm_mix_w.shape[0] == 1, "single-layer trunk"
    b, l, d = x_prompt.shape
    nb, n_new, _ = x_sample.shape
    assert l % TILE == 0 and n_new <= SAMPLE_ROWS and (nb * SAMPLE_ROWS) % TILE == 0
    p = _prep_params(norm_mix_w, w_in, conf_dw_w, conf_dw_b, conf_ln_g, conf_ln_b, conf_out_w, conf_out_b,
                     ssm_conv_w, ssm_conv_b, dt_bias, a_log, d_skip, ssm_norm_w, ssm_out_w, gate_b, w_o,
                     norm_ffn_w, router_group_w, router_group_b, router_expert_w, router_expert_b, exp_w1, exp_w3,
                     exp_w2, norm_final_w)

    head = jnp.concatenate([jnp.zeros((TILE - N_META, d), F32), meta_tokens.astype(F32)], axis=0)
    ya_p, new_conf_p = _conf_prompt(head, x_prompt, p)
    yb_p, new_sconv_p, new_ssm_p = _ssm_prompt(head, x_prompt, p)

    xs_rows = jnp.pad(x_sample, ((0, 0), (0, SAMPLE_ROWS - n_new), (0, 0))).reshape(nb * SAMPLE_ROWS, d)
    conf_pad = jnp.pad(state_conf_conv[0], ((0, 0), (CONF_PAD - (CONF_K - 1), 0), (0, 0)))
    d_inner = ssm_norm_w.shape[-1]
    sconv_pad = jnp.pad(_group_xbc_cols(state_ssm_conv[0], d_inner), ((0, 0), (SCONV_PAD - (SSM_K - 1), 0), (0, 0)))
    ya_s, new_conf_s = _conf_sample(xs_rows, conf_pad, p, n_new)
    yb_s, new_sconv_s, new_ssm_s = _ssm_sample(xs_rows, sconv_pad, state_ssm[0], p, n_new)
    unpad = lambda v: v.reshape(nb, SAMPLE_ROWS, d)[:, :n_new].reshape(nb * n_new, d)

    outs = []
    for x, ya, yb in ((x_prompt.reshape(b * l, d), ya_p.reshape(b * l, d), yb_p.reshape(b * l, d)),
                      (x_sample.reshape(nb * n_new, d), unpad(ya_s), unpad(yb_s))):
        tm = next(t for t in (1024, 512, TILE) if x.shape[0] % t == 0)
        x1, xn2, comb = _merge(x, ya, yb, p, min(tm, 512))
        outs.append(_moe(xn2, comb, x1, p, tm))
    y_prompt = outs[0].reshape(b, l, d)
    y_sample = outs[1].reshape(nb, n_new, d)
    return (y_prompt, y_sample, new_conf_p[None], _ungroup_xbc_cols(new_sconv_p, d_inner)[None], new_ssm_p[None],
            new_conf_s[None], _ungroup_xbc_cols(new_sconv_s, d_inner)[None], new_ssm_s[None])
```

```python
import functools

import jax
import jax.numpy as jnp
from jax import lax
from jax.experimental import pallas as pl
from jax.experimental.pallas import tpu as pltpu

F32 = jnp.float32
BF16 = jnp.bfloat16
EPS = 1e-6

LANES = 128
SUBLANES = 8
TILE = 128
VMEM_LIMIT = 56 * 1024 * 1024

N_META = 16
CONF_K = 31
SSM_K = 4
HEAD_DIM = 64
N_STATE = 128
N_GROUPS = 8
HEADS_PER_GROUP = 4
GROUP_W = HEADS_PER_GROUP * HEAD_DIM
N_EXPERTS = 32
EXPERTS_PER_GROUP = 8
N_EXPERT_GROUPS = 4
SAMPLE_ROWS = 8
CONF_PAD = 32
SCONV_PAD = 8
XBC_GROUP_W = GROUP_W + 2 * N_STATE
ZX_GROUP_W = GROUP_W + XBC_GROUP_W
GSEL_LANE = 64


def _rms(x, w):
    return x * lax.rsqrt(jnp.mean(x * x, axis=-1, keepdims=True) + EPS) * w


def _sigmoid(x):
    return 0.5 * jnp.tanh(0.5 * x) + 0.5


def _silu(x):
    return x * _sigmoid(x)


def _dot(a, b):
    return jnp.dot(a, b, preferred_element_type=F32)


def _dot_nt(a, b):
    return lax.dot_general(a, b, (((1,), (1,)), ((), ())), preferred_element_type=F32)


def _dot_tn(a, b):
    return lax.dot_general(a, b, (((0,), (0,)), ((), ())), preferred_element_type=F32)


def _const_spec(shape):
    zeros = (0,) * len(shape)
    return pl.BlockSpec(shape, lambda *_: zeros, pipeline_mode=pl.Buffered(1))


def _dwconv(src_ref, base, rows, w_ref, b_ref, taps, width, emit, col0=0):
    for c in range(col0 // LANES, (col0 + width) // LANES):
        cs = slice(c * LANES, (c + 1) * LANES)
        acc = jnp.broadcast_to(b_ref[:, cs], (rows, LANES))
        for r in range(min(SUBLANES, taps)):
            qs = range((taps - r + SUBLANES - 1) // SUBLANES)
            slab = src_ref[pl.ds(base + r, rows + SUBLANES * (len(qs) - 1)), cs]
            part = w_ref[r:r + 1, cs] * slab[0:rows]
            for q in qs[1:]:
                k = SUBLANES * q + r
                part = part + w_ref[k:k + 1, cs] * slab[SUBLANES * q:SUBLANES * q + rows]
            acc = acc + part
        emit(cs, acc)


def _conf_tail(conv_ref, ln_g_ref, ln_b_ref, wout_ref, bout_ref, ya_ref):
    c = conv_ref[...]
    mu = jnp.mean(c, axis=-1, keepdims=True)
    d = c - mu
    var = jnp.mean(d * d, axis=-1, keepdims=True)
    y = d * lax.rsqrt(var + EPS) * ln_g_ref[...] + ln_b_ref[...]
    y = _silu(y)
    ya_ref[...] = _dot(y.astype(BF16), wout_ref[...]) + bout_ref[...]


def _conf_glu(x, nw_ref, wvg_ref):
    d = x.shape[-1]
    xn = _rms(x, nw_ref[...]).astype(BF16)
    vg = _dot(xn, wvg_ref[...])
    return vg[:, :d] * _sigmoid(vg[:, d:])


def _conf_prompt_kernel(meta_ref, x_ref, nw_ref, wvg_ref, dww_ref, dwb_ref, lng_ref, lnb_ref, wout_ref, bout_ref,
                        ya_ref, nc_ref, afull_ref, conv_ref):
    j = pl.program_id(1)
    d = x_ref.shape[-1]
    rows = x_ref.shape[0]

    @pl.when(j == 0)
    def _():
        afull_ref[0:CONF_PAD - N_META, :] = jnp.zeros((CONF_PAD - N_META, d), F32)
        afull_ref[CONF_PAD - N_META:CONF_PAD, :] = _conf_glu(meta_ref[...], nw_ref, wvg_ref)

    afull_ref[CONF_PAD:CONF_PAD + rows, :] = _conf_glu(x_ref[...], nw_ref, wvg_ref)
    for r0 in range(0, rows, TILE):
        def emit(cs, v, r0=r0):
            conv_ref[r0:r0 + TILE, cs] = v

        _dwconv(afull_ref, CONF_PAD - (CONF_K - 1) + r0, TILE, dww_ref, dwb_ref, CONF_K, d, emit)
    nc_ref[...] = afull_ref[pl.ds(CONF_PAD + rows - (CONF_K - 1), CONF_K - 1), :]
    afull_ref[0:CONF_PAD, :] = afull_ref[rows:rows + CONF_PAD, :]
    _conf_tail(conv_ref, lng_ref, lnb_ref, wout_ref, bout_ref, ya_ref)


def _conf_sample_kernel(x_ref, st_ref, nw_ref, wvg_ref, dww_ref, dwb_ref, lng_ref, lnb_ref, wout_ref, bout_ref,
                        ya_ref, nc_ref, a_ref, cs_ref, conv_ref, *, n_new):
    d = x_ref.shape[-1]
    n_seq = TILE // SAMPLE_ROWS
    a_ref[...] = _conf_glu(x_ref[...], nw_ref, wvg_ref)

    def seq_body(s, carry):
        r0 = pl.multiple_of(s * SAMPLE_ROWS, SAMPLE_ROWS)
        cs_ref[0:CONF_PAD, :] = st_ref[s]
        cs_ref[CONF_PAD:CONF_PAD + SAMPLE_ROWS, :] = a_ref[pl.ds(r0, SAMPLE_ROWS), :]

        def emit(cs, v):
            conv_ref[pl.ds(r0, SAMPLE_ROWS), cs] = v

        _dwconv(cs_ref, CONF_PAD - (CONF_K - 1), SAMPLE_ROWS, dww_ref, dwb_ref, CONF_K, d, emit)
        nc_ref[s] = cs_ref[pl.ds(CONF_PAD + n_new - (CONF_K - 1), CONF_K - 1), :]
        return carry

    lax.fori_loop(0, n_seq, seq_body, 0)
    _conf_tail(conv_ref, lng_ref, lnb_ref, wout_ref, bout_ref, ya_ref)


def _conf_weights(p):
    d = p["norm_w"].shape[-1]
    return [p["norm_w"], p["w_vg"], p["conf_dw_w"], p["conf_dw_b"], p["conf_ln_g"], p["conf_ln_b"],
            p["conf_out_w"], p["conf_out_b"]], [
        _const_spec((1, d)), _const_spec((d, 2 * d)), _const_spec((CONF_PAD, d)), _const_spec((1, d)),
        _const_spec((1, d)), _const_spec((1, d)), _const_spec((d, d)), _const_spec((1, d))]


def _conf_prompt(meta, x_prompt, p):
    b, l, d = x_prompt.shape
    rows = 2 * TILE if l % (2 * TILE) == 0 else TILE
    weights, wspecs = _conf_weights(p)
    xmap = lambda i, j: (i, j, 0)
    return pl.pallas_call(
        _conf_prompt_kernel,
        grid=(b, l // rows),
        in_specs=[_const_spec(meta.shape), pl.BlockSpec((None, rows, d), xmap)] + wspecs,
        out_specs=[pl.BlockSpec((None, rows, d), xmap),
                   pl.BlockSpec((None, CONF_K - 1, d), lambda i, j: (i, 0, 0))],
        out_shape=[jax.ShapeDtypeStruct((b, l, d), F32), jax.ShapeDtypeStruct((b, CONF_K - 1, d), F32)],
        scratch_shapes=[pltpu.VMEM((CONF_PAD + rows, d), F32), pltpu.VMEM((rows, d), F32)],
        compiler_params=pltpu.CompilerParams(dimension_semantics=("arbitrary", "arbitrary"),
                                             vmem_limit_bytes=VMEM_LIMIT),
        name="conf_prompt",
    )(meta, x_prompt, *weights)


def _conf_sample(x_rows, state_pad, p, n_new):
    rows, d = x_rows.shape
    n_seq = TILE // SAMPLE_ROWS
    weights, wspecs = _conf_weights(p)
    return pl.pallas_call(
        functools.partial(_conf_sample_kernel, n_new=n_new),
        grid=(rows // TILE,),
        in_specs=[pl.BlockSpec((TILE, d), lambda i: (i, 0)),
                  pl.BlockSpec((n_seq, CONF_PAD, d), lambda i: (i, 0, 0))] + wspecs,
        out_specs=[pl.BlockSpec((TILE, d), lambda i: (i, 0)),
                   pl.BlockSpec((n_seq, CONF_K - 1, d), lambda i: (i, 0, 0))],
        out_shape=[jax.ShapeDtypeStruct((rows, d), F32),
                   jax.ShapeDtypeStruct((rows // SAMPLE_ROWS, CONF_K - 1, d), F32)],
        scratch_shapes=[pltpu.VMEM((TILE, d), F32), pltpu.VMEM((CONF_PAD + SAMPLE_ROWS, d), F32),
                        pltpu.VMEM((TILE, d), F32)],
        compiler_params=pltpu.CompilerParams(dimension_semantics=("arbitrary",), vmem_limit_bytes=VMEM_LIMIT),
        name="conf_sample",
    )(x_rows, state_pad, *weights)


def _project_group(g, xn, wzx_ref, z_ref, pre_ref, row0):
    zx = _dot(xn, wzx_ref[:, g * ZX_GROUP_W:(g + 1) * ZX_GROUP_W])
    z_ref[:, g * GROUP_W:(g + 1) * GROUP_W] = zx[:, :GROUP_W]
    pre_ref[row0:row0 + xn.shape[0], g * XBC_GROUP_W:(g + 1) * XBC_GROUP_W] = zx[:, GROUP_W:]


def _seg_cumsum(x, row, seg_len, reverse):
    pos = row % seg_len
    n = x.shape[0]
    step = 1
    while step < seg_len:
        if reverse:
            x = x + jnp.where(pos < seg_len - step, pltpu.roll(x, n - step, axis=0), 0.0)
        else:
            x = x + jnp.where(pos >= step, pltpu.roll(x, step, axis=0), 0.0)
        step *= 2
    return x


def _expand_heads(m, g, lane):
    rows = m.shape[0]
    cols = [jnp.broadcast_to(m[:, HEADS_PER_GROUP * g + i:HEADS_PER_GROUP * g + i + 1], (rows, LANES))
            for i in range(HEADS_PER_GROUP)]
    lo = jnp.where(lane < HEAD_DIM, cols[0], cols[1])
    hi = jnp.where(lane < HEAD_DIM, cols[2], cols[3])
    return jnp.concatenate([lo, hi], axis=1)


def _ssd_prelude(dtr, valid, seg_len, dtb_ref, alog_ref):
    row = lax.broadcasted_iota(jnp.int32, (TILE, TILE), 0)
    col = lax.broadcasted_iota(jnp.int32, (TILE, TILE), 1)
    same_seq = (col // seg_len) == (row // seg_len)
    causal = (col <= row) & same_seq
    xdt = dtr + dtb_ref[...]
    e = jnp.exp(-jnp.abs(xdt))
    u = 1.0 + e
    dt = (jnp.maximum(xdt, 0.0) + jnp.where(u == 1.0, e, jnp.log(u) * e / (u - 1.0))) * valid
    da = dt * (-jnp.exp(alog_ref[...]))
    d1 = da.astype(BF16)
    r1 = da - d1.astype(F32)
    d2 = r1.astype(BF16)
    d3 = (r1 - d2.astype(F32)).astype(BF16)
    lower = jnp.where(causal, 1.0, 0.0).astype(BF16)
    upper = jnp.where((col > row) & same_seq, 1.0, 0.0).astype(BF16)
    cum = _dot(lower, d1) + _dot(lower, d2) + _dot(lower, d3)
    rest = _dot(upper, d1) + _dot(upper, d2) + _dot(upper, d3)
    return {
        "causal": causal,
        "cum": cum,
        "cum_t": cum.T,
        "dt_t": dt.T,
        "ecum": jnp.exp(cum),
        "tail": jnp.exp(rest) * dt,
    }


_PRE_KEYS = ("cum", "cum_t", "dt_t", "ecum", "tail")


def _ssd_group(g, pre, z_ref, xc_ref, seg_len, n_seq, h_in, h_out, dskip_ref, normw_ref, after_first_dot=None):
    lane = lax.broadcasted_iota(jnp.int32, (TILE, LANES), 1)
    lane_g = lax.broadcasted_iota(jnp.int32, (TILE, GROUP_W), 1)
    cum, cum_t, dt_t = pre["cum"], pre["cum_t"], pre["dt_t"]
    c0 = g * XBC_GROUP_W
    gs = slice(g * GROUP_W, (g + 1) * GROUP_W)
    xs = xc_ref[:, c0:c0 + GROUP_W]
    zg = z_ref[:, gs]
    bm16 = xc_ref[:, c0 + GROUP_W:c0 + GROUP_W + N_STATE].astype(BF16)
    cm16 = xc_ref[:, c0 + GROUP_W + N_STATE:c0 + XBC_GROUP_W].astype(BF16)
    cb = _dot_nt(cm16, bm16)
    extra = after_first_dot() if after_first_dot is not None else None
    seqs = [(s, slice(s * seg_len, (s + 1) * seg_len), (s + 1) * seg_len - 1) for s in range(n_seq)]
    ystate = [_dot_nt(cm16[rs], h_in(s, g).astype(BF16)) for s, rs, _ in seqs]
    if n_seq * seg_len < TILE:
        ystate.append(jnp.zeros((TILE - n_seq * seg_len, GROUP_W), F32))
    ys = ystate[0] if len(ystate) == 1 else jnp.concatenate(ystate, axis=0)
    ws, xms = [], []
    for i in range(HEADS_PER_GROUP):
        h = HEADS_PER_GROUP * g + i
        seg = cum[:, h:h + 1] - cum_t[h:h + 1, :]
        decay = jnp.exp(jnp.where(pre["causal"], seg, -jnp.inf))
        ws.append((cb * decay * dt_t[h:h + 1, :]).astype(BF16))
        xms.append(jnp.where((lane_g // HEAD_DIM) == i, xs, 0.0).astype(BF16))
    yg = _dot(jnp.concatenate(ws, axis=1), jnp.concatenate(xms, axis=0))
    yg = yg + ys * _expand_heads(pre["ecum"], g, lane)
    xw16 = (xs * _expand_heads(pre["tail"], g, lane)).astype(BF16)

    def update_state():
        for s, rs, last in seqs:
            upd = _dot_tn(xw16[rs], bm16[rs])
            dec = jnp.concatenate(
                [jnp.broadcast_to(jnp.exp(cum_t[HEADS_PER_GROUP * g + i:HEADS_PER_GROUP * g + i + 1, last:last + 1]),
                                  (HEAD_DIM, N_STATE)) for i in range(HEADS_PER_GROUP)], axis=0)
            h_out(s, g, h_in(s, g) * dec + upd)

    v = (yg + dskip_ref[:, gs] * xs) * _silu(zg)
    v = v * lax.rsqrt(jnp.mean(v * v, axis=-1, keepdims=True) + EPS) * normw_ref[:, gs]
    return v.astype(BF16), update_state, extra


def _out_proj_group(g, v16, wout_ref):
    return _dot(v16, wout_ref[g * GROUP_W:(g + 1) * GROUP_W, :])


def _ssm_prompt_kernel(x_ref, hist_ref, h0_ref, nw_ref, wzx_ref, wdt_ref, cw_ref, cb_ref, dtb_ref, alog_ref,
                       dskip_ref, normw_ref, wout_ref, yb_ref, ns_ref, hs_ref, z_ref, cfull_ref, xc_ref, pre_ref, *,
                       n_inert):
    s = pl.program_id(1)
    n_tiles = pl.num_programs(1) - 1
    nq, _, d = x_ref.shape
    seqs = range(nq)

    @pl.when(s == 0)
    def _():
        for q in seqs:
            cfull_ref[q, 0:SCONV_PAD, :] = hist_ref[...]
            hs_ref[q] = h0_ref[...]
        z_ref[...] = jnp.zeros(z_ref.shape, F32)
        xc_ref[...] = jnp.zeros(xc_ref.shape, F32)
        pre_ref[...] = jnp.zeros(pre_ref.shape, F32)

    xn = _rms(x_ref[...].reshape(nq * TILE, d), nw_ref[...]).astype(BF16)
    row = lax.broadcasted_iota(jnp.int32, (TILE, TILE), 0)
    col = lax.broadcasted_iota(jnp.int32, (TILE, TILE), 1)
    pres = []
    for q in seqs:
        pre = {k: pre_ref[q, i] for i, k in enumerate(_PRE_KEYS)}
        pre["causal"] = col <= row
        pres.append(pre)
    rowv = lax.broadcasted_iota(jnp.int32, (TILE, 1), 0)
    valid = jnp.where((s == 0) & (rowv < n_inert), 0.0, 1.0)
    dtr = _dot(xn, wdt_ref[...])

    def state_access(q):
        def h_in(_, g):
            return hs_ref[q, HEADS_PER_GROUP * g:HEADS_PER_GROUP * (g + 1)].reshape(GROUP_W, N_STATE)

        def h_out(_, g, v):
            hs_ref[q, HEADS_PER_GROUP * g:HEADS_PER_GROUP * (g + 1)] = v.reshape(HEADS_PER_GROUP, HEAD_DIM, N_STATE)

        return h_in, h_out

    acc = [jnp.zeros((TILE, d), F32) for _ in seqs]
    prev = None
    for g in range(N_GROUPS):
        projected = {}
        for q in seqs:
            def independent_dots(g=g, q=q, prev=prev, projected=projected):
                if q == 0:
                    projected["zx"] = _dot(xn, wzx_ref[:, g * ZX_GROUP_W:(g + 1) * ZX_GROUP_W])
                zx = projected["zx"][q * TILE:(q + 1) * TILE]
                z_ref[q, :, g * GROUP_W:(g + 1) * GROUP_W] = zx[:, :GROUP_W]
                cfull_ref[q, SCONV_PAD:SCONV_PAD + TILE, g * XBC_GROUP_W:(g + 1) * XBC_GROUP_W] = zx[:, GROUP_W:]
                if prev is None:
                    return None
                prev_v16, prev_update, prev_q, prev_g = prev
                term = _out_proj_group(prev_g, prev_v16, wout_ref)
                prev_update()
                return prev_q, term

            h_in, h_out = state_access(q)
            v16, update, extra = _ssd_group(g, pres[q], z_ref.at[q], xc_ref.at[q], TILE, 1, h_in, h_out, dskip_ref,
                                            normw_ref, independent_dots)
            if extra is not None:
                acc[extra[0]] = acc[extra[0]] + extra[1]

            def emit(cs, v, q=q):
                xc_ref[q, :, cs] = _silu(v)

            _dwconv(cfull_ref.at[q], SCONV_PAD - (SSM_K - 1), TILE, cw_ref, cb_ref, SSM_K, XBC_GROUP_W, emit,
                    col0=g * XBC_GROUP_W)
            prev = (v16, update, q, g)
    acc[prev[2]] = acc[prev[2]] + _out_proj_group(prev[3], prev[0], wout_ref)
    prev[1]()
    for q in seqs:
        yb_ref[q] = acc[q]
        pre_next = _ssd_prelude(dtr[q * TILE:(q + 1) * TILE], valid, TILE, dtb_ref, alog_ref)
        for i, k in enumerate(_PRE_KEYS):
            pre_ref[q, i] = pre_next[k]

    @pl.when(s == n_tiles - 1)
    def _():
        for q in seqs:
            ns_ref[q] = cfull_ref[q, pl.ds(SCONV_PAD + TILE - (SSM_K - 1), SSM_K - 1), :]

    for q in seqs:
        cfull_ref[q, 0:SCONV_PAD, :] = cfull_ref[q, TILE:TILE + SCONV_PAD, :]


def _ssm_sample_proj_kernel(x_ref, st_ref, nw_ref, wzx_ref, wdt_ref, cw_ref, cb_ref,
                            z_ref, xc_ref, dtr_ref, ns_ref, xbc_ref, cs_ref, *, n_new):
    conv_dim = xc_ref.shape[-1]
    n_seq = TILE // SAMPLE_ROWS
    xn = _rms(x_ref[...], nw_ref[...]).astype(BF16)
    for g in range(N_GROUPS):
        _project_group(g, xn, wzx_ref, z_ref, xbc_ref, 0)
    dtr_ref[...] = _dot(xn, wdt_ref[...])

    def seq_body(s, carry):
        r0 = pl.multiple_of(s * SAMPLE_ROWS, SAMPLE_ROWS)
        cs_ref[0:SCONV_PAD, :] = st_ref[s]
        cs_ref[SCONV_PAD:SCONV_PAD + SAMPLE_ROWS, :] = xbc_ref[pl.ds(r0, SAMPLE_ROWS), :]

        def emit(cs, v):
            xc_ref[pl.ds(r0, SAMPLE_ROWS), cs] = _silu(v)

        _dwconv(cs_ref, SCONV_PAD - (SSM_K - 1), SAMPLE_ROWS, cw_ref, cb_ref, SSM_K, conv_dim, emit)
        ns_ref[s] = cs_ref[pl.ds(SCONV_PAD + n_new - (SSM_K - 1), SSM_K - 1), :]
        return carry

    lax.fori_loop(0, n_seq, seq_body, 0)


def _ssm_sample_ssd_kernel(z_in, xc_in, dtr_in, hin_ref, dtb_ref, alog_ref, dskip_ref, normw_ref, wout_ref,
                           yb_ref, hout_ref, z_ref, xc_ref, dtr_ref, *, n_new, n_seq):
    rows = n_seq * SAMPLE_ROWS
    for src, dst in ((z_in, z_ref), (xc_in, xc_ref), (dtr_in, dtr_ref)):
        dst[0:rows, :] = src[...]
        dst[rows:TILE, :] = jnp.zeros((TILE - rows, dst.shape[-1]), F32)
    rowv = lax.broadcasted_iota(jnp.int32, (TILE, 1), 0)
    valid = jnp.where((rowv < rows) & ((rowv % SAMPLE_ROWS) < n_new), 1.0, 0.0)

    def h_in(s, g):
        return hin_ref[s, HEADS_PER_GROUP * g:HEADS_PER_GROUP * (g + 1)].reshape(GROUP_W, N_STATE)

    def h_out(s, g, v):
        hout_ref[s, HEADS_PER_GROUP * g:HEADS_PER_GROUP * (g + 1)] = v.reshape(HEADS_PER_GROUP, HEAD_DIM, N_STATE)

    pre = _ssd_prelude(dtr_ref[...], valid, SAMPLE_ROWS, dtb_ref, alog_ref)
    acc = jnp.zeros((TILE, yb_ref.shape[-1]), F32)
    for g in range(N_GROUPS):
        v16, update, _ = _ssd_group(g, pre, z_ref, xc_ref, SAMPLE_ROWS, n_seq, h_in, h_out, dskip_ref, normw_ref)
        acc = acc + _out_proj_group(g, v16, wout_ref)
        update()
    yb_ref[...] = acc[0:rows, :]


def _ssm_prompt(x, hist, h0, p, n_inert):
    b, l, d = x.shape
    nb = l // TILE
    nq = 2 if b % 2 == 0 else 1
    d_inner = p["ssm_norm_w"].shape[-1]
    conv_dim = p["ssm_conv_b"].shape[-1]
    n_heads = d_inner // HEAD_DIM
    xmap = lambda i, j: (i, jnp.minimum(j, nb - 1), 0)
    ymap = lambda i, j: (i, jnp.maximum(j - 1, 0), 0)
    weights = [hist, h0, p["norm_w"], p["w_zx"], p["w_dt"], p["ssm_conv_w"], p["ssm_conv_b"], p["dt_bias"],
               p["a_log"], p["d_skip"], p["ssm_norm_w"], p["ssm_out_w"]]
    wspecs = [_const_spec(w.shape) for w in weights]
    return pl.pallas_call(
        functools.partial(_ssm_prompt_kernel, n_inert=n_inert),
        grid=(b // nq, nb + 1),
        in_specs=[pl.BlockSpec((nq, TILE, d), xmap)] + wspecs,
        out_specs=[pl.BlockSpec((nq, TILE, d), ymap),
                   pl.BlockSpec((nq, SSM_K - 1, conv_dim), lambda i, j: (i, 0, 0)),
                   pl.BlockSpec((nq, n_heads, HEAD_DIM, N_STATE), lambda i, j: (i, 0, 0, 0))],
        out_shape=[jax.ShapeDtypeStruct((b, l, d), F32), jax.ShapeDtypeStruct((b, SSM_K - 1, conv_dim), F32),
                   jax.ShapeDtypeStruct((b, n_heads, HEAD_DIM, N_STATE), F32)],
        scratch_shapes=[pltpu.VMEM((nq, TILE, d_inner), F32), pltpu.VMEM((nq, SCONV_PAD + TILE, conv_dim), F32),
                        pltpu.VMEM((nq, TILE, conv_dim), F32), pltpu.VMEM((nq, len(_PRE_KEYS), TILE, LANES), F32)],
        compiler_params=pltpu.CompilerParams(dimension_semantics=("arbitrary", "arbitrary"),
                                             vmem_limit_bytes=VMEM_LIMIT),
        name="ssm_prompt",
    )(x, *weights)


def _ssm_sample(x_rows, sconv_pad, h0, p, n_new):
    rows, d = x_rows.shape
    d_inner = p["ssm_norm_w"].shape[-1]
    conv_dim = p["ssm_conv_b"].shape[-1]
    n_heads = d_inner // HEAD_DIM
    n_seq_proj = TILE // SAMPLE_ROWS
    weights = [p["norm_w"], p["w_zx"], p["w_dt"], p["ssm_conv_w"], p["ssm_conv_b"]]
    z, xc, dtr, new_sconv = pl.pallas_call(
        functools.partial(_ssm_sample_proj_kernel, n_new=n_new),
        grid=(rows // TILE,),
        in_specs=[pl.BlockSpec((TILE, d), lambda i: (i, 0)),
                  pl.BlockSpec((n_seq_proj, SCONV_PAD, conv_dim), lambda i: (i, 0, 0))]
        + [_const_spec(w.shape) for w in weights],
        out_specs=[pl.BlockSpec((TILE, d_inner), lambda i: (i, 0)), pl.BlockSpec((TILE, conv_dim), lambda i: (i, 0)),
                   pl.BlockSpec((TILE, LANES), lambda i: (i, 0)),
                   pl.BlockSpec((n_seq_proj, SSM_K - 1, conv_dim), lambda i: (i, 0, 0))],
        out_shape=[jax.ShapeDtypeStruct((rows, d_inner), F32), jax.ShapeDtypeStruct((rows, conv_dim), F32),
                   jax.ShapeDtypeStruct((rows, LANES), F32),
                   jax.ShapeDtypeStruct((rows // SAMPLE_ROWS, SSM_K - 1, conv_dim), F32)],
        scratch_shapes=[pltpu.VMEM((TILE, conv_dim), F32), pltpu.VMEM((SCONV_PAD + SAMPLE_ROWS, conv_dim), F32)],
        compiler_params=pltpu.CompilerParams(dimension_semantics=("arbitrary",), vmem_limit_bytes=VMEM_LIMIT),
        name="ssm_sample_proj",
    )(x_rows, sconv_pad, *weights)

    n_seq = 8 if rows % (8 * SAMPLE_ROWS) == 0 else 4
    r = n_seq * SAMPLE_ROWS
    weights = [p["dt_bias"], p["a_log"], p["d_skip"], p["ssm_norm_w"], p["ssm_out_w"]]
    yb, h_new = pl.pallas_call(
        functools.partial(_ssm_sample_ssd_kernel, n_new=n_new, n_seq=n_seq),
        grid=(rows // r,),
        in_specs=[pl.BlockSpec((r, d_inner), lambda i: (i, 0)), pl.BlockSpec((r, conv_dim), lambda i: (i, 0)),
                  pl.BlockSpec((r, LANES), lambda i: (i, 0)),
                  pl.BlockSpec((n_seq, n_heads, HEAD_DIM, N_STATE), lambda i: (i, 0, 0, 0))]
        + [_const_spec(w.shape) for w in weights],
        out_specs=[pl.BlockSpec((r, d), lambda i: (i, 0)),
                   pl.BlockSpec((n_seq, n_heads, HEAD_DIM, N_STATE), lambda i: (i, 0, 0, 0))],
        out_shape=[jax.ShapeDtypeStruct((rows, d), F32), jax.ShapeDtypeStruct(h0.shape, F32)],
        scratch_shapes=[pltpu.VMEM((TILE, d_inner), F32), pltpu.VMEM((TILE, conv_dim), F32),
                        pltpu.VMEM((TILE, LANES), F32)],
        compiler_params=pltpu.CompilerParams(dimension_semantics=("arbitrary",), vmem_limit_bytes=VMEM_LIMIT),
        name="ssm_sample_ssd",
    )(z, xc, dtr, h0, *weights)
    return yb, new_sconv, h_new


def _merge_kernel(x_ref, ya_ref, yb_ref, nw_ref, wg_ref, gb_ref, wo_ref, nfw_ref, rwh_ref, rwl_ref, rb_ref,
                  x1_ref, xn2_ref, comb_ref):
    d = x_ref.shape[-1]
    half = x_ref.shape[0] // 2
    parts = [slice(0, half), slice(half, 2 * half)]
    xs = [x_ref[rs, :] for rs in parts]
    xns = [_rms(x, nw_ref[...]).astype(BF16) for x in xs]
    glog = [_dot(xn, wg_ref[...]) for xn in xns]
    merged = []
    for rs, gl_ in zip(parts, glog):
        gates = _sigmoid(gl_ + gb_ref[...])
        merged.append((gates[:, :d] * ya_ref[rs, :] + gates[:, d:] * yb_ref[rs, :]).astype(BF16))
    outs = [_dot(m, wo_ref[...]) for m in merged]
    logit_parts = []
    for rs, x, o in zip(parts, xs, outs):
        x1 = x + o
        x1_ref[rs, :] = x1
        xn2 = _rms(x1, nfw_ref[...])
        x_hi = xn2.astype(BF16)
        xn2_ref[rs, :] = x_hi
        x_lo = (xn2 - x_hi.astype(F32)).astype(BF16)
        logit_parts.append(_dot(x_hi, rwh_ref[...]) + _dot(x_lo, rwh_ref[...]) + _dot(x_hi, rwl_ref[...]))

    logits = jnp.concatenate(logit_parts, axis=0) + rb_ref[...]
    rows = logits.shape[0]
    lane = lax.broadcasted_iota(jnp.int32, (rows, LANES), 1)
    neg = -jnp.inf
    is_g = (lane >= N_EXPERTS) & (lane < N_EXPERTS + N_EXPERT_GROUPS)
    gl = jnp.where(is_g, logits, neg)
    gmax = jnp.max(gl, axis=-1, keepdims=True)
    gsel = jnp.min(jnp.where(gl == gmax, lane, LANES), axis=-1, keepdims=True) - N_EXPERTS
    gprob = 1.0 / jnp.sum(jnp.exp(gl - gmax), axis=-1, keepdims=True)
    el = jnp.where((lane < N_EXPERTS) & ((lane // EXPERTS_PER_GROUP) == gsel), logits, neg)
    m1 = jnp.max(el, axis=-1, keepdims=True)
    i1 = jnp.min(jnp.where(el == m1, lane, LANES), axis=-1, keepdims=True)
    el2 = jnp.where(lane == i1, neg, el)
    m2 = jnp.max(el2, axis=-1, keepdims=True)
    i2 = jnp.min(jnp.where(el2 == m2, lane, LANES), axis=-1, keepdims=True)
    e2 = jnp.exp(m2 - m1)
    den = 1.0 + e2
    comb = jnp.where(lane == i1, (1.0 / den) * gprob, jnp.where(lane == i2, (e2 / den) * gprob, 0.0))
    comb_ref[...] = jnp.where(lane == GSEL_LANE, gsel.astype(F32), comb)


def _merge(x, ya, yb, p, tm):
    rows, d = x.shape
    weights = [p["norm_w"], p["w_gate"], p["gate_b"], p["w_o"], p["norm_ffn_w"], p["router_w_hi"],
               p["router_w_lo"], p["router_b"]]
    row_spec = pl.BlockSpec((tm, d), lambda i: (i, 0))
    return pl.pallas_call(
        _merge_kernel,
        grid=(rows // tm,),
        in_specs=[row_spec, row_spec, row_spec] + [_const_spec(w.shape) for w in weights],
        out_specs=[row_spec, row_spec, pl.BlockSpec((tm, LANES), lambda i: (i, 0))],
        out_shape=[jax.ShapeDtypeStruct((rows, d), F32), jax.ShapeDtypeStruct((rows, d), BF16),
                   jax.ShapeDtypeStruct((rows, LANES), F32)],
        compiler_params=pltpu.CompilerParams(dimension_semantics=("arbitrary",), vmem_limit_bytes=VMEM_LIMIT),
        name="merge_router",
    )(x, ya, yb, *weights)


def _moe_kernel(xn_ref, comb_ref, x1_ref, w1_ref, w3_ref, w2_ref, nw_ref, y_ref, rt_ref, *, chunk):
    g = pl.program_id(1)
    tm = xn_ref.shape[0]
    comb = comb_ref[...]
    lane = lax.broadcasted_iota(jnp.int32, (tm, LANES), 1)
    gsel = comb[:, GSEL_LANE:GSEL_LANE + 1]

    @pl.when(g == 0)
    def _():
        row = lax.broadcasted_iota(jnp.int32, (tm, LANES), 0)
        onehot = jnp.where(lane.astype(F32) == gsel, 1.0, 0.0)
        incl = _seg_cumsum(onehot, row, tm, reverse=False)
        rank = jnp.sum(onehot * (incl - onehot), axis=-1, keepdims=True)
        packed = jnp.where(lane == 0, rank, jnp.where(lane == 1, gsel, 0.0))
        rt_ref[0:SUBLANES, :] = packed.T[0:SUBLANES, :]
        rt_ref[SUBLANES:2 * SUBLANES, 0:LANES] = jnp.broadcast_to(incl[tm - 1:tm, :], (SUBLANES, LANES))
        y_ref[...] = jnp.zeros(y_ref.shape, F32)

    gf = g.astype(F32)
    lane_row = lax.broadcasted_iota(jnp.int32, (1, LANES), 1)
    count = jnp.sum(jnp.where(lane_row == g, rt_ref[SUBLANES:SUBLANES + 1, 0:LANES], 0.0), axis=-1, keepdims=True)
    n_chunks = (count[0, 0].astype(jnp.int32) + chunk - 1) // chunk
    w_hi = jnp.where(lane < N_EXPERTS, comb, 0.0).astype(BF16)
    w_lo = (jnp.where(lane < N_EXPERTS, comb, 0.0) - w_hi.astype(F32)).astype(BF16)
    rowi = lax.broadcasted_iota(jnp.int32, (chunk, tm), 0).astype(F32)
    lane_c = lax.broadcasted_iota(jnp.int32, (chunk, LANES), 1)

    def chunk_body(c, carry):
        base = (c * chunk).astype(F32)
        sel = jnp.where((rt_ref[1:2, :] == gf) & ((rt_ref[0:1, :] - base) == rowi), 1.0, 0.0).astype(BF16)
        xc = _dot(sel, xn_ref[...]).astype(BF16)
        wc = _dot(sel, w_hi) + _dot(sel, w_lo)
        acts = []
        for e in range(EXPERTS_PER_GROUP):
            ce = jnp.sum(jnp.where(lane_c == g * EXPERTS_PER_GROUP + e, wc, 0.0), axis=-1, keepdims=True)
            acts.append((_silu(_dot(xc, w1_ref[e])) * _dot(xc, w3_ref[e]) * ce).astype(BF16))
        y = _dot(jnp.concatenate(acts, axis=1), w2_ref[...])
        y_ref[...] += _dot_tn(sel, y.astype(BF16))
        return carry

    lax.fori_loop(0, n_chunks, chunk_body, 0)

    @pl.when(g == pl.num_programs(1) - 1)
    def _():
        y_ref[...] = _rms(x1_ref[...] + y_ref[...], nw_ref[...])


def _moe(xn2, comb, x1, p, tm):
    rows, d = x1.shape
    n_e, _, d_e = p["exp_w1"].shape
    n_g = n_e // EXPERTS_PER_GROUP
    w2 = p["exp_w2"].reshape(n_g, EXPERTS_PER_GROUP * d_e, d)
    pack = 2 * SUBLANES
    chunk = -(-(tm * 9 // (8 * n_g)) // pack) * pack
    row = lambda i, g: (i, 0)
    return pl.pallas_call(
        functools.partial(_moe_kernel, chunk=chunk),
        grid=(rows // tm, n_g),
        in_specs=[pl.BlockSpec((tm, d), row), pl.BlockSpec((tm, LANES), row),
                  pl.BlockSpec((tm, d), row, pipeline_mode=pl.Buffered(1)),
                  pl.BlockSpec((EXPERTS_PER_GROUP, d, d_e), lambda i, g: (g, 0, 0)),
                  pl.BlockSpec((EXPERTS_PER_GROUP, d, d_e), lambda i, g: (g, 0, 0)),
                  pl.BlockSpec((None, EXPERTS_PER_GROUP * d_e, d), lambda i, g: (g, 0, 0)),
                  _const_spec((1, d))],
        out_specs=pl.BlockSpec((tm, d), row),
        out_shape=jax.ShapeDtypeStruct((rows, d), F32),
        scratch_shapes=[pltpu.VMEM((2 * SUBLANES, tm), F32)],
        compiler_params=pltpu.CompilerParams(dimension_semantics=("arbitrary", "arbitrary"),
                                             vmem_limit_bytes=VMEM_LIMIT),
        name="moe",
    )(xn2, comb, x1, p["exp_w1"], p["exp_w3"], w2, p["norm_final_w"])


def _group_xbc_cols(v, d_inner):
    lead = v.shape[:-1]
    xs = v[..., :d_inner].reshape(lead + (N_GROUPS, GROUP_W))
    bm = v[..., d_inner:d_inner + N_GROUPS * N_STATE].reshape(lead + (N_GROUPS, N_STATE))
    cm = v[..., d_inner + N_GROUPS * N_STATE:].reshape(lead + (N_GROUPS, N_STATE))
    return jnp.concatenate([xs, bm, cm], axis=-1).reshape(lead + (-1,))


def _ungroup_xbc_cols(v, d_inner):
    lead = v.shape[:-1]
    v = v.reshape(lead + (N_GROUPS, XBC_GROUP_W))
    parts = [v[..., :GROUP_W], v[..., GROUP_W:GROUP_W + N_STATE], v[..., GROUP_W + N_STATE:]]
    return jnp.concatenate([q.reshape(lead + (-1,)) for q in parts], axis=-1)


def _prep_params(norm_mix_w, w_in, conf_dw_w, conf_dw_b, conf_ln_g, conf_ln_b, conf_out_w, conf_out_b, ssm_conv_w,
                 ssm_conv_b, dt_bias, a_log, d_skip, ssm_norm_w, ssm_out_w, gate_b, w_o, norm_ffn_w,
                 router_group_w, router_group_b, router_expert_w, router_expert_b, exp_w1, exp_w3, exp_w2,
                 norm_final_w):
    d = norm_mix_w.shape[-1]
    d_inner = ssm_norm_w.shape[-1]
    conv_dim = ssm_conv_b.shape[-1]
    n_heads = dt_bias.shape[-1]
    s1, s2 = d, 2 * d
    s3 = s2 + d_inner
    s4 = s3 + conv_dim
    s5 = s4 + n_heads
    w = w_in[0]
    row = lambda v: v.reshape(1, -1).astype(F32)
    pad_lanes = lambda v: jnp.pad(v, ((0, 0), (0, LANES - v.shape[-1])))
    router_w = pad_lanes(jnp.concatenate([router_expert_w[0], router_group_w[0]], axis=1))
    router_w_hi = router_w.astype(BF16)
    return {
        "norm_w": row(norm_mix_w[0]),
        "w_vg": w[:, :s2].astype(BF16),
        "w_zx": jnp.concatenate([w[:, s2:s3].reshape(d, N_GROUPS, GROUP_W),
                                 _group_xbc_cols(w[:, s3:s4], d_inner).reshape(d, N_GROUPS, XBC_GROUP_W)],
                                axis=-1).reshape(d, -1).astype(BF16),
        "w_dt": pad_lanes(w[:, s4:s5]).astype(BF16),
        "w_gate": w[:, s5:].astype(BF16),
        "conf_dw_w": jnp.pad(conf_dw_w[0], ((0, CONF_PAD - CONF_K), (0, 0))),
        "conf_dw_b": row(conf_dw_b[0]),
        "conf_ln_g": row(conf_ln_g[0]),
        "conf_ln_b": row(conf_ln_b[0]),
        "conf_out_w": conf_out_w[0].astype(BF16),
        "conf_out_b": row(conf_out_b[0]),
        "ssm_conv_w": jnp.pad(_group_xbc_cols(ssm_conv_w[0], d_inner), ((0, SUBLANES - SSM_K), (0, 0))),
        "ssm_conv_b": row(_group_xbc_cols(ssm_conv_b[0], d_inner)),
        "dt_bias": pad_lanes(row(dt_bias[0])),
        "a_log": pad_lanes(row(a_log[0])),
        "d_skip": row(jnp.repeat(d_skip[0], HEAD_DIM)),
        "ssm_norm_w": row(ssm_norm_w[0]),
        "ssm_out_w": ssm_out_w[0].astype(BF16),
        "gate_b": row(gate_b[0]),
        "w_o": w_o[0].astype(BF16),
        "norm_ffn_w": row(norm_ffn_w[0]),
        "router_w_hi": router_w_hi,
        "router_w_lo": (router_w - router_w_hi.astype(F32)).astype(BF16),
        "router_b": pad_lanes(row(jnp.concatenate([router_expert_b[0], router_group_b[0]]))),
        "exp_w1": exp_w1[0].astype(BF16),
        "exp_w3": exp_w3[0].astype(BF16),
        "exp_w2": exp_w2[0].astype(BF16),
        "norm_final_w": row(norm_final_w),
    }


def kernel(x_prompt, x_sample, state_conf_conv, state_ssm_conv, state_ssm, meta_tokens, norm_mix_w, w_in, conf_dw_w, conf_dw_b, conf_ln_g, conf_ln_b, conf_out_w, conf_out_b, ssm_conv_w, ssm_conv_b, dt_bias, a_log, d_skip, ssm_norm_w, ssm_out_w, gate_b, w_o, norm_ffn_w, router_group_w, router_group_b, router_expert_w, router_expert_b, exp_w1, exp_w3, exp_w2, norm_final_w):
    assert norm_mix_w.shape[0] == 1, "single-layer trunk"
    b, l, d = x_prompt.shape
    nb, n_new, _ = x_sample.shape
    assert l % TILE == 0 and n_new <= SAMPLE_ROWS and (nb * SAMPLE_ROWS) % TILE == 0
    p = _prep_params(norm_mix_w, w_in, conf_dw_w, conf_dw_b, conf_ln_g, conf_ln_b, conf_out_w, conf_out_b,
                     ssm_conv_w, ssm_conv_b, dt_bias, a_log, d_skip, ssm_norm_w, ssm_out_w, gate_b, w_o,
                     norm_ffn_w, router_group_w, router_group_b, router_expert_w, router_expert_b, exp_w1, exp_w3,
                     exp_w2, norm_final_w)

    assert meta_tokens.shape[0] == N_META
    ya_p, new_conf_p = _conf_prompt(meta_tokens.astype(F32), x_prompt, p)
    d_inner = ssm_norm_w.shape[-1]
    conv_dim = ssm_conv_b.shape[-1]
    head = jnp.concatenate([jnp.zeros((TILE - N_META, d), F32), meta_tokens.astype(F32)], axis=0)
    _, meta_sconv, meta_ssm = _ssm_prompt(head[None], jnp.zeros((SCONV_PAD, conv_dim), F32),
                                          jnp.zeros(state_ssm.shape[2:], F32), p, TILE - N_META)
    meta_hist = jnp.pad(meta_sconv[0], ((SCONV_PAD - (SSM_K - 1), 0), (0, 0)))
    yb_p, new_sconv_p, new_ssm_p = _ssm_prompt(x_prompt, meta_hist, meta_ssm[0], p, 0)

    xs_rows = jnp.pad(x_sample, ((0, 0), (0, SAMPLE_ROWS - n_new), (0, 0))).reshape(nb * SAMPLE_ROWS, d)
    conf_pad = jnp.pad(state_conf_conv[0], ((0, 0), (CONF_PAD - (CONF_K - 1), 0), (0, 0)))
    sconv_pad = jnp.pad(_group_xbc_cols(state_ssm_conv[0], d_inner), ((0, 0), (SCONV_PAD - (SSM_K - 1), 0), (0, 0)))
    ya_s, new_conf_s = _conf_sample(xs_rows, conf_pad, p, n_new)
    yb_s, new_sconv_s, new_ssm_s = _ssm_sample(xs_rows, sconv_pad, state_ssm[0], p, n_new)
    unpad = lambda v: v.reshape(nb, SAMPLE_ROWS, d)[:, :n_new].reshape(nb * n_new, d)

    outs = []
    for x, ya, yb in ((x_prompt.reshape(b * l, d), ya_p.reshape(b * l, d), yb_p.reshape(b * l, d)),
                      (x_sample.reshape(nb * n_new, d), unpad(ya_s), unpad(yb_s))):
        tm = next(t for t in (1024, 512, TILE) if x.shape[0] % t == 0)
        x1, xn2, comb = _merge(x, ya, yb, p, min(tm, 512))
        outs.append(_moe(xn2, comb, x1, p, tm))
    y_prompt = outs[0].reshape(b, l, d)
    y_sample = outs[1].reshape(nb, n_new, d)
    return (y_prompt, y_sample, new_conf_p[None], _ungroup_xbc_cols(new_sconv_p, d_inner)[None], new_ssm_p[None],
            new_conf_s[None], _ungroup_xbc_cols(new_sconv_s, d_inner)[None], new_ssm_s[None])
```

```python
import functools

import jax
import jax.numpy as jnp
from jax import lax
from jax.experimental import pallas as pl
from jax.experimental.pallas import tpu as pltpu

F32 = jnp.float32
BF16 = jnp.bfloat16
EPS = 1e-6

LANES = 128
SUBLANES = 8
TILE = 128
VMEM_LIMIT = 56 * 1024 * 1024

N_META = 16
CONF_K = 31
SSM_K = 4
HEAD_DIM = 64
N_STATE = 128
N_GROUPS = 8
HEADS_PER_GROUP = 4
GROUP_W = HEADS_PER_GROUP * HEAD_DIM
N_EXPERTS = 32
EXPERTS_PER_GROUP = 8
N_EXPERT_GROUPS = 4
SAMPLE_ROWS = 8
CONF_PAD = 32
SCONV_PAD = 8
XBC_GROUP_W = GROUP_W + 2 * N_STATE
ZX_GROUP_W = GROUP_W + XBC_GROUP_W
GSEL_LANE = 64


def _rms(x, w):
    return x * lax.rsqrt(jnp.mean(x * x, axis=-1, keepdims=True) + EPS) * w


def _sigmoid(x):
    return 0.5 * jnp.tanh(0.5 * x) + 0.5


def _silu(x):
    return x * _sigmoid(x)


def _dot(a, b):
    return jnp.dot(a, b, preferred_element_type=F32)


def _dot_nt(a, b):
    return lax.dot_general(a, b, (((1,), (1,)), ((), ())), preferred_element_type=F32)


def _dot_tn(a, b):
    return lax.dot_general(a, b, (((0,), (0,)), ((), ())), preferred_element_type=F32)


def _const_spec(shape):
    zeros = (0,) * len(shape)
    return pl.BlockSpec(shape, lambda *_: zeros, pipeline_mode=pl.Buffered(1))


def _dwconv(src_ref, base, rows, w_ref, b_ref, taps, width, emit, col0=0):
    for c in range(col0 // LANES, (col0 + width) // LANES):
        cs = slice(c * LANES, (c + 1) * LANES)
        acc = jnp.broadcast_to(b_ref[:, cs], (rows, LANES))
        for r in range(min(SUBLANES, taps)):
            qs = range((taps - r + SUBLANES - 1) // SUBLANES)
            slab = src_ref[pl.ds(base + r, rows + SUBLANES * (len(qs) - 1)), cs]
            part = w_ref[r:r + 1, cs] * slab[0:rows]
            for q in qs[1:]:
                k = SUBLANES * q + r
                part = part + w_ref[k:k + 1, cs] * slab[SUBLANES * q:SUBLANES * q + rows]
            acc = acc + part
        emit(cs, acc)


def _conf_tail(conv_ref, ln_g_ref, ln_b_ref, wout_ref, bout_ref, ya_ref):
    c = conv_ref[...]
    mu = jnp.mean(c, axis=-1, keepdims=True)
    d = c - mu
    var = jnp.mean(d * d, axis=-1, keepdims=True)
    y = d * lax.rsqrt(var + EPS) * ln_g_ref[...] + ln_b_ref[...]
    y = _silu(y)
    ya_ref[...] = _dot(y.astype(BF16), wout_ref[...]) + bout_ref[...]


def _conf_glu(x, nw_ref, wvg_ref):
    d = x.shape[-1]
    xn = _rms(x, nw_ref[...]).astype(BF16)
    vg = _dot(xn, wvg_ref[...])
    return vg[:, :d] * _sigmoid(vg[:, d:])


def _conf_prompt_kernel(meta_ref, x_ref, nw_ref, wvg_ref, dww_ref, dwb_ref, lng_ref, lnb_ref, wout_ref, bout_ref,
                        ya_ref, nc_ref, afull_ref, conv_ref):
    j = pl.program_id(1)
    d = x_ref.shape[-1]
    rows = x_ref.shape[0]

    @pl.when(j == 0)
    def _():
        afull_ref[0:CONF_PAD - N_META, :] = jnp.zeros((CONF_PAD - N_META, d), F32)
        afull_ref[CONF_PAD - N_META:CONF_PAD, :] = _conf_glu(meta_ref[...], nw_ref, wvg_ref)

    afull_ref[CONF_PAD:CONF_PAD + rows, :] = _conf_glu(x_ref[...], nw_ref, wvg_ref)
    for r0 in range(0, rows, TILE):
        def emit(cs, v, r0=r0):
            conv_ref[r0:r0 + TILE, cs] = v

        _dwconv(afull_ref, CONF_PAD - (CONF_K - 1) + r0, TILE, dww_ref, dwb_ref, CONF_K, d, emit)
    nc_ref[...] = afull_ref[pl.ds(CONF_PAD + rows - (CONF_K - 1), CONF_K - 1), :]
    afull_ref[0:CONF_PAD, :] = afull_ref[rows:rows + CONF_PAD, :]
    _conf_tail(conv_ref, lng_ref, lnb_ref, wout_ref, bout_ref, ya_ref)


def _conf_sample_kernel(x_ref, st_ref, nw_ref, wvg_ref, dww_ref, dwb_ref, lng_ref, lnb_ref, wout_ref, bout_ref,
                        ya_ref, nc_ref, a_ref, cs_ref, conv_ref, *, n_new):
    d = x_ref.shape[-1]
    n_seq = TILE // SAMPLE_ROWS
    a_ref[...] = _conf_glu(x_ref[...], nw_ref, wvg_ref)

    def seq_body(s, carry):
        r0 = pl.multiple_of(s * SAMPLE_ROWS, SAMPLE_ROWS)
        cs_ref[CONF_PAD - (CONF_K - 1):CONF_PAD, :] = st_ref[s]
        cs_ref[CONF_PAD:CONF_PAD + SAMPLE_ROWS, :] = a_ref[pl.ds(r0, SAMPLE_ROWS), :]

        def emit(cs, v):
            conv_ref[pl.ds(r0, SAMPLE_ROWS), cs] = v

        _dwconv(cs_ref, CONF_PAD - (CONF_K - 1), SAMPLE_ROWS, dww_ref, dwb_ref, CONF_K, d, emit)
        nc_ref[s] = cs_ref[pl.ds(CONF_PAD + n_new - (CONF_K - 1), CONF_K - 1), :]
        return carry

    lax.fori_loop(0, n_seq, seq_body, 0)
    _conf_tail(conv_ref, lng_ref, lnb_ref, wout_ref, bout_ref, ya_ref)


def _conf_weights(p):
    d = p["norm_w"].shape[-1]
    return [p["norm_w"], p["w_vg"], p["conf_dw_w"], p["conf_dw_b"], p["conf_ln_g"], p["conf_ln_b"],
            p["conf_out_w"], p["conf_out_b"]], [
        _const_spec((1, d)), _const_spec((d, 2 * d)), _const_spec((CONF_PAD, d)), _const_spec((1, d)),
        _const_spec((1, d)), _const_spec((1, d)), _const_spec((d, d)), _const_spec((1, d))]


def _conf_prompt(meta, x_prompt, p):
    b, l, d = x_prompt.shape
    rows = 2 * TILE if l % (2 * TILE) == 0 else TILE
    weights, wspecs = _conf_weights(p)
    xmap = lambda i, j: (i, j, 0)
    return pl.pallas_call(
        _conf_prompt_kernel,
        grid=(b, l // rows),
        in_specs=[_const_spec(meta.shape), pl.BlockSpec((None, rows, d), xmap)] + wspecs,
        out_specs=[pl.BlockSpec((None, rows, d), xmap),
                   pl.BlockSpec((None, CONF_K - 1, d), lambda i, j: (i, 0, 0))],
        out_shape=[jax.ShapeDtypeStruct((b, l, d), F32), jax.ShapeDtypeStruct((b, CONF_K - 1, d), F32)],
        scratch_shapes=[pltpu.VMEM((CONF_PAD + rows, d), F32), pltpu.VMEM((rows, d), F32)],
        compiler_params=pltpu.CompilerParams(dimension_semantics=("arbitrary", "arbitrary"),
                                             vmem_limit_bytes=VMEM_LIMIT),
        name="conf_prompt",
    )(meta, x_prompt, *weights)


def _conf_sample(x_rows, state, p, n_new):
    rows, d = x_rows.shape
    n_seq = TILE // SAMPLE_ROWS
    weights, wspecs = _conf_weights(p)
    return pl.pallas_call(
        functools.partial(_conf_sample_kernel, n_new=n_new),
        grid=(rows // TILE,),
        in_specs=[pl.BlockSpec((TILE, d), lambda i: (i, 0)),
                  pl.BlockSpec((n_seq, CONF_K - 1, d), lambda i: (i, 0, 0))] + wspecs,
        out_specs=[pl.BlockSpec((TILE, d), lambda i: (i, 0)),
                   pl.BlockSpec((n_seq, CONF_K - 1, d), lambda i: (i, 0, 0))],
        out_shape=[jax.ShapeDtypeStruct((rows, d), F32),
                   jax.ShapeDtypeStruct((rows // SAMPLE_ROWS, CONF_K - 1, d), F32)],
        scratch_shapes=[pltpu.VMEM((TILE, d), F32), pltpu.VMEM((CONF_PAD + SAMPLE_ROWS, d), F32),
                        pltpu.VMEM((TILE, d), F32)],
        compiler_params=pltpu.CompilerParams(dimension_semantics=("arbitrary",), vmem_limit_bytes=VMEM_LIMIT),
        name="conf_sample",
    )(x_rows, state, *weights)


def _xbc_col_pairs(d_inner):
    pairs = []
    for g in range(N_GROUPS):
        c0 = g * XBC_GROUP_W
        pairs.append((slice(c0, c0 + GROUP_W), slice(g * GROUP_W, (g + 1) * GROUP_W)))
        for k in range(2):
            ours = c0 + GROUP_W + k * N_STATE
            theirs = d_inner + (k * N_GROUPS + g) * N_STATE
            pairs.append((slice(ours, ours + N_STATE), slice(theirs, theirs + N_STATE)))
    return pairs


def _project_group(g, xn, wzx_ref, z_ref, pre_ref, row0):
    zx = _dot(xn, wzx_ref[:, g * ZX_GROUP_W:(g + 1) * ZX_GROUP_W])
    z_ref[:, g * GROUP_W:(g + 1) * GROUP_W] = zx[:, :GROUP_W]
    pre_ref[row0:row0 + xn.shape[0], g * XBC_GROUP_W:(g + 1) * XBC_GROUP_W] = zx[:, GROUP_W:]


def _seg_cumsum(x, row, seg_len, reverse):
    pos = row % seg_len
    n = x.shape[0]
    step = 1
    while step < seg_len:
        if reverse:
            x = x + jnp.where(pos < seg_len - step, pltpu.roll(x, n - step, axis=0), 0.0)
        else:
            x = x + jnp.where(pos >= step, pltpu.roll(x, step, axis=0), 0.0)
        step *= 2
    return x


def _expand_heads(m, g, lane):
    rows = m.shape[0]
    cols = [jnp.broadcast_to(m[:, HEADS_PER_GROUP * g + i:HEADS_PER_GROUP * g + i + 1], (rows, LANES))
            for i in range(HEADS_PER_GROUP)]
    lo = jnp.where(lane < HEAD_DIM, cols[0], cols[1])
    hi = jnp.where(lane < HEAD_DIM, cols[2], cols[3])
    return jnp.concatenate([lo, hi], axis=1)


def _ssd_prelude(dtr, valid, seg_len, dtb_ref, alog_ref):
    row = lax.broadcasted_iota(jnp.int32, (TILE, TILE), 0)
    col = lax.broadcasted_iota(jnp.int32, (TILE, TILE), 1)
    same_seq = (col // seg_len) == (row // seg_len)
    causal = (col <= row) & same_seq
    xdt = dtr + dtb_ref[...]
    e = jnp.exp(-jnp.abs(xdt))
    u = 1.0 + e
    dt = (jnp.maximum(xdt, 0.0) + jnp.where(u == 1.0, e, jnp.log(u) * e / (u - 1.0))) * valid
    da = dt * (-jnp.exp(alog_ref[...]))
    d1 = da.astype(BF16)
    r1 = da - d1.astype(F32)
    d2 = r1.astype(BF16)
    d3 = (r1 - d2.astype(F32)).astype(BF16)
    lower = jnp.where(causal, 1.0, 0.0).astype(BF16)
    upper = jnp.where((col > row) & same_seq, 1.0, 0.0).astype(BF16)
    cum = _dot(lower, d1) + _dot(lower, d2) + _dot(lower, d3)
    rest = _dot(upper, d1) + _dot(upper, d2) + _dot(upper, d3)
    return {
        "causal": causal,
        "cum": cum,
        "cum_t": cum.T,
        "dt_t": dt.T,
        "ecum": jnp.exp(cum),
        "tail": jnp.exp(rest) * dt,
    }


_PRE_KEYS = ("cum", "cum_t", "dt_t", "ecum", "tail")


def _ssd_group(g, pre, z_ref, xc_ref, seg_len, n_seq, h_in, h_out, dskip_ref, normw_ref, after_first_dot=None):
    lane = lax.broadcasted_iota(jnp.int32, (TILE, LANES), 1)
    lane_g = lax.broadcasted_iota(jnp.int32, (TILE, GROUP_W), 1)
    cum, cum_t, dt_t = pre["cum"], pre["cum_t"], pre["dt_t"]
    c0 = g * XBC_GROUP_W
    gs = slice(g * GROUP_W, (g + 1) * GROUP_W)
    xs = xc_ref[:, c0:c0 + GROUP_W]
    zg = z_ref[:, gs]
    bm16 = xc_ref[:, c0 + GROUP_W:c0 + GROUP_W + N_STATE].astype(BF16)
    cm16 = xc_ref[:, c0 + GROUP_W + N_STATE:c0 + XBC_GROUP_W].astype(BF16)
    cb = _dot_nt(cm16, bm16)
    extra = after_first_dot() if after_first_dot is not None else None
    seqs = [(s, slice(s * seg_len, (s + 1) * seg_len), (s + 1) * seg_len - 1) for s in range(n_seq)]
    ystate = [_dot_nt(cm16[rs], h_in(s, g).astype(BF16)) for s, rs, _ in seqs]
    if n_seq * seg_len < TILE:
        ystate.append(jnp.zeros((TILE - n_seq * seg_len, GROUP_W), F32))
    ys = ystate[0] if len(ystate) == 1 else jnp.concatenate(ystate, axis=0)
    ws, xms = [], []
    for i in range(HEADS_PER_GROUP):
        h = HEADS_PER_GROUP * g + i
        seg = cum[:, h:h + 1] - cum_t[h:h + 1, :]
        decay = jnp.exp(jnp.where(pre["causal"], seg, -jnp.inf))
        ws.append((cb * decay * dt_t[h:h + 1, :]).astype(BF16))
        xms.append(jnp.where((lane_g // HEAD_DIM) == i, xs, 0.0).astype(BF16))
    yg = _dot(jnp.concatenate(ws, axis=1), jnp.concatenate(xms, axis=0))
    yg = yg + ys * _expand_heads(pre["ecum"], g, lane)
    xw16 = (xs * _expand_heads(pre["tail"], g, lane)).astype(BF16)

    def update_state():
        for s, rs, last in seqs:
            upd = _dot_tn(xw16[rs], bm16[rs])
            dec = jnp.concatenate(
                [jnp.broadcast_to(jnp.exp(cum_t[HEADS_PER_GROUP * g + i:HEADS_PER_GROUP * g + i + 1, last:last + 1]),
                                  (HEAD_DIM, N_STATE)) for i in range(HEADS_PER_GROUP)], axis=0)
            h_out(s, g, h_in(s, g) * dec + upd)

    v = (yg + dskip_ref[:, gs] * xs) * _silu(zg)
    v = v * lax.rsqrt(jnp.mean(v * v, axis=-1, keepdims=True) + EPS) * normw_ref[:, gs]
    return v.astype(BF16), update_state, extra


def _out_proj_group(g, v16, wout_ref):
    return _dot(v16, wout_ref[g * GROUP_W:(g + 1) * GROUP_W, :])


def _ssm_prompt_kernel(x_ref, hist_ref, h0_ref, nw_ref, wzx_ref, wdt_ref, cw_ref, cb_ref, dtb_ref, alog_ref,
                       dskip_ref, normw_ref, wout_ref, yb_ref, ns_ref, hs_ref, z_ref, cfull_ref, xc_ref, pre_ref, *,
                       n_inert):
    s = pl.program_id(1)
    n_tiles = pl.num_programs(1) - 1
    nq, _, d = x_ref.shape
    seqs = range(nq)

    @pl.when(s == 0)
    def _():
        for q in seqs:
            cfull_ref[q, 0:SCONV_PAD, :] = hist_ref[...]
            hs_ref[q] = h0_ref[...]
        z_ref[...] = jnp.zeros(z_ref.shape, F32)
        xc_ref[...] = jnp.zeros(xc_ref.shape, F32)
        pre_ref[...] = jnp.zeros(pre_ref.shape, F32)

    xn = _rms(x_ref[...].reshape(nq * TILE, d), nw_ref[...]).astype(BF16)
    row = lax.broadcasted_iota(jnp.int32, (TILE, TILE), 0)
    col = lax.broadcasted_iota(jnp.int32, (TILE, TILE), 1)
    pres = []
    for q in seqs:
        pre = {k: pre_ref[q, i] for i, k in enumerate(_PRE_KEYS)}
        pre["causal"] = col <= row
        pres.append(pre)
    rowv = lax.broadcasted_iota(jnp.int32, (TILE, 1), 0)
    valid = jnp.where((s == 0) & (rowv < n_inert), 0.0, 1.0)
    dtr = _dot(xn, wdt_ref[...])

    def state_access(q):
        def h_in(_, g):
            return hs_ref[q, HEADS_PER_GROUP * g:HEADS_PER_GROUP * (g + 1)].reshape(GROUP_W, N_STATE)

        def h_out(_, g, v):
            hs_ref[q, HEADS_PER_GROUP * g:HEADS_PER_GROUP * (g + 1)] = v.reshape(HEADS_PER_GROUP, HEAD_DIM, N_STATE)

        return h_in, h_out

    acc = [jnp.zeros((TILE, d), F32) for _ in seqs]
    prev = None
    for g in range(N_GROUPS):
        projected = {}
        for q in seqs:
            def independent_dots(g=g, q=q, prev=prev, projected=projected):
                if q == 0:
                    projected["zx"] = _dot(xn, wzx_ref[:, g * ZX_GROUP_W:(g + 1) * ZX_GROUP_W])
                zx = projected["zx"][q * TILE:(q + 1) * TILE]
                z_ref[q, :, g * GROUP_W:(g + 1) * GROUP_W] = zx[:, :GROUP_W]
                cfull_ref[q, SCONV_PAD:SCONV_PAD + TILE, g * XBC_GROUP_W:(g + 1) * XBC_GROUP_W] = zx[:, GROUP_W:]
                if prev is None:
                    return None
                prev_v16, prev_update, prev_q, prev_g = prev
                term = _out_proj_group(prev_g, prev_v16, wout_ref)
                prev_update()
                return prev_q, term

            h_in, h_out = state_access(q)
            v16, update, extra = _ssd_group(g, pres[q], z_ref.at[q], xc_ref.at[q], TILE, 1, h_in, h_out, dskip_ref,
                                            normw_ref, independent_dots)
            if extra is not None:
                acc[extra[0]] = acc[extra[0]] + extra[1]

            def emit(cs, v, q=q):
                xc_ref[q, :, cs] = _silu(v)

            _dwconv(cfull_ref.at[q], SCONV_PAD - (SSM_K - 1), TILE, cw_ref, cb_ref, SSM_K, XBC_GROUP_W, emit,
                    col0=g * XBC_GROUP_W)
            prev = (v16, update, q, g)
    acc[prev[2]] = acc[prev[2]] + _out_proj_group(prev[3], prev[0], wout_ref)
    prev[1]()
    for q in seqs:
        yb_ref[q] = acc[q]
        pre_next = _ssd_prelude(dtr[q * TILE:(q + 1) * TILE], valid, TILE, dtb_ref, alog_ref)
        for i, k in enumerate(_PRE_KEYS):
            pre_ref[q, i] = pre_next[k]

    @pl.when(s == n_tiles - 1)
    def _():
        for q in seqs:
            ns_ref[q] = cfull_ref[q, pl.ds(SCONV_PAD + TILE - (SSM_K - 1), SSM_K - 1), :]

    for q in seqs:
        cfull_ref[q, 0:SCONV_PAD, :] = cfull_ref[q, TILE:TILE + SCONV_PAD, :]


def _ssm_sample_proj_kernel(x_ref, st_ref, nw_ref, wzx_ref, wdt_ref, cw_ref, cb_ref,
                            z_ref, xc_ref, dtr_ref, ns_ref, xbc_ref, cs_ref, *, n_new):
    conv_dim = xc_ref.shape[-1]
    d_inner = z_ref.shape[-1]
    n_seq = TILE // SAMPLE_ROWS
    xn = _rms(x_ref[...], nw_ref[...]).astype(BF16)
    for g in range(N_GROUPS):
        _project_group(g, xn, wzx_ref, z_ref, xbc_ref, 0)
    dtr_ref[...] = _dot(xn, wdt_ref[...])
    col_pairs = _xbc_col_pairs(d_inner)

    def seq_body(s, carry):
        r0 = pl.multiple_of(s * SAMPLE_ROWS, SAMPLE_ROWS)
        hist = st_ref[s]
        for ours, theirs in col_pairs:
            cs_ref[SCONV_PAD - (SSM_K - 1):SCONV_PAD, ours] = hist[:, theirs]
        cs_ref[SCONV_PAD:SCONV_PAD + SAMPLE_ROWS, :] = xbc_ref[pl.ds(r0, SAMPLE_ROWS), :]

        def emit(cs, v):
            xc_ref[pl.ds(r0, SAMPLE_ROWS), cs] = _silu(v)

        _dwconv(cs_ref, SCONV_PAD - (SSM_K - 1), SAMPLE_ROWS, cw_ref, cb_ref, SSM_K, conv_dim, emit)
        for ours, theirs in col_pairs:
            ns_ref[s, :, theirs] = cs_ref[pl.ds(SCONV_PAD + n_new - (SSM_K - 1), SSM_K - 1), ours]
        return carry

    lax.fori_loop(0, n_seq, seq_body, 0)


def _ssm_sample_ssd_kernel(z_in, xc_in, dtr_in, hin_ref, dtb_ref, alog_ref, dskip_ref, normw_ref, wout_ref,
                           yb_ref, hout_ref, z_ref, xc_ref, dtr_ref, *, n_new, n_seq):
    rows = n_seq * SAMPLE_ROWS
    for src, dst in ((z_in, z_ref), (xc_in, xc_ref), (dtr_in, dtr_ref)):
        dst[0:rows, :] = src[...]
        dst[rows:TILE, :] = jnp.zeros((TILE - rows, dst.shape[-1]), F32)
    rowv = lax.broadcasted_iota(jnp.int32, (TILE, 1), 0)
    valid = jnp.where((rowv < rows) & ((rowv % SAMPLE_ROWS) < n_new), 1.0, 0.0)

    def h_in(s, g):
        return hin_ref[s, HEADS_PER_GROUP * g:HEADS_PER_GROUP * (g + 1)].reshape(GROUP_W, N_STATE)

    def h_out(s, g, v):
        hout_ref[s, HEADS_PER_GROUP * g:HEADS_PER_GROUP * (g + 1)] = v.reshape(HEADS_PER_GROUP, HEAD_DIM, N_STATE)

    pre = _ssd_prelude(dtr_ref[...], valid, SAMPLE_ROWS, dtb_ref, alog_ref)
    acc = jnp.zeros((TILE, yb_ref.shape[-1]), F32)
    for g in range(N_GROUPS):
        v16, update, _ = _ssd_group(g, pre, z_ref, xc_ref, SAMPLE_ROWS, n_seq, h_in, h_out, dskip_ref, normw_ref)
        acc = acc + _out_proj_group(g, v16, wout_ref)
        update()
    yb_ref[...] = acc[0:rows, :]


def _ssm_prompt(x, hist, h0, p, n_inert):
    b, l, d = x.shape
    nb = l // TILE
    nq = 2 if b % 2 == 0 else 1
    d_inner = p["ssm_norm_w"].shape[-1]
    conv_dim = p["ssm_conv_b"].shape[-1]
    n_heads = d_inner // HEAD_DIM
    xmap = lambda i, j: (i, jnp.minimum(j, nb - 1), 0)
    ymap = lambda i, j: (i, jnp.maximum(j - 1, 0), 0)
    weights = [hist, h0, p["norm_w"], p["w_zx"], p["w_dt"], p["ssm_conv_w"], p["ssm_conv_b"], p["dt_bias"],
               p["a_log"], p["d_skip"], p["ssm_norm_w"], p["ssm_out_w"]]
    wspecs = [_const_spec(w.shape) for w in weights]
    return pl.pallas_call(
        functools.partial(_ssm_prompt_kernel, n_inert=n_inert),
        grid=(b // nq, nb + 1),
        in_specs=[pl.BlockSpec((nq, TILE, d), xmap)] + wspecs,
        out_specs=[pl.BlockSpec((nq, TILE, d), ymap),
                   pl.BlockSpec((nq, SSM_K - 1, conv_dim), lambda i, j: (i, 0, 0)),
                   pl.BlockSpec((nq, n_heads, HEAD_DIM, N_STATE), lambda i, j: (i, 0, 0, 0))],
        out_shape=[jax.ShapeDtypeStruct((b, l, d), F32), jax.ShapeDtypeStruct((b, SSM_K - 1, conv_dim), F32),
                   jax.ShapeDtypeStruct((b, n_heads, HEAD_DIM, N_STATE), F32)],
        scratch_shapes=[pltpu.VMEM((nq, TILE, d_inner), F32), pltpu.VMEM((nq, SCONV_PAD + TILE, conv_dim), F32),
                        pltpu.VMEM((nq, TILE, conv_dim), F32), pltpu.VMEM((nq, len(_PRE_KEYS), TILE, LANES), F32)],
        compiler_params=pltpu.CompilerParams(dimension_semantics=("arbitrary", "arbitrary"),
                                             vmem_limit_bytes=VMEM_LIMIT),
        name="ssm_prompt",
    )(x, *weights)


def _ssm_sample(x_rows, sconv, h0, p, n_new):
    rows, d = x_rows.shape
    d_inner = p["ssm_norm_w"].shape[-1]
    conv_dim = p["ssm_conv_b"].shape[-1]
    n_heads = d_inner // HEAD_DIM
    n_seq_proj = TILE // SAMPLE_ROWS
    weights = [p["norm_w"], p["w_zx"], p["w_dt"], p["ssm_conv_w"], p["ssm_conv_b"]]
    z, xc, dtr, new_sconv = pl.pallas_call(
        functools.partial(_ssm_sample_proj_kernel, n_new=n_new),
        grid=(rows // TILE,),
        in_specs=[pl.BlockSpec((TILE, d), lambda i: (i, 0)),
                  pl.BlockSpec((n_seq_proj, SSM_K - 1, conv_dim), lambda i: (i, 0, 0))]
        + [_const_spec(w.shape) for w in weights],
        out_specs=[pl.BlockSpec((TILE, d_inner), lambda i: (i, 0)), pl.BlockSpec((TILE, conv_dim), lambda i: (i, 0)),
                   pl.BlockSpec((TILE, LANES), lambda i: (i, 0)),
                   pl.BlockSpec((n_seq_proj, SSM_K - 1, conv_dim), lambda i: (i, 0, 0))],
        out_shape=[jax.ShapeDtypeStruct((rows, d_inner), F32), jax.ShapeDtypeStruct((rows, conv_dim), F32),
                   jax.ShapeDtypeStruct((rows, LANES), F32),
                   jax.ShapeDtypeStruct((rows // SAMPLE_ROWS, SSM_K - 1, conv_dim), F32)],
        scratch_shapes=[pltpu.VMEM((TILE, conv_dim), F32), pltpu.VMEM((SCONV_PAD + SAMPLE_ROWS, conv_dim), F32)],
        compiler_params=pltpu.CompilerParams(dimension_semantics=("arbitrary",), vmem_limit_bytes=VMEM_LIMIT),
        name="ssm_sample_proj",
    )(x_rows, sconv, *weights)

    n_seq = 8 if rows % (8 * SAMPLE_ROWS) == 0 else 4
    r = n_seq * SAMPLE_ROWS
    weights = [p["dt_bias"], p["a_log"], p["d_skip"], p["ssm_norm_w"], p["ssm_out_w"]]
    yb, h_new = pl.pallas_call(
        functools.partial(_ssm_sample_ssd_kernel, n_new=n_new, n_seq=n_seq),
        grid=(rows // r,),
        in_specs=[pl.BlockSpec((r, d_inner), lambda i: (i, 0)), pl.BlockSpec((r, conv_dim), lambda i: (i, 0)),
                  pl.BlockSpec((r, LANES), lambda i: (i, 0)),
                  pl.BlockSpec((n_seq, n_heads, HEAD_DIM, N_STATE), lambda i: (i, 0, 0, 0))]
        + [_const_spec(w.shape) for w in weights],
        out_specs=[pl.BlockSpec((r, d), lambda i: (i, 0)),
                   pl.BlockSpec((n_seq, n_heads, HEAD_DIM, N_STATE), lambda i: (i, 0, 0, 0))],
        out_shape=[jax.ShapeDtypeStruct((rows, d), F32), jax.ShapeDtypeStruct(h0.shape, F32)],
        scratch_shapes=[pltpu.VMEM((TILE, d_inner), F32), pltpu.VMEM((TILE, conv_dim), F32),
                        pltpu.VMEM((TILE, LANES), F32)],
        compiler_params=pltpu.CompilerParams(dimension_semantics=("arbitrary",), vmem_limit_bytes=VMEM_LIMIT),
        name="ssm_sample_ssd",
    )(z, xc, dtr, h0, *weights)
    return yb, new_sconv, h_new


def _merge_kernel(x_ref, ya_ref, yb_ref, nw_ref, wg_ref, gb_ref, wo_ref, nfw_ref, rwh_ref, rwl_ref, rb_ref,
                  x1_ref, xn2_ref, comb_ref):
    d = x_ref.shape[-1]
    half = x_ref.shape[0] // 2
    parts = [slice(0, half), slice(half, 2 * half)]
    xs = [x_ref[rs, :] for rs in parts]
    xns = [_rms(x, nw_ref[...]).astype(BF16) for x in xs]
    glog = [_dot(xn, wg_ref[...]) for xn in xns]
    merged = []
    for rs, gl_ in zip(parts, glog):
        gates = _sigmoid(gl_ + gb_ref[...])
        merged.append((gates[:, :d] * ya_ref[rs, :] + gates[:, d:] * yb_ref[rs, :]).astype(BF16))
    outs = [_dot(m, wo_ref[...]) for m in merged]
    logit_parts = []
    for rs, x, o in zip(parts, xs, outs):
        x1 = x + o
        x1_ref[rs, :] = x1
        xn2 = _rms(x1, nfw_ref[...])
        x_hi = xn2.astype(BF16)
        xn2_ref[rs, :] = x_hi
        x_lo = (xn2 - x_hi.astype(F32)).astype(BF16)
        logit_parts.append(_dot(x_hi, rwh_ref[...]) + _dot(x_lo, rwh_ref[...]) + _dot(x_hi, rwl_ref[...]))

    logits = jnp.concatenate(logit_parts, axis=0) + rb_ref[...]
    rows = logits.shape[0]
    lane = lax.broadcasted_iota(jnp.int32, (rows, LANES), 1)
    neg = -jnp.inf
    is_g = (lane >= N_EXPERTS) & (lane < N_EXPERTS + N_EXPERT_GROUPS)
    gl = jnp.where(is_g, logits, neg)
    gmax = jnp.max(gl, axis=-1, keepdims=True)
    gsel = jnp.min(jnp.where(gl == gmax, lane, LANES), axis=-1, keepdims=True) - N_EXPERTS
    gprob = 1.0 / jnp.sum(jnp.exp(gl - gmax), axis=-1, keepdims=True)
    el = jnp.where((lane < N_EXPERTS) & ((lane // EXPERTS_PER_GROUP) == gsel), logits, neg)
    m1 = jnp.max(el, axis=-1, keepdims=True)
    i1 = jnp.min(jnp.where(el == m1, lane, LANES), axis=-1, keepdims=True)
    el2 = jnp.where(lane == i1, neg, el)
    m2 = jnp.max(el2, axis=-1, keepdims=True)
    i2 = jnp.min(jnp.where(el2 == m2, lane, LANES), axis=-1, keepdims=True)
    e2 = jnp.exp(m2 - m1)
    den = 1.0 + e2
    comb = jnp.where(lane == i1, (1.0 / den) * gprob, jnp.where(lane == i2, (e2 / den) * gprob, 0.0))
    comb_ref[...] = jnp.where(lane == GSEL_LANE, gsel.astype(F32), comb)


def _merge(x, ya, yb, p, tm):
    rows, d = x.shape
    weights = [p["norm_w"], p["w_gate"], p["gate_b"], p["w_o"], p["norm_ffn_w"], p["router_w_hi"],
               p["router_w_lo"], p["router_b"]]
    row_spec = pl.BlockSpec((tm, d), lambda i: (i, 0))
    return pl.pallas_call(
        _merge_kernel,
        grid=(rows // tm,),
        in_specs=[row_spec, row_spec, row_spec] + [_const_spec(w.shape) for w in weights],
        out_specs=[row_spec, row_spec, pl.BlockSpec((tm, LANES), lambda i: (i, 0))],
        out_shape=[jax.ShapeDtypeStruct((rows, d), F32), jax.ShapeDtypeStruct((rows, d), BF16),
                   jax.ShapeDtypeStruct((rows, LANES), F32)],
        compiler_params=pltpu.CompilerParams(dimension_semantics=("arbitrary",), vmem_limit_bytes=VMEM_LIMIT),
        name="merge_router",
    )(x, ya, yb, *weights)


def _moe_kernel(xn_ref, comb_ref, x1_ref, w1_ref, w3_ref, w2_ref, nw_ref, y_ref, rt_ref, *, chunk):
    g = pl.program_id(1)
    tm = xn_ref.shape[0]
    comb = comb_ref[...]
    lane = lax.broadcasted_iota(jnp.int32, (tm, LANES), 1)
    gsel = comb[:, GSEL_LANE:GSEL_LANE + 1]

    @pl.when(g == 0)
    def _():
        row = lax.broadcasted_iota(jnp.int32, (tm, LANES), 0)
        onehot = jnp.where(lane.astype(F32) == gsel, 1.0, 0.0)
        incl = _seg_cumsum(onehot, row, tm, reverse=False)
        rank = jnp.sum(onehot * (incl - onehot), axis=-1, keepdims=True)
        packed = jnp.where(lane == 0, rank, jnp.where(lane == 1, gsel, 0.0))
        rt_ref[0:SUBLANES, :] = packed.T[0:SUBLANES, :]
        rt_ref[SUBLANES:2 * SUBLANES, 0:LANES] = jnp.broadcast_to(incl[tm - 1:tm, :], (SUBLANES, LANES))
        y_ref[...] = jnp.zeros(y_ref.shape, F32)

    gf = g.astype(F32)
    lane_row = lax.broadcasted_iota(jnp.int32, (1, LANES), 1)
    count = jnp.sum(jnp.where(lane_row == g, rt_ref[SUBLANES:SUBLANES + 1, 0:LANES], 0.0), axis=-1, keepdims=True)
    n_chunks = (count[0, 0].astype(jnp.int32) + chunk - 1) // chunk
    w_hi = jnp.where(lane < N_EXPERTS, comb, 0.0).astype(BF16)
    w_lo = (jnp.where(lane < N_EXPERTS, comb, 0.0) - w_hi.astype(F32)).astype(BF16)
    rowi = lax.broadcasted_iota(jnp.int32, (chunk, tm), 0).astype(F32)
    lane_c = lax.broadcasted_iota(jnp.int32, (chunk, LANES), 1)

    def chunk_body(c, carry):
        base = (c * chunk).astype(F32)
        sel = jnp.where((rt_ref[1:2, :] == gf) & ((rt_ref[0:1, :] - base) == rowi), 1.0, 0.0).astype(BF16)
        xc = _dot(sel, xn_ref[...]).astype(BF16)
        wc = _dot(sel, w_hi) + _dot(sel, w_lo)
        acts = []
        for e in range(EXPERTS_PER_GROUP):
            ce = jnp.sum(jnp.where(lane_c == g * EXPERTS_PER_GROUP + e, wc, 0.0), axis=-1, keepdims=True)
            acts.append((_silu(_dot(xc, w1_ref[e])) * _dot(xc, w3_ref[e]) * ce).astype(BF16))
        y = _dot(jnp.concatenate(acts, axis=1), w2_ref[...])
        y_ref[...] += _dot_tn(sel, y.astype(BF16))
        return carry

    lax.fori_loop(0, n_chunks, chunk_body, 0)

    @pl.when(g == pl.num_programs(1) - 1)
    def _():
        y_ref[...] = _rms(x1_ref[...] + y_ref[...], nw_ref[...])


def _moe(xn2, comb, x1, p, tm):
    rows, d = x1.shape
    n_e, _, d_e = p["exp_w1"].shape
    n_g = n_e // EXPERTS_PER_GROUP
    w2 = p["exp_w2"].reshape(n_g, EXPERTS_PER_GROUP * d_e, d)
    pack = 2 * SUBLANES
    chunk = -(-(tm * 9 // (8 * n_g)) // pack) * pack
    row = lambda i, g: (i, 0)
    return pl.pallas_call(
        functools.partial(_moe_kernel, chunk=chunk),
        grid=(rows // tm, n_g),
        in_specs=[pl.BlockSpec((tm, d), row), pl.BlockSpec((tm, LANES), row),
                  pl.BlockSpec((tm, d), row, pipeline_mode=pl.Buffered(1)),
                  pl.BlockSpec((EXPERTS_PER_GROUP, d, d_e), lambda i, g: (g, 0, 0)),
                  pl.BlockSpec((EXPERTS_PER_GROUP, d, d_e), lambda i, g: (g, 0, 0)),
                  pl.BlockSpec((None, EXPERTS_PER_GROUP * d_e, d), lambda i, g: (g, 0, 0)),
                  _const_spec((1, d))],
        out_specs=pl.BlockSpec((tm, d), row),
        out_shape=jax.ShapeDtypeStruct((rows, d), F32),
        scratch_shapes=[pltpu.VMEM((2 * SUBLANES, tm), F32)],
        compiler_params=pltpu.CompilerParams(dimension_semantics=("arbitrary", "arbitrary"),
                                             vmem_limit_bytes=VMEM_LIMIT),
        name="moe",
    )(xn2, comb, x1, p["exp_w1"], p["exp_w3"], w2, p["norm_final_w"])


def _group_xbc_cols(v, d_inner):
    lead = v.shape[:-1]
    xs = v[..., :d_inner].reshape(lead + (N_GROUPS, GROUP_W))
    bm = v[..., d_inner:d_inner + N_GROUPS * N_STATE].reshape(lead + (N_GROUPS, N_STATE))
    cm = v[..., d_inner + N_GROUPS * N_STATE:].reshape(lead + (N_GROUPS, N_STATE))
    return jnp.concatenate([xs, bm, cm], axis=-1).reshape(lead + (-1,))


def _ungroup_xbc_cols(v, d_inner):
    lead = v.shape[:-1]
    v = v.reshape(lead + (N_GROUPS, XBC_GROUP_W))
    parts = [v[..., :GROUP_W], v[..., GROUP_W:GROUP_W + N_STATE], v[..., GROUP_W + N_STATE:]]
    return jnp.concatenate([q.reshape(lead + (-1,)) for q in parts], axis=-1)


def _prep_params(norm_mix_w, w_in, conf_dw_w, conf_dw_b, conf_ln_g, conf_ln_b, conf_out_w, conf_out_b, ssm_conv_w,
                 ssm_conv_b, dt_bias, a_log, d_skip, ssm_norm_w, ssm_out_w, gate_b, w_o, norm_ffn_w,
                 router_group_w, router_group_b, router_expert_w, router_expert_b, exp_w1, exp_w3, exp_w2,
                 norm_final_w):
    d = norm_mix_w.shape[-1]
    d_inner = ssm_norm_w.shape[-1]
    conv_dim = ssm_conv_b.shape[-1]
    n_heads = dt_bias.shape[-1]
    s1, s2 = d, 2 * d
    s3 = s2 + d_inner
    s4 = s3 + conv_dim
    s5 = s4 + n_heads
    w = w_in[0]
    row = lambda v: v.reshape(1, -1).astype(F32)
    pad_lanes = lambda v: jnp.pad(v, ((0, 0), (0, LANES - v.shape[-1])))
    router_w = pad_lanes(jnp.concatenate([router_expert_w[0], router_group_w[0]], axis=1))
    router_w_hi = router_w.astype(BF16)
    return {
        "norm_w": row(norm_mix_w[0]),
        "w_vg": w[:, :s2].astype(BF16),
        "w_zx": jnp.concatenate([w[:, s2:s3].reshape(d, N_GROUPS, GROUP_W),
                                 _group_xbc_cols(w[:, s3:s4], d_inner).reshape(d, N_GROUPS, XBC_GROUP_W)],
                                axis=-1).reshape(d, -1).astype(BF16),
        "w_dt": pad_lanes(w[:, s4:s5]).astype(BF16),
        "w_gate": w[:, s5:].astype(BF16),
        "conf_dw_w": jnp.pad(conf_dw_w[0], ((0, CONF_PAD - CONF_K), (0, 0))),
        "conf_dw_b": row(conf_dw_b[0]),
        "conf_ln_g": row(conf_ln_g[0]),
        "conf_ln_b": row(conf_ln_b[0]),
        "conf_out_w": conf_out_w[0].astype(BF16),
        "conf_out_b": row(conf_out_b[0]),
        "ssm_conv_w": jnp.pad(_group_xbc_cols(ssm_conv_w[0], d_inner), ((0, SUBLANES - SSM_K), (0, 0))),
        "ssm_conv_b": row(_group_xbc_cols(ssm_conv_b[0], d_inner)),
        "dt_bias": pad_lanes(row(dt_bias[0])),
        "a_log": pad_lanes(row(a_log[0])),
        "d_skip": row(jnp.repeat(d_skip[0], HEAD_DIM)),
        "ssm_norm_w": row(ssm_norm_w[0]),
        "ssm_out_w": ssm_out_w[0].astype(BF16),
        "gate_b": row(gate_b[0]),
        "w_o": w_o[0].astype(BF16),
        "norm_ffn_w": row(norm_ffn_w[0]),
        "router_w_hi": router_w_hi,
        "router_w_lo": (router_w - router_w_hi.astype(F32)).astype(BF16),
        "router_b": pad_lanes(row(jnp.concatenate([router_expert_b[0], router_group_b[0]]))),
        "exp_w1": exp_w1[0].astype(BF16),
        "exp_w3": exp_w3[0].astype(BF16),
        "exp_w2": exp_w2[0].astype(BF16),
        "norm_final_w": row(norm_final_w),
    }


def kernel(x_prompt, x_sample, state_conf_conv, state_ssm_conv, state_ssm, meta_tokens, norm_mix_w, w_in, conf_dw_w, conf_dw_b, conf_ln_g, conf_ln_b, conf_out_w, conf_out_b, ssm_conv_w, ssm_conv_b, dt_bias, a_log, d_skip, ssm_norm_w, ssm_out_w, gate_b, w_o, norm_ffn_w, router_group_w, router_group_b, router_expert_w, router_expert_b, exp_w1, exp_w3, exp_w2, norm_final_w):
    assert norm_mix_w.shape[0] == 1, "single-layer trunk"
    b, l, d = x_prompt.shape
    nb, n_new, _ = x_sample.shape
    assert l % TILE == 0 and n_new <= SAMPLE_ROWS and (nb * SAMPLE_ROWS) % TILE == 0
    p = _prep_params(norm_mix_w, w_in, conf_dw_w, conf_dw_b, conf_ln_g, conf_ln_b, conf_out_w, conf_out_b,
                     ssm_conv_w, ssm_conv_b, dt_bias, a_log, d_skip, ssm_norm_w, ssm_out_w, gate_b, w_o,
                     norm_ffn_w, router_group_w, router_group_b, router_expert_w, router_expert_b, exp_w1, exp_w3,
                     exp_w2, norm_final_w)

    assert meta_tokens.shape[0] == N_META
    ya_p, new_conf_p = _conf_prompt(meta_tokens.astype(F32), x_prompt, p)
    d_inner = ssm_norm_w.shape[-1]
    conv_dim = ssm_conv_b.shape[-1]
    head = jnp.concatenate([jnp.zeros((TILE - N_META, d), F32), meta_tokens.astype(F32)], axis=0)
    _, meta_sconv, meta_ssm = _ssm_prompt(head[None], jnp.zeros((SCONV_PAD, conv_dim), F32),
                                          jnp.zeros(state_ssm.shape[2:], F32), p, TILE - N_META)
    meta_hist = jnp.pad(meta_sconv[0], ((SCONV_PAD - (SSM_K - 1), 0), (0, 0)))
    yb_p, new_sconv_p, new_ssm_p = _ssm_prompt(x_prompt, meta_hist, meta_ssm[0], p, 0)

    xs_rows = jnp.pad(x_sample, ((0, 0), (0, SAMPLE_ROWS - n_new), (0, 0))).reshape(nb * SAMPLE_ROWS, d)
    ya_s, new_conf_s = _conf_sample(xs_rows, state_conf_conv[0], p, n_new)
    yb_s, new_sconv_s, new_ssm_s = _ssm_sample(xs_rows, state_ssm_conv[0], state_ssm[0], p, n_new)
    unpad = lambda v: v.reshape(nb, SAMPLE_ROWS, d)[:, :n_new].reshape(nb * n_new, d)

    outs = []
    for x, ya, yb in ((x_prompt.reshape(b * l, d), ya_p.reshape(b * l, d), yb_p.reshape(b * l, d)),
                      (x_sample.reshape(nb * n_new, d), unpad(ya_s), unpad(yb_s))):
        tm = next(t for t in (1024, 512, TILE) if x.shape[0] % t == 0)
        x1, xn2, comb = _merge(x, ya, yb, p, min(tm, 512))
        outs.append(_moe(xn2, comb, x1, p, tm))
    y_prompt = outs[0].reshape(b, l, d)
    y_sample = outs[1].reshape(nb, n_new, d)
    return (y_prompt, y_sample, new_conf_p[None], _ungroup_xbc_cols(new_sconv_p, d_inner)[None], new_ssm_p[None],
            new_conf_s[None], new_sconv_s[None], new_ssm_s[None])
```

```python
import functools

import jax
import jax.numpy as jnp
from jax import lax
from jax.experimental import pallas as pl
from jax.experimental.pallas import tpu as pltpu

F32 = jnp.float32
BF16 = jnp.bfloat16
EPS = 1e-6

LANES = 128
SUBLANES = 8
TILE = 128
VMEM_LIMIT = 56 * 1024 * 1024

N_META = 16
CONF_K = 31
SSM_K = 4
HEAD_DIM = 64
N_STATE = 128
N_GROUPS = 8
HEADS_PER_GROUP = 4
GROUP_W = HEADS_PER_GROUP * HEAD_DIM
N_EXPERTS = 32
EXPERTS_PER_GROUP = 8
N_EXPERT_GROUPS = 4
SAMPLE_ROWS = 8
CONF_PAD = 32
SCONV_PAD = 8
XBC_GROUP_W = GROUP_W + 2 * N_STATE
ZX_GROUP_W = GROUP_W + XBC_GROUP_W
GSEL_LANE = 64


def _rms(x, w):
    return x * lax.rsqrt(jnp.mean(x * x, axis=-1, keepdims=True) + EPS) * w


def _sigmoid(x):
    return 0.5 * jnp.tanh(0.5 * x) + 0.5


def _silu(x):
    return x * _sigmoid(x)


def _dot(a, b):
    return jnp.dot(a, b, preferred_element_type=F32)


def _dot_nt(a, b):
    return lax.dot_general(a, b, (((1,), (1,)), ((), ())), preferred_element_type=F32)


def _dot_tn(a, b):
    return lax.dot_general(a, b, (((0,), (0,)), ((), ())), preferred_element_type=F32)


def _const_spec(shape):
    zeros = (0,) * len(shape)
    return pl.BlockSpec(shape, lambda *_: zeros, pipeline_mode=pl.Buffered(1))


def _dwconv(src_ref, base, rows, w_ref, b_ref, taps, width, emit, col0=0):
    for c in range(col0 // LANES, (col0 + width) // LANES):
        cs = slice(c * LANES, (c + 1) * LANES)
        acc = jnp.broadcast_to(b_ref[:, cs], (rows, LANES))
        for r in range(min(SUBLANES, taps)):
            qs = range((taps - r + SUBLANES - 1) // SUBLANES)
            slab = src_ref[pl.ds(base + r, rows + SUBLANES * (len(qs) - 1)), cs]
            part = w_ref[r:r + 1, cs] * slab[0:rows]
            for q in qs[1:]:
                k = SUBLANES * q + r
                part = part + w_ref[k:k + 1, cs] * slab[SUBLANES * q:SUBLANES * q + rows]
            acc = acc + part
        emit(cs, acc)


def _conf_tail(conv_ref, ln_g_ref, ln_b_ref, wout_ref, bout_ref):
    c = conv_ref[...]
    mu = jnp.mean(c, axis=-1, keepdims=True)
    d = c - mu
    var = jnp.mean(d * d, axis=-1, keepdims=True)
    y = d * lax.rsqrt(var + EPS) * ln_g_ref[...] + ln_b_ref[...]
    y = _silu(y)
    return _dot(y.astype(BF16), wout_ref[...]) + bout_ref[...]


def _conf_glu(x, nw_ref, wvg_ref):
    d = x.shape[-1]
    xn = _rms(x, nw_ref[...]).astype(BF16)
    vg = _dot(xn, wvg_ref[...])
    return vg[:, :d] * _sigmoid(vg[:, d:])


def _conf_prompt_kernel(meta_ref, x_ref, nw_ref, wvg_ref, dww_ref, dwb_ref, lng_ref, lnb_ref, wout_ref, bout_ref,
                        ya_ref, nc_ref, afull_ref, conv_ref):
    j = pl.program_id(1)
    nq, rows, d = x_ref.shape

    @pl.when(j == 0)
    def _():
        a_meta = _conf_glu(meta_ref[...], nw_ref, wvg_ref)
        for q in range(nq):
            afull_ref[q, 0:CONF_PAD - N_META, :] = jnp.zeros((CONF_PAD - N_META, d), F32)
            afull_ref[q, CONF_PAD - N_META:CONF_PAD, :] = a_meta

    a = _conf_glu(x_ref[...].reshape(nq * rows, d), nw_ref, wvg_ref)
    for q in range(nq):
        afull_ref[q, CONF_PAD:CONF_PAD + rows, :] = a[q * rows:(q + 1) * rows]
        for r0 in range(0, rows, TILE):
            def emit(cs, v, r0=q * rows + r0):
                conv_ref[r0:r0 + TILE, cs] = v

            _dwconv(afull_ref.at[q], CONF_PAD - (CONF_K - 1) + r0, TILE, dww_ref, dwb_ref, CONF_K, d, emit)
        nc_ref[q] = afull_ref[q, pl.ds(CONF_PAD + rows - (CONF_K - 1), CONF_K - 1), :]
        afull_ref[q, 0:CONF_PAD, :] = afull_ref[q, rows:rows + CONF_PAD, :]
    ya_ref[...] = _conf_tail(conv_ref, lng_ref, lnb_ref, wout_ref, bout_ref).reshape(nq, rows, d)


def _conf_sample_kernel(x_ref, st_ref, nw_ref, wvg_ref, dww_ref, dwb_ref, lng_ref, lnb_ref, wout_ref, bout_ref,
                        ya_ref, nc_ref, a_ref, cs_ref, conv_ref, *, n_new):
    d = x_ref.shape[-1]
    n_seq = TILE // SAMPLE_ROWS
    a_ref[...] = _conf_glu(x_ref[...], nw_ref, wvg_ref)

    def seq_body(s, carry):
        r0 = pl.multiple_of(s * SAMPLE_ROWS, SAMPLE_ROWS)
        cs_ref[CONF_PAD - (CONF_K - 1):CONF_PAD, :] = st_ref[s]
        cs_ref[CONF_PAD:CONF_PAD + SAMPLE_ROWS, :] = a_ref[pl.ds(r0, SAMPLE_ROWS), :]

        def emit(cs, v):
            conv_ref[pl.ds(r0, SAMPLE_ROWS), cs] = v

        _dwconv(cs_ref, CONF_PAD - (CONF_K - 1), SAMPLE_ROWS, dww_ref, dwb_ref, CONF_K, d, emit)
        nc_ref[s] = cs_ref[pl.ds(CONF_PAD + n_new - (CONF_K - 1), CONF_K - 1), :]
        return carry

    lax.fori_loop(0, n_seq, seq_body, 0)
    ya_ref[...] = _conf_tail(conv_ref, lng_ref, lnb_ref, wout_ref, bout_ref)


def _conf_weights(p):
    d = p["norm_w"].shape[-1]
    return [p["norm_w"], p["w_vg"], p["conf_dw_w"], p["conf_dw_b"], p["conf_ln_g"], p["conf_ln_b"],
            p["conf_out_w"], p["conf_out_b"]], [
        _const_spec((1, d)), _const_spec((d, 2 * d)), _const_spec((CONF_PAD, d)), _const_spec((1, d)),
        _const_spec((1, d)), _const_spec((1, d)), _const_spec((d, d)), _const_spec((1, d))]


def _conf_prompt(meta, x_prompt, p):
    b, l, d = x_prompt.shape
    rows = 2 * TILE if l % (2 * TILE) == 0 else TILE
    nq = 2 if b % 2 == 0 else 1
    weights, wspecs = _conf_weights(p)
    xmap = lambda i, j: (i, j, 0)
    return pl.pallas_call(
        _conf_prompt_kernel,
        grid=(b // nq, l // rows),
        in_specs=[_const_spec(meta.shape), pl.BlockSpec((nq, rows, d), xmap)] + wspecs,
        out_specs=[pl.BlockSpec((nq, rows, d), xmap),
                   pl.BlockSpec((nq, CONF_K - 1, d), lambda i, j: (i, 0, 0))],
        out_shape=[jax.ShapeDtypeStruct((b, l, d), F32), jax.ShapeDtypeStruct((b, CONF_K - 1, d), F32)],
        scratch_shapes=[pltpu.VMEM((nq, CONF_PAD + rows, d), F32), pltpu.VMEM((nq * rows, d), F32)],
        compiler_params=pltpu.CompilerParams(dimension_semantics=("arbitrary", "arbitrary"),
                                             vmem_limit_bytes=VMEM_LIMIT),
        name="conf_prompt",
    )(meta, x_prompt, *weights)


def _conf_sample(x_rows, state, p, n_new):
    rows, d = x_rows.shape
    n_seq = TILE // SAMPLE_ROWS
    weights, wspecs = _conf_weights(p)
    return pl.pallas_call(
        functools.partial(_conf_sample_kernel, n_new=n_new),
        grid=(rows // TILE,),
        in_specs=[pl.BlockSpec((TILE, d), lambda i: (i, 0)),
                  pl.BlockSpec((n_seq, CONF_K - 1, d), lambda i: (i, 0, 0))] + wspecs,
        out_specs=[pl.BlockSpec((TILE, d), lambda i: (i, 0)),
                   pl.BlockSpec((n_seq, CONF_K - 1, d), lambda i: (i, 0, 0))],
        out_shape=[jax.ShapeDtypeStruct((rows, d), F32),
                   jax.ShapeDtypeStruct((rows // SAMPLE_ROWS, CONF_K - 1, d), F32)],
        scratch_shapes=[pltpu.VMEM((TILE, d), F32), pltpu.VMEM((CONF_PAD + SAMPLE_ROWS, d), F32),
                        pltpu.VMEM((TILE, d), F32)],
        compiler_params=pltpu.CompilerParams(dimension_semantics=("arbitrary",), vmem_limit_bytes=VMEM_LIMIT),
        name="conf_sample",
    )(x_rows, state, *weights)


def _xbc_col_pairs(d_inner):
    pairs = []
    for g in range(N_GROUPS):
        c0 = g * XBC_GROUP_W
        pairs.append((slice(c0, c0 + GROUP_W), slice(g * GROUP_W, (g + 1) * GROUP_W)))
        for k in range(2):
            ours = c0 + GROUP_W + k * N_STATE
            theirs = d_inner + (k * N_GROUPS + g) * N_STATE
            pairs.append((slice(ours, ours + N_STATE), slice(theirs, theirs + N_STATE)))
    return pairs


def _project_group(g, xn, wzx_ref, z_ref, pre_ref, row0):
    zx = _dot(xn, wzx_ref[:, g * ZX_GROUP_W:(g + 1) * ZX_GROUP_W])
    z_ref[:, g * GROUP_W:(g + 1) * GROUP_W] = zx[:, :GROUP_W]
    pre_ref[row0:row0 + xn.shape[0], g * XBC_GROUP_W:(g + 1) * XBC_GROUP_W] = zx[:, GROUP_W:]


def _seg_cumsum(x, row, seg_len, reverse):
    pos = row % seg_len
    n = x.shape[0]
    step = 1
    while step < seg_len:
        if reverse:
            x = x + jnp.where(pos < seg_len - step, pltpu.roll(x, n - step, axis=0), 0.0)
        else:
            x = x + jnp.where(pos >= step, pltpu.roll(x, step, axis=0), 0.0)
        step *= 2
    return x


def _expand_heads(m, g, lane):
    rows = m.shape[0]
    cols = [jnp.broadcast_to(m[:, HEADS_PER_GROUP * g + i:HEADS_PER_GROUP * g + i + 1], (rows, LANES))
            for i in range(HEADS_PER_GROUP)]
    lo = jnp.where(lane < HEAD_DIM, cols[0], cols[1])
    hi = jnp.where(lane < HEAD_DIM, cols[2], cols[3])
    return jnp.concatenate([lo, hi], axis=1)


def _ssd_prelude(dtr, valid, seg_len, dtb_ref, alog_ref):
    row = lax.broadcasted_iota(jnp.int32, (TILE, TILE), 0)
    col = lax.broadcasted_iota(jnp.int32, (TILE, TILE), 1)
    same_seq = (col // seg_len) == (row // seg_len)
    causal = (col <= row) & same_seq
    xdt = dtr + dtb_ref[...]
    e = jnp.exp(-jnp.abs(xdt))
    u = 1.0 + e
    dt = (jnp.maximum(xdt, 0.0) + jnp.where(u == 1.0, e, jnp.log(u) * e / (u - 1.0))) * valid
    da = dt * (-jnp.exp(alog_ref[...]))
    d1 = da.astype(BF16)
    r1 = da - d1.astype(F32)
    d2 = r1.astype(BF16)
    d3 = (r1 - d2.astype(F32)).astype(BF16)
    lower = jnp.where(causal, 1.0, 0.0).astype(BF16)
    upper = jnp.where((col > row) & same_seq, 1.0, 0.0).astype(BF16)
    cum = _dot(lower, d1) + _dot(lower, d2) + _dot(lower, d3)
    rest = _dot(upper, d1) + _dot(upper, d2) + _dot(upper, d3)
    return {
        "causal": causal,
        "cum": cum,
        "cum_t": cum.T,
        "dt_t": dt.T,
        "ecum": jnp.exp(cum),
        "tail": jnp.exp(rest) * dt,
    }


_PRE_KEYS = ("cum", "cum_t", "dt_t", "ecum", "tail")


def _ssd_group(g, pre, z_ref, xc_ref, seg_len, n_seq, h_in, h_out, dskip_ref, normw_ref, after_first_dot=None):
    lane = lax.broadcasted_iota(jnp.int32, (TILE, LANES), 1)
    lane_g = lax.broadcasted_iota(jnp.int32, (TILE, GROUP_W), 1)
    cum, cum_t, dt_t = pre["cum"], pre["cum_t"], pre["dt_t"]
    c0 = g * XBC_GROUP_W
    gs = slice(g * GROUP_W, (g + 1) * GROUP_W)
    xs = xc_ref[:, c0:c0 + GROUP_W]
    zg = z_ref[:, gs]
    bm16 = xc_ref[:, c0 + GROUP_W:c0 + GROUP_W + N_STATE].astype(BF16)
    cm16 = xc_ref[:, c0 + GROUP_W + N_STATE:c0 + XBC_GROUP_W].astype(BF16)
    cb = _dot_nt(cm16, bm16)
    extra = after_first_dot() if after_first_dot is not None else None
    seqs = [(s, slice(s * seg_len, (s + 1) * seg_len), (s + 1) * seg_len - 1) for s in range(n_seq)]
    ystate = [_dot_nt(cm16[rs], h_in(s, g).astype(BF16)) for s, rs, _ in seqs]
    if n_seq * seg_len < TILE:
        ystate.append(jnp.zeros((TILE - n_seq * seg_len, GROUP_W), F32))
    ys = ystate[0] if len(ystate) == 1 else jnp.concatenate(ystate, axis=0)
    ws, xms = [], []
    for i in range(HEADS_PER_GROUP):
        h = HEADS_PER_GROUP * g + i
        seg = cum[:, h:h + 1] - cum_t[h:h + 1, :]
        decay = jnp.exp(jnp.where(pre["causal"], seg, -jnp.inf))
        ws.append((cb * decay * dt_t[h:h + 1, :]).astype(BF16))
        xms.append(jnp.where((lane_g // HEAD_DIM) == i, xs, 0.0).astype(BF16))
    yg = _dot(jnp.concatenate(ws, axis=1), jnp.concatenate(xms, axis=0))
    yg = yg + ys * _expand_heads(pre["ecum"], g, lane)
    xw16 = (xs * _expand_heads(pre["tail"], g, lane)).astype(BF16)

    def update_state():
        for s, rs, last in seqs:
            upd = _dot_tn(xw16[rs], bm16[rs])
            dec = jnp.concatenate(
                [jnp.broadcast_to(jnp.exp(cum_t[HEADS_PER_GROUP * g + i:HEADS_PER_GROUP * g + i + 1, last:last + 1]),
                                  (HEAD_DIM, N_STATE)) for i in range(HEADS_PER_GROUP)], axis=0)
            h_out(s, g, h_in(s, g) * dec + upd)

    v = (yg + dskip_ref[:, gs] * xs) * _silu(zg)
    v = v * lax.rsqrt(jnp.mean(v * v, axis=-1, keepdims=True) + EPS) * normw_ref[:, gs]
    return v.astype(BF16), update_state, extra


def _out_proj_group(g, v16, wout_ref):
    return _dot(v16, wout_ref[g * GROUP_W:(g + 1) * GROUP_W, :])


def _ssm_prompt_kernel(x_ref, hist_ref, h0_ref, nw_ref, wzx_ref, wdt_ref, cw_ref, cb_ref, dtb_ref, alog_ref,
                       dskip_ref, normw_ref, wout_ref, yb_ref, ns_ref, hs_ref, z_ref, cfull_ref, xc_ref, pre_ref, *,
                       n_inert):
    s = pl.program_id(1)
    n_tiles = pl.num_programs(1) - 1
    nq, _, d = x_ref.shape
    seqs = range(nq)

    @pl.when(s == 0)
    def _():
        for q in seqs:
            cfull_ref[q, 0:SCONV_PAD, :] = hist_ref[...]
            hs_ref[q] = h0_ref[...]
        z_ref[...] = jnp.zeros(z_ref.shape, F32)
        xc_ref[...] = jnp.zeros(xc_ref.shape, F32)
        pre_ref[...] = jnp.zeros(pre_ref.shape, F32)

    xn = _rms(x_ref[...].reshape(nq * TILE, d), nw_ref[...]).astype(BF16)
    row = lax.broadcasted_iota(jnp.int32, (TILE, TILE), 0)
    col = lax.broadcasted_iota(jnp.int32, (TILE, TILE), 1)
    pres = []
    for q in seqs:
        pre = {k: pre_ref[q, i] for i, k in enumerate(_PRE_KEYS)}
        pre["causal"] = col <= row
        pres.append(pre)
    rowv = lax.broadcasted_iota(jnp.int32, (TILE, 1), 0)
    valid = jnp.where((s == 0) & (rowv < n_inert), 0.0, 1.0)
    dtr = _dot(xn, wdt_ref[...])

    def state_access(q):
        def h_in(_, g):
            return hs_ref[q, HEADS_PER_GROUP * g:HEADS_PER_GROUP * (g + 1)].reshape(GROUP_W, N_STATE)

        def h_out(_, g, v):
            hs_ref[q, HEADS_PER_GROUP * g:HEADS_PER_GROUP * (g + 1)] = v.reshape(HEADS_PER_GROUP, HEAD_DIM, N_STATE)

        return h_in, h_out

    acc = [jnp.zeros((TILE, d), F32) for _ in seqs]
    pending = []

    def finish_oldest():
        v16, update, q, g = pending.pop(0)
        acc[q] = acc[q] + _out_proj_group(g, v16, wout_ref)
        update()

    for g in range(N_GROUPS):
        projected = {}
        for q in seqs:
            def independent_dots(g=g, q=q, projected=projected):
                if q == 0:
                    projected["zx"] = _dot(xn, wzx_ref[:, g * ZX_GROUP_W:(g + 1) * ZX_GROUP_W])
                zx = projected["zx"][q * TILE:(q + 1) * TILE]
                z_ref[q, :, g * GROUP_W:(g + 1) * GROUP_W] = zx[:, :GROUP_W]
                cfull_ref[q, SCONV_PAD:SCONV_PAD + TILE, g * XBC_GROUP_W:(g + 1) * XBC_GROUP_W] = zx[:, GROUP_W:]
                if len(pending) >= nq:
                    finish_oldest()

            h_in, h_out = state_access(q)
            v16, update, _ = _ssd_group(g, pres[q], z_ref.at[q], xc_ref.at[q], TILE, 1, h_in, h_out, dskip_ref,
                                        normw_ref, independent_dots)

            def emit(cs, v, q=q):
                xc_ref[q, :, cs] = _silu(v)

            _dwconv(cfull_ref.at[q], SCONV_PAD - (SSM_K - 1), TILE, cw_ref, cb_ref, SSM_K, XBC_GROUP_W, emit,
                    col0=g * XBC_GROUP_W)
            pending.append((v16, update, q, g))
    while pending:
        finish_oldest()
    for q in seqs:
        yb_ref[q] = acc[q]
        pre_next = _ssd_prelude(dtr[q * TILE:(q + 1) * TILE], valid, TILE, dtb_ref, alog_ref)
        for i, k in enumerate(_PRE_KEYS):
            pre_ref[q, i] = pre_next[k]

    @pl.when(s == n_tiles - 1)
    def _():
        for q in seqs:
            ns_ref[q] = cfull_ref[q, pl.ds(SCONV_PAD + TILE - (SSM_K - 1), SSM_K - 1), :]

    for q in seqs:
        cfull_ref[q, 0:SCONV_PAD, :] = cfull_ref[q, TILE:TILE + SCONV_PAD, :]


def _ssm_sample_proj_kernel(x_ref, st_ref, nw_ref, wzx_ref, wdt_ref, cw_ref, cb_ref,
                            z_ref, xc_ref, dtr_ref, ns_ref, xbc_ref, cs_ref, *, n_new):
    conv_dim = xc_ref.shape[-1]
    d_inner = z_ref.shape[-1]
    n_seq = TILE // SAMPLE_ROWS
    xn = _rms(x_ref[...], nw_ref[...]).astype(BF16)
    for g in range(N_GROUPS):
        _project_group(g, xn, wzx_ref, z_ref, xbc_ref, 0)
    dtr_ref[...] = _dot(xn, wdt_ref[...])
    col_pairs = _xbc_col_pairs(d_inner)

    def seq_body(s, carry):
        r0 = pl.multiple_of(s * SAMPLE_ROWS, SAMPLE_ROWS)
        hist = st_ref[s]
        for ours, theirs in col_pairs:
            cs_ref[SCONV_PAD - (SSM_K - 1):SCONV_PAD, ours] = hist[:, theirs]
        cs_ref[SCONV_PAD:SCONV_PAD + SAMPLE_ROWS, :] = xbc_ref[pl.ds(r0, SAMPLE_ROWS), :]

        def emit(cs, v):
            xc_ref[pl.ds(r0, SAMPLE_ROWS), cs] = _silu(v)

        _dwconv(cs_ref, SCONV_PAD - (SSM_K - 1), SAMPLE_ROWS, cw_ref, cb_ref, SSM_K, conv_dim, emit)
        for ours, theirs in col_pairs:
            ns_ref[s, :, theirs] = cs_ref[pl.ds(SCONV_PAD + n_new - (SSM_K - 1), SSM_K - 1), ours]
        return carry

    lax.fori_loop(0, n_seq, seq_body, 0)


def _ssm_sample_ssd_kernel(z_in, xc_in, dtr_in, hin_ref, dtb_ref, alog_ref, dskip_ref, normw_ref, wout_ref,
                           yb_ref, hout_ref, z_ref, xc_ref, dtr_ref, *, n_new, n_seq):
    rows = n_seq * SAMPLE_ROWS
    for src, dst in ((z_in, z_ref), (xc_in, xc_ref), (dtr_in, dtr_ref)):
        dst[0:rows, :] = src[...]
        dst[rows:TILE, :] = jnp.zeros((TILE - rows, dst.shape[-1]), F32)
    rowv = lax.broadcasted_iota(jnp.int32, (TILE, 1), 0)
    valid = jnp.where((rowv < rows) & ((rowv % SAMPLE_ROWS) < n_new), 1.0, 0.0)

    def h_in(s, g):
        return hin_ref[s, HEADS_PER_GROUP * g:HEADS_PER_GROUP * (g + 1)].reshape(GROUP_W, N_STATE)

    def h_out(s, g, v):
        hout_ref[s, HEADS_PER_GROUP * g:HEADS_PER_GROUP * (g + 1)] = v.reshape(HEADS_PER_GROUP, HEAD_DIM, N_STATE)

    pre = _ssd_prelude(dtr_ref[...], valid, SAMPLE_ROWS, dtb_ref, alog_ref)
    acc = jnp.zeros((TILE, yb_ref.shape[-1]), F32)
    for g in range(N_GROUPS):
        v16, update, _ = _ssd_group(g, pre, z_ref, xc_ref, SAMPLE_ROWS, n_seq, h_in, h_out, dskip_ref, normw_ref)
        acc = acc + _out_proj_group(g, v16, wout_ref)
        update()
    yb_ref[...] = acc[0:rows, :]


def _ssm_prompt(x, hist, h0, p, n_inert):
    b, l, d = x.shape
    nb = l // TILE
    nq = 2 if b % 2 == 0 else 1
    d_inner = p["ssm_norm_w"].shape[-1]
    conv_dim = p["ssm_conv_b"].shape[-1]
    n_heads = d_inner // HEAD_DIM
    xmap = lambda i, j: (i, jnp.minimum(j, nb - 1), 0)
    ymap = lambda i, j: (i, jnp.maximum(j - 1, 0), 0)
    weights = [hist, h0, p["norm_w"], p["w_zx"], p["w_dt"], p["ssm_conv_w"], p["ssm_conv_b"], p["dt_bias"],
               p["a_log"], p["d_skip"], p["ssm_norm_w"], p["ssm_out_w"]]
    wspecs = [_const_spec(w.shape) for w in weights]
    return pl.pallas_call(
        functools.partial(_ssm_prompt_kernel, n_inert=n_inert),
        grid=(b // nq, nb + 1),
        in_specs=[pl.BlockSpec((nq, TILE, d), xmap)] + wspecs,
        out_specs=[pl.BlockSpec((nq, TILE, d), ymap),
                   pl.BlockSpec((nq, SSM_K - 1, conv_dim), lambda i, j: (i, 0, 0)),
                   pl.BlockSpec((nq, n_heads, HEAD_DIM, N_STATE), lambda i, j: (i, 0, 0, 0))],
        out_shape=[jax.ShapeDtypeStruct((b, l, d), F32), jax.ShapeDtypeStruct((b, SSM_K - 1, conv_dim), F32),
                   jax.ShapeDtypeStruct((b, n_heads, HEAD_DIM, N_STATE), F32)],
        scratch_shapes=[pltpu.VMEM((nq, TILE, d_inner), F32), pltpu.VMEM((nq, SCONV_PAD + TILE, conv_dim), F32),
                        pltpu.VMEM((nq, TILE, conv_dim), F32), pltpu.VMEM((nq, len(_PRE_KEYS), TILE, LANES), F32)],
        compiler_params=pltpu.CompilerParams(dimension_semantics=("arbitrary", "arbitrary"),
                                             vmem_limit_bytes=VMEM_LIMIT),
        name="ssm_prompt",
    )(x, *weights)


def _ssm_sample(x_rows, sconv, h0, p, n_new):
    rows, d = x_rows.shape
    d_inner = p["ssm_norm_w"].shape[-1]
    conv_dim = p["ssm_conv_b"].shape[-1]
    n_heads = d_inner // HEAD_DIM
    n_seq_proj = TILE // SAMPLE_ROWS
    weights = [p["norm_w"], p["w_zx"], p["w_dt"], p["ssm_conv_w"], p["ssm_conv_b"]]
    z, xc, dtr, new_sconv = pl.pallas_call(
        functools.partial(_ssm_sample_proj_kernel, n_new=n_new),
        grid=(rows // TILE,),
        in_specs=[pl.BlockSpec((TILE, d), lambda i: (i, 0)),
                  pl.BlockSpec((n_seq_proj, SSM_K - 1, conv_dim), lambda i: (i, 0, 0))]
        + [_const_spec(w.shape) for w in weights],
        out_specs=[pl.BlockSpec((TILE, d_inner), lambda i: (i, 0)), pl.BlockSpec((TILE, conv_dim), lambda i: (i, 0)),
                   pl.BlockSpec((TILE, LANES), lambda i: (i, 0)),
                   pl.BlockSpec((n_seq_proj, SSM_K - 1, conv_dim), lambda i: (i, 0, 0))],
        out_shape=[jax.ShapeDtypeStruct((rows, d_inner), F32), jax.ShapeDtypeStruct((rows, conv_dim), F32),
                   jax.ShapeDtypeStruct((rows, LANES), F32),
                   jax.ShapeDtypeStruct((rows // SAMPLE_ROWS, SSM_K - 1, conv_dim), F32)],
        scratch_shapes=[pltpu.VMEM((TILE, conv_dim), F32), pltpu.VMEM((SCONV_PAD + SAMPLE_ROWS, conv_dim), F32)],
        compiler_params=pltpu.CompilerParams(dimension_semantics=("arbitrary",), vmem_limit_bytes=VMEM_LIMIT),
        name="ssm_sample_proj",
    )(x_rows, sconv, *weights)

    n_seq = 8 if rows % (8 * SAMPLE_ROWS) == 0 else 4
    r = n_seq * SAMPLE_ROWS
    weights = [p["dt_bias"], p["a_log"], p["d_skip"], p["ssm_norm_w"], p["ssm_out_w"]]
    yb, h_new = pl.pallas_call(
        functools.partial(_ssm_sample_ssd_kernel, n_new=n_new, n_seq=n_seq),
        grid=(rows // r,),
        in_specs=[pl.BlockSpec((r, d_inner), lambda i: (i, 0)), pl.BlockSpec((r, conv_dim), lambda i: (i, 0)),
                  pl.BlockSpec((r, LANES), lambda i: (i, 0)),
                  pl.BlockSpec((n_seq, n_heads, HEAD_DIM, N_STATE), lambda i: (i, 0, 0, 0))]
        + [_const_spec(w.shape) for w in weights],
        out_specs=[pl.BlockSpec((r, d), lambda i: (i, 0)),
                   pl.BlockSpec((n_seq, n_heads, HEAD_DIM, N_STATE), lambda i: (i, 0, 0, 0))],
        out_shape=[jax.ShapeDtypeStruct((rows, d), F32), jax.ShapeDtypeStruct(h0.shape, F32)],
        scratch_shapes=[pltpu.VMEM((TILE, d_inner), F32), pltpu.VMEM((TILE, conv_dim), F32),
                        pltpu.VMEM((TILE, LANES), F32)],
        compiler_params=pltpu.CompilerParams(dimension_semantics=("arbitrary",), vmem_limit_bytes=VMEM_LIMIT),
        name="ssm_sample_ssd",
    )(z, xc, dtr, h0, *weights)
    return yb, new_sconv, h_new


def _merge_kernel(x_ref, ya_ref, yb_ref, nw_ref, wg_ref, gb_ref, wo_ref, nfw_ref, rwh_ref, rwl_ref, rb_ref,
                  x1_ref, xn2_ref, comb_ref):
    d = x_ref.shape[-1]
    half = x_ref.shape[0] // 2
    parts = [slice(0, half), slice(half, 2 * half)]
    xs = [x_ref[rs, :] for rs in parts]
    xns = [_rms(x, nw_ref[...]).astype(BF16) for x in xs]
    glog = [_dot(xn, wg_ref[...]) for xn in xns]
    merged = []
    for rs, gl_ in zip(parts, glog):
        gates = _sigmoid(gl_ + gb_ref[...])
        merged.append((gates[:, :d] * ya_ref[rs, :] + gates[:, d:] * yb_ref[rs, :]).astype(BF16))
    outs = [_dot(m, wo_ref[...]) for m in merged]
    logit_parts = []
    for rs, x, o in zip(parts, xs, outs):
        x1 = x + o
        x1_ref[rs, :] = x1
        xn2 = _rms(x1, nfw_ref[...])
        x_hi = xn2.astype(BF16)
        xn2_ref[rs, :] = x_hi
        x_lo = (xn2 - x_hi.astype(F32)).astype(BF16)
        logit_parts.append(_dot(x_hi, rwh_ref[...]) + _dot(x_lo, rwh_ref[...]) + _dot(x_hi, rwl_ref[...]))

    logits = jnp.concatenate(logit_parts, axis=0) + rb_ref[...]
    rows = logits.shape[0]
    lane = lax.broadcasted_iota(jnp.int32, (rows, LANES), 1)
    neg = -jnp.inf
    is_g = (lane >= N_EXPERTS) & (lane < N_EXPERTS + N_EXPERT_GROUPS)
    gl = jnp.where(is_g, logits, neg)
    gmax = jnp.max(gl, axis=-1, keepdims=True)
    gsel = jnp.min(jnp.where(gl == gmax, lane, LANES), axis=-1, keepdims=True) - N_EXPERTS
    gprob = 1.0 / jnp.sum(jnp.exp(gl - gmax), axis=-1, keepdims=True)
    el = jnp.where((lane < N_EXPERTS) & ((lane // EXPERTS_PER_GROUP) == gsel), logits, neg)
    m1 = jnp.max(el, axis=-1, keepdims=True)
    i1 = jnp.min(jnp.where(el == m1, lane, LANES), axis=-1, keepdims=True)
    el2 = jnp.where(lane == i1, neg, el)
    m2 = jnp.max(el2, axis=-1, keepdims=True)
    i2 = jnp.min(jnp.where(el2 == m2, lane, LANES), axis=-1, keepdims=True)
    e2 = jnp.exp(m2 - m1)
    den = 1.0 + e2
    comb = jnp.where(lane == i1, (1.0 / den) * gprob, jnp.where(lane == i2, (e2 / den) * gprob, 0.0))
    comb_ref[...] = jnp.where(lane == GSEL_LANE, gsel.astype(F32), comb)


def _merge(x, ya, yb, p, tm):
    rows, d = x.shape
    weights = [p["norm_w"], p["w_gate"], p["gate_b"], p["w_o"], p["norm_ffn_w"], p["router_w_hi"],
               p["router_w_lo"], p["router_b"]]
    row_spec = pl.BlockSpec((tm, d), lambda i: (i, 0))
    return pl.pallas_call(
        _merge_kernel,
        grid=(rows // tm,),
        in_specs=[row_spec, row_spec, row_spec] + [_const_spec(w.shape) for w in weights],
        out_specs=[row_spec, row_spec, pl.BlockSpec((tm, LANES), lambda i: (i, 0))],
        out_shape=[jax.ShapeDtypeStruct((rows, d), F32), jax.ShapeDtypeStruct((rows, d), BF16),
                   jax.ShapeDtypeStruct((rows, LANES), F32)],
        compiler_params=pltpu.CompilerParams(dimension_semantics=("arbitrary",), vmem_limit_bytes=VMEM_LIMIT),
        name="merge_router",
    )(x, ya, yb, *weights)


def _moe_kernel(xn_ref, comb_ref, x1_ref, w1_ref, w3_ref, w2_ref, nw_ref, y_ref, rt_ref, *, chunk):
    g = pl.program_id(1)
    tm = xn_ref.shape[0]
    comb = comb_ref[...]
    lane = lax.broadcasted_iota(jnp.int32, (tm, LANES), 1)
    gsel = comb[:, GSEL_LANE:GSEL_LANE + 1]

    @pl.when(g == 0)
    def _():
        row = lax.broadcasted_iota(jnp.int32, (tm, LANES), 0)
        onehot = jnp.where(lane.astype(F32) == gsel, 1.0, 0.0)
        incl = _seg_cumsum(onehot, row, tm, reverse=False)
        rank = jnp.sum(onehot * (incl - onehot), axis=-1, keepdims=True)
        packed = jnp.where(lane == 0, rank, jnp.where(lane == 1, gsel, 0.0))
        rt_ref[0:SUBLANES, :] = packed.T[0:SUBLANES, :]
        rt_ref[SUBLANES:2 * SUBLANES, 0:LANES] = jnp.broadcast_to(incl[tm - 1:tm, :], (SUBLANES, LANES))
        y_ref[...] = jnp.zeros(y_ref.shape, F32)

    gf = g.astype(F32)
    lane_row = lax.broadcasted_iota(jnp.int32, (1, LANES), 1)
    count = jnp.sum(jnp.where(lane_row == g, rt_ref[SUBLANES:SUBLANES + 1, 0:LANES], 0.0), axis=-1, keepdims=True)
    n_chunks = (count[0, 0].astype(jnp.int32) + chunk - 1) // chunk
    w_hi = jnp.where(lane < N_EXPERTS, comb, 0.0).astype(BF16)
    w_lo = (jnp.where(lane < N_EXPERTS, comb, 0.0) - w_hi.astype(F32)).astype(BF16)
    rowi = lax.broadcasted_iota(jnp.int32, (chunk, tm), 0).astype(F32)
    lane_c = lax.broadcasted_iota(jnp.int32, (chunk, LANES), 1)

    def chunk_body(c, carry):
        base = (c * chunk).astype(F32)
        sel = jnp.where((rt_ref[1:2, :] == gf) & ((rt_ref[0:1, :] - base) == rowi), 1.0, 0.0).astype(BF16)
        xc = _dot(sel, xn_ref[...]).astype(BF16)
        wc = _dot(sel, w_hi) + _dot(sel, w_lo)
        acts = []
        for e in range(EXPERTS_PER_GROUP):
            ce = jnp.sum(jnp.where(lane_c == g * EXPERTS_PER_GROUP + e, wc, 0.0), axis=-1, keepdims=True)
            acts.append((_silu(_dot(xc, w1_ref[e])) * _dot(xc, w3_ref[e]) * ce).astype(BF16))
        y = _dot(jnp.concatenate(acts, axis=1), w2_ref[...])
        y_ref[...] += _dot_tn(sel, y.astype(BF16))
        return carry

    lax.fori_loop(0, n_chunks, chunk_body, 0)

    @pl.when(g == pl.num_programs(1) - 1)
    def _():
        y_ref[...] = _rms(x1_ref[...] + y_ref[...], nw_ref[...])


def _moe(xn2, comb, x1, p, tm):
    rows, d = x1.shape
    n_e, _, d_e = p["exp_w1"].shape
    n_g = n_e // EXPERTS_PER_GROUP
    w2 = p["exp_w2"].reshape(n_g, EXPERTS_PER_GROUP * d_e, d)
    pack = 2 * SUBLANES
    chunk = -(-(tm * 9 // (8 * n_g)) // pack) * pack
    row = lambda i, g: (i, 0)
    return pl.pallas_call(
        functools.partial(_moe_kernel, chunk=chunk),
        grid=(rows // tm, n_g),
        in_specs=[pl.BlockSpec((tm, d), row), pl.BlockSpec((tm, LANES), row),
                  pl.BlockSpec((tm, d), row),
                  pl.BlockSpec((EXPERTS_PER_GROUP, d, d_e), lambda i, g: (g, 0, 0)),
                  pl.BlockSpec((EXPERTS_PER_GROUP, d, d_e), lambda i, g: (g, 0, 0)),
                  pl.BlockSpec((None, EXPERTS_PER_GROUP * d_e, d), lambda i, g: (g, 0, 0)),
                  _const_spec((1, d))],
        out_specs=pl.BlockSpec((tm, d), row),
        out_shape=jax.ShapeDtypeStruct((rows, d), F32),
        scratch_shapes=[pltpu.VMEM((2 * SUBLANES, tm), F32)],
        compiler_params=pltpu.CompilerParams(dimension_semantics=("arbitrary", "arbitrary"),
                                             vmem_limit_bytes=VMEM_LIMIT),
        name="moe",
    )(xn2, comb, x1, p["exp_w1"], p["exp_w3"], w2, p["norm_final_w"])


def _group_xbc_cols(v, d_inner):
    lead = v.shape[:-1]
    xs = v[..., :d_inner].reshape(lead + (N_GROUPS, GROUP_W))
    bm = v[..., d_inner:d_inner + N_GROUPS * N_STATE].reshape(lead + (N_GROUPS, N_STATE))
    cm = v[..., d_inner + N_GROUPS * N_STATE:].reshape(lead + (N_GROUPS, N_STATE))
    return jnp.concatenate([xs, bm, cm], axis=-1).reshape(lead + (-1,))


def _ungroup_xbc_cols(v, d_inner):
    lead = v.shape[:-1]
    v = v.reshape(lead + (N_GROUPS, XBC_GROUP_W))
    parts = [v[..., :GROUP_W], v[..., GROUP_W:GROUP_W + N_STATE], v[..., GROUP_W + N_STATE:]]
    return jnp.concatenate([q.reshape(lead + (-1,)) for q in parts], axis=-1)


def _prep_params(norm_mix_w, w_in, conf_dw_w, conf_dw_b, conf_ln_g, conf_ln_b, conf_out_w, conf_out_b, ssm_conv_w,
                 ssm_conv_b, dt_bias, a_log, d_skip, ssm_norm_w, ssm_out_w, gate_b, w_o, norm_ffn_w,
                 router_group_w, router_group_b, router_expert_w, router_expert_b, exp_w1, exp_w3, exp_w2,
                 norm_final_w):
    d = norm_mix_w.shape[-1]
    d_inner = ssm_norm_w.shape[-1]
    conv_dim = ssm_conv_b.shape[-1]
    n_heads = dt_bias.shape[-1]
    s1, s2 = d, 2 * d
    s3 = s2 + d_inner
    s4 = s3 + conv_dim
    s5 = s4 + n_heads
    w = w_in[0]
    row = lambda v: v.reshape(1, -1).astype(F32)
    pad_lanes = lambda v: jnp.pad(v, ((0, 0), (0, LANES - v.shape[-1])))
    router_w = pad_lanes(jnp.concatenate([router_expert_w[0], router_group_w[0]], axis=1))
    router_w_hi = router_w.astype(BF16)
    return {
        "norm_w": row(norm_mix_w[0]),
        "w_vg": w[:, :s2].astype(BF16),
        "w_zx": jnp.concatenate([w[:, s2:s3].reshape(d, N_GROUPS, GROUP_W),
                                 _group_xbc_cols(w[:, s3:s4], d_inner).reshape(d, N_GROUPS, XBC_GROUP_W)],
                                axis=-1).reshape(d, -1).astype(BF16),
        "w_dt": pad_lanes(w[:, s4:s5]).astype(BF16),
        "w_gate": w[:, s5:].astype(BF16),
        "conf_dw_w": jnp.pad(conf_dw_w[0], ((0, CONF_PAD - CONF_K), (0, 0))),
        "conf_dw_b": row(conf_dw_b[0]),
        "conf_ln_g": row(conf_ln_g[0]),
        "conf_ln_b": row(conf_ln_b[0]),
        "conf_out_w": conf_out_w[0].astype(BF16),
        "conf_out_b": row(conf_out_b[0]),
        "ssm_conv_w": jnp.pad(_group_xbc_cols(ssm_conv_w[0], d_inner), ((0, SUBLANES - SSM_K), (0, 0))),
        "ssm_conv_b": row(_group_xbc_cols(ssm_conv_b[0], d_inner)),
        "dt_bias": pad_lanes(row(dt_bias[0])),
        "a_log": pad_lanes(row(a_log[0])),
        "d_skip": row(jnp.repeat(d_skip[0], HEAD_DIM)),
        "ssm_norm_w": row(ssm_norm_w[0]),
        "ssm_out_w": ssm_out_w[0].astype(BF16),
        "gate_b": row(gate_b[0]),
        "w_o": w_o[0].astype(BF16),
        "norm_ffn_w": row(norm_ffn_w[0]),
        "router_w_hi": router_w_hi,
        "router_w_lo": (router_w - router_w_hi.astype(F32)).astype(BF16),
        "router_b": pad_lanes(row(jnp.concatenate([router_expert_b[0], router_group_b[0]]))),
        "exp_w1": exp_w1[0].astype(BF16),
        "exp_w3": exp_w3[0].astype(BF16),
        "exp_w2": exp_w2[0].astype(BF16),
        "norm_final_w": row(norm_final_w),
    }


def kernel(x_prompt, x_sample, state_conf_conv, state_ssm_conv, state_ssm, meta_tokens, norm_mix_w, w_in, conf_dw_w, conf_dw_b, conf_ln_g, conf_ln_b, conf_out_w, conf_out_b, ssm_conv_w, ssm_conv_b, dt_bias, a_log, d_skip, ssm_norm_w, ssm_out_w, gate_b, w_o, norm_ffn_w, router_group_w, router_group_b, router_expert_w, router_expert_b, exp_w1, exp_w3, exp_w2, norm_final_w):
    assert norm_mix_w.shape[0] == 1, "single-layer trunk"
    b, l, d = x_prompt.shape
    nb, n_new, _ = x_sample.shape
    assert l % TILE == 0 and n_new <= SAMPLE_ROWS and (nb * SAMPLE_ROWS) % TILE == 0
    p = _prep_params(norm_mix_w, w_in, conf_dw_w, conf_dw_b, conf_ln_g, conf_ln_b, conf_out_w, conf_out_b,
                     ssm_conv_w, ssm_conv_b, dt_bias, a_log, d_skip, ssm_norm_w, ssm_out_w, gate_b, w_o,
                     norm_ffn_w, router_group_w, router_group_b, router_expert_w, router_expert_b, exp_w1, exp_w3,
                     exp_w2, norm_final_w)

    assert meta_tokens.shape[0] == N_META
    ya_p, new_conf_p = _conf_prompt(meta_tokens.astype(F32), x_prompt, p)
    d_inner = ssm_norm_w.shape[-1]
    conv_dim = ssm_conv_b.shape[-1]
    head = jnp.concatenate([jnp.zeros((TILE - N_META, d), F32), meta_tokens.astype(F32)], axis=0)
    _, meta_sconv, meta_ssm = _ssm_prompt(head[None], jnp.zeros((SCONV_PAD, conv_dim), F32),
                                          jnp.zeros(state_ssm.shape[2:], F32), p, TILE - N_META)
    meta_hist = jnp.pad(meta_sconv[0], ((SCONV_PAD - (SSM_K - 1), 0), (0, 0)))
    yb_p, new_sconv_p, new_ssm_p = _ssm_prompt(x_prompt, meta_hist, meta_ssm[0], p, 0)

    xs_rows = jnp.pad(x_sample, ((0, 0), (0, SAMPLE_ROWS - n_new), (0, 0))).reshape(nb * SAMPLE_ROWS, d)
    ya_s, new_conf_s = _conf_sample(xs_rows, state_conf_conv[0], p, n_new)
    yb_s, new_sconv_s, new_ssm_s = _ssm_sample(xs_rows, state_ssm_conv[0], state_ssm[0], p, n_new)
    unpad = lambda v: v.reshape(nb, SAMPLE_ROWS, d)[:, :n_new].reshape(nb * n_new, d)

    outs = []
    for x, ya, yb in ((x_prompt.reshape(b * l, d), ya_p.reshape(b * l, d), yb_p.reshape(b * l, d)),
                      (x_sample.reshape(nb * n_new, d), unpad(ya_s), unpad(yb_s))):
        tm = next(t for t in (1024, 512, TILE) if x.shape[0] % t == 0)
        x1, xn2, comb = _merge(x, ya, yb, p, min(tm, 512))
        outs.append(_moe(xn2, comb, x1, p, tm))
    y_prompt = outs[0].reshape(b, l, d)
    y_sample = outs[1].reshape(nb, n_new, d)
    return (y_prompt, y_sample, new_conf_p[None], _ungroup_xbc_cols(new_sconv_p, d_inner)[None], new_ssm_p[None],
            new_conf_s[None], new_sconv_s[None], new_ssm_s[None])
```

```python
import functools

import jax
import jax.numpy as jnp
from jax import lax
from jax.experimental import pallas as pl
from jax.experimental.pallas import tpu as pltpu

F32 = jnp.float32
BF16 = jnp.bfloat16
EPS = 1e-6

LANES = 128
SUBLANES = 8
TILE = 128
VMEM_LIMIT = 56 * 1024 * 1024
VMEM_LIMIT_MOE = 60 * 1024 * 1024

N_META = 16
CONF_K = 31
SSM_K = 4
HEAD_DIM = 64
N_STATE = 128
N_GROUPS = 8
HEADS_PER_GROUP = 4
GROUP_W = HEADS_PER_GROUP * HEAD_DIM
N_EXPERTS = 32
EXPERTS_PER_GROUP = 8
N_EXPERT_GROUPS = 4
SAMPLE_ROWS = 8
CONF_PAD = 32
SCONV_PAD = 8
XBC_GROUP_W = GROUP_W + 2 * N_STATE
ZX_GROUP_W = GROUP_W + XBC_GROUP_W
GSEL_LANE = 64


def _rms(x, w):
    return x * lax.rsqrt(jnp.mean(x * x, axis=-1, keepdims=True) + EPS) * w


def _sigmoid(x):
    return 0.5 * jnp.tanh(0.5 * x) + 0.5


def _silu(x):
    return x * _sigmoid(x)


def _dot(a, b):
    return jnp.dot(a, b, preferred_element_type=F32)


def _dot_nt(a, b):
    return lax.dot_general(a, b, (((1,), (1,)), ((), ())), preferred_element_type=F32)


def _dot_tn(a, b):
    return lax.dot_general(a, b, (((0,), (0,)), ((), ())), preferred_element_type=F32)


def _const_spec(shape):
    zeros = (0,) * len(shape)
    return pl.BlockSpec(shape, lambda *_: zeros, pipeline_mode=pl.Buffered(1))


def _dwconv(src_ref, base, rows, w_ref, b_ref, taps, width, emit, col0=0):
    for c in range(col0 // LANES, (col0 + width) // LANES):
        cs = slice(c * LANES, (c + 1) * LANES)
        acc = jnp.broadcast_to(b_ref[:, cs], (rows, LANES))
        for r in range(min(SUBLANES, taps)):
            qs = range((taps - r + SUBLANES - 1) // SUBLANES)
            slab = src_ref[pl.ds(base + r, rows + SUBLANES * (len(qs) - 1)), cs]
            part = w_ref[r:r + 1, cs] * slab[0:rows]
            for q in qs[1:]:
                k = SUBLANES * q + r
                part = part + w_ref[k:k + 1, cs] * slab[SUBLANES * q:SUBLANES * q + rows]
            acc = acc + part
        emit(cs, acc)


def _conf_tail(conv_ref, ln_g_ref, ln_b_ref, wout_ref, bout_ref):
    c = conv_ref[...]
    mu = jnp.mean(c, axis=-1, keepdims=True)
    d = c - mu
    var = jnp.mean(d * d, axis=-1, keepdims=True)
    y = d * lax.rsqrt(var + EPS) * ln_g_ref[...] + ln_b_ref[...]
    y = _silu(y)
    return _dot(y.astype(BF16), wout_ref[...]) + bout_ref[...]


def _conf_glu(x, nw_ref, wvg_ref):
    d = x.shape[-1]
    xn = _rms(x, nw_ref[...]).astype(BF16)
    vg = _dot(xn, wvg_ref[...])
    return vg[:, :d] * _sigmoid(vg[:, d:])


def _conf_prompt_kernel(meta_ref, x_ref, nw_ref, wvg_ref, dww_ref, dwb_ref, lng_ref, lnb_ref, wout_ref, bout_ref,
                        ya_ref, nc_ref, afull_ref, conv_ref):
    j = pl.program_id(1)
    nq, rows, d = x_ref.shape

    @pl.when(j == 0)
    def _():
        a_meta = _conf_glu(meta_ref[...], nw_ref, wvg_ref)
        for q in range(nq):
            afull_ref[q, 0:CONF_PAD - N_META, :] = jnp.zeros((CONF_PAD - N_META, d), F32)
            afull_ref[q, CONF_PAD - N_META:CONF_PAD, :] = a_meta

    a = _conf_glu(x_ref[...].reshape(nq * rows, d), nw_ref, wvg_ref)
    for q in range(nq):
        afull_ref[q, CONF_PAD:CONF_PAD + rows, :] = a[q * rows:(q + 1) * rows]
        for r0 in range(0, rows, TILE):
            def emit(cs, v, r0=q * rows + r0):
                conv_ref[r0:r0 + TILE, cs] = v

            _dwconv(afull_ref.at[q], CONF_PAD - (CONF_K - 1) + r0, TILE, dww_ref, dwb_ref, CONF_K, d, emit)
        nc_ref[q] = afull_ref[q, pl.ds(CONF_PAD + rows - (CONF_K - 1), CONF_K - 1), :]
        afull_ref[q, 0:CONF_PAD, :] = afull_ref[q, rows:rows + CONF_PAD, :]
    ya_ref[...] = _conf_tail(conv_ref, lng_ref, lnb_ref, wout_ref, bout_ref).reshape(nq, rows, d)


def _conf_sample_kernel(x_ref, st_ref, nw_ref, wvg_ref, dww_ref, dwb_ref, lng_ref, lnb_ref, wout_ref, bout_ref,
                        ya_ref, nc_ref, a_ref, cs_ref, conv_ref, *, n_new):
    d = x_ref.shape[-1]
    n_seq = TILE // SAMPLE_ROWS
    a_ref[...] = _conf_glu(x_ref[...], nw_ref, wvg_ref)

    def seq_body(s, carry):
        r0 = pl.multiple_of(s * SAMPLE_ROWS, SAMPLE_ROWS)
        cs_ref[CONF_PAD - (CONF_K - 1):CONF_PAD, :] = st_ref[s]
        cs_ref[CONF_PAD:CONF_PAD + SAMPLE_ROWS, :] = a_ref[pl.ds(r0, SAMPLE_ROWS), :]

        def emit(cs, v):
            conv_ref[pl.ds(r0, SAMPLE_ROWS), cs] = v

        _dwconv(cs_ref, CONF_PAD - (CONF_K - 1), SAMPLE_ROWS, dww_ref, dwb_ref, CONF_K, d, emit)
        nc_ref[s] = cs_ref[pl.ds(CONF_PAD + n_new - (CONF_K - 1), CONF_K - 1), :]
        return carry

    lax.fori_loop(0, n_seq, seq_body, 0)
    ya_ref[...] = _conf_tail(conv_ref, lng_ref, lnb_ref, wout_ref, bout_ref)


def _conf_weights(p):
    d = p["norm_w"].shape[-1]
    return [p["norm_w"], p["w_vg"], p["conf_dw_w"], p["conf_dw_b"], p["conf_ln_g"], p["conf_ln_b"],
            p["conf_out_w"], p["conf_out_b"]], [
        _const_spec((1, d)), _const_spec((d, 2 * d)), _const_spec((CONF_PAD, d)), _const_spec((1, d)),
        _const_spec((1, d)), _const_spec((1, d)), _const_spec((d, d)), _const_spec((1, d))]


def _conf_prompt(meta, x_prompt, p):
    b, l, d = x_prompt.shape
    rows = 2 * TILE if l % (2 * TILE) == 0 else TILE
    nq = 2 if b % 2 == 0 else 1
    weights, wspecs = _conf_weights(p)
    xmap = lambda i, j: (i, j, 0)
    return pl.pallas_call(
        _conf_prompt_kernel,
        grid=(b // nq, l // rows),
        in_specs=[_const_spec(meta.shape), pl.BlockSpec((nq, rows, d), xmap)] + wspecs,
        out_specs=[pl.BlockSpec((nq, rows, d), xmap),
                   pl.BlockSpec((nq, CONF_K - 1, d), lambda i, j: (i, 0, 0))],
        out_shape=[jax.ShapeDtypeStruct((b, l, d), F32), jax.ShapeDtypeStruct((b, CONF_K - 1, d), F32)],
        scratch_shapes=[pltpu.VMEM((nq, CONF_PAD + rows, d), F32), pltpu.VMEM((nq * rows, d), F32)],
        compiler_params=pltpu.CompilerParams(dimension_semantics=("arbitrary", "arbitrary"),
                                             vmem_limit_bytes=VMEM_LIMIT),
        name="conf_prompt",
    )(meta, x_prompt, *weights)


def _conf_sample(x_rows, state, p, n_new):
    rows, d = x_rows.shape
    n_seq = TILE // SAMPLE_ROWS
    weights, wspecs = _conf_weights(p)
    return pl.pallas_call(
        functools.partial(_conf_sample_kernel, n_new=n_new),
        grid=(rows // TILE,),
        in_specs=[pl.BlockSpec((TILE, d), lambda i: (i, 0)),
                  pl.BlockSpec((n_seq, CONF_K - 1, d), lambda i: (i, 0, 0))] + wspecs,
        out_specs=[pl.BlockSpec((TILE, d), lambda i: (i, 0)),
                   pl.BlockSpec((n_seq, CONF_K - 1, d), lambda i: (i, 0, 0))],
        out_shape=[jax.ShapeDtypeStruct((rows, d), F32),
                   jax.ShapeDtypeStruct((rows // SAMPLE_ROWS, CONF_K - 1, d), F32)],
        scratch_shapes=[pltpu.VMEM((TILE, d), F32), pltpu.VMEM((CONF_PAD + SAMPLE_ROWS, d), F32),
                        pltpu.VMEM((TILE, d), F32)],
        compiler_params=pltpu.CompilerParams(dimension_semantics=("arbitrary",), vmem_limit_bytes=VMEM_LIMIT),
        name="conf_sample",
    )(x_rows, state, *weights)


def _xbc_col_pairs(d_inner):
    pairs = []
    for g in range(N_GROUPS):
        c0 = g * XBC_GROUP_W
        pairs.append((slice(c0, c0 + GROUP_W), slice(g * GROUP_W, (g + 1) * GROUP_W)))
        for k in range(2):
            ours = c0 + GROUP_W + k * N_STATE
            theirs = d_inner + (k * N_GROUPS + g) * N_STATE
            pairs.append((slice(ours, ours + N_STATE), slice(theirs, theirs + N_STATE)))
    return pairs


def _project_group(g, xn, wzx_ref, z_ref, pre_ref, row0):
    zx = _dot(xn, wzx_ref[:, g * ZX_GROUP_W:(g + 1) * ZX_GROUP_W])
    z_ref[:, g * GROUP_W:(g + 1) * GROUP_W] = zx[:, :GROUP_W]
    pre_ref[row0:row0 + xn.shape[0], g * XBC_GROUP_W:(g + 1) * XBC_GROUP_W] = zx[:, GROUP_W:]


def _seg_cumsum(x, row, seg_len, reverse):
    pos = row % seg_len
    n = x.shape[0]
    step = 1
    while step < seg_len:
        if reverse:
            x = x + jnp.where(pos < seg_len - step, pltpu.roll(x, n - step, axis=0), 0.0)
        else:
            x = x + jnp.where(pos >= step, pltpu.roll(x, step, axis=0), 0.0)
        step *= 2
    return x


def _expand_heads(m, g, lane):
    rows = m.shape[0]
    cols = [jnp.broadcast_to(m[:, HEADS_PER_GROUP * g + i:HEADS_PER_GROUP * g + i + 1], (rows, LANES))
            for i in range(HEADS_PER_GROUP)]
    lo = jnp.where(lane < HEAD_DIM, cols[0], cols[1])
    hi = jnp.where(lane < HEAD_DIM, cols[2], cols[3])
    return jnp.concatenate([lo, hi], axis=1)


def _ssd_prelude(dtr, valid, seg_len, dtb_ref, alog_ref):
    row = lax.broadcasted_iota(jnp.int32, (TILE, TILE), 0)
    col = lax.broadcasted_iota(jnp.int32, (TILE, TILE), 1)
    same_seq = (col // seg_len) == (row // seg_len)
    causal = (col <= row) & same_seq
    xdt = dtr + dtb_ref[...]
    e = jnp.exp(-jnp.abs(xdt))
    u = 1.0 + e
    dt = (jnp.maximum(xdt, 0.0) + jnp.where(u == 1.0, e, jnp.log(u) * e / (u - 1.0))) * valid
    da = dt * (-jnp.exp(alog_ref[...]))
    d1 = da.astype(BF16)
    r1 = da - d1.astype(F32)
    d2 = r1.astype(BF16)
    d3 = (r1 - d2.astype(F32)).astype(BF16)
    lower = jnp.where(causal, 1.0, 0.0).astype(BF16)
    upper = jnp.where((col > row) & same_seq, 1.0, 0.0).astype(BF16)
    cum = _dot(lower, d1) + _dot(lower, d2) + _dot(lower, d3)
    rest = _dot(upper, d1) + _dot(upper, d2) + _dot(upper, d3)
    return {
        "causal": causal,
        "cum": cum,
        "cum_t": cum.T,
        "dt_t": dt.T,
        "ecum": jnp.exp(cum),
        "tail": jnp.exp(rest) * dt,
    }


_PRE_KEYS = ("cum", "cum_t", "dt_t", "ecum", "tail")


def _ssd_group(g, pre, z_ref, xc_ref, seg_len, n_seq, h_in, h_out, dskip_ref, normw_ref, after_first_dot=None):
    lane = lax.broadcasted_iota(jnp.int32, (TILE, LANES), 1)
    lane_g = lax.broadcasted_iota(jnp.int32, (TILE, GROUP_W), 1)
    cum, cum_t, dt_t = pre["cum"], pre["cum_t"], pre["dt_t"]
    c0 = g * XBC_GROUP_W
    gs = slice(g * GROUP_W, (g + 1) * GROUP_W)
    xs = xc_ref[:, c0:c0 + GROUP_W]
    zg = z_ref[:, gs]
    bm16 = xc_ref[:, c0 + GROUP_W:c0 + GROUP_W + N_STATE].astype(BF16)
    cm16 = xc_ref[:, c0 + GROUP_W + N_STATE:c0 + XBC_GROUP_W].astype(BF16)
    cb = _dot_nt(cm16, bm16)
    extra = after_first_dot() if after_first_dot is not None else None
    seqs = [(s, slice(s * seg_len, (s + 1) * seg_len), (s + 1) * seg_len - 1) for s in range(n_seq)]
    ystate = [_dot_nt(cm16[rs], h_in(s, g).astype(BF16)) for s, rs, _ in seqs]
    if n_seq * seg_len < TILE:
        ystate.append(jnp.zeros((TILE - n_seq * seg_len, GROUP_W), F32))
    ys = ystate[0] if len(ystate) == 1 else jnp.concatenate(ystate, axis=0)
    ws, xms = [], []
    for i in range(HEADS_PER_GROUP):
        h = HEADS_PER_GROUP * g + i
        seg = cum[:, h:h + 1] - cum_t[h:h + 1, :]
        decay = jnp.exp(jnp.where(pre["causal"], seg, -jnp.inf))
        ws.append((cb * decay * dt_t[h:h + 1, :]).astype(BF16))
        xms.append(jnp.where((lane_g // HEAD_DIM) == i, xs, 0.0).astype(BF16))
    yg = _dot(jnp.concatenate(ws, axis=1), jnp.concatenate(xms, axis=0))
    yg = yg + ys * _expand_heads(pre["ecum"], g, lane)
    xw16 = (xs * _expand_heads(pre["tail"], g, lane)).astype(BF16)

    def update_state():
        for s, rs, last in seqs:
            upd = _dot_tn(xw16[rs], bm16[rs])
            dec = jnp.concatenate(
                [jnp.broadcast_to(jnp.exp(cum_t[HEADS_PER_GROUP * g + i:HEADS_PER_GROUP * g + i + 1, last:last + 1]),
                                  (HEAD_DIM, N_STATE)) for i in range(HEADS_PER_GROUP)], axis=0)
            h_out(s, g, h_in(s, g) * dec + upd)

    v = (yg + dskip_ref[:, gs] * xs) * _silu(zg)
    v = v * lax.rsqrt(jnp.mean(v * v, axis=-1, keepdims=True) + EPS) * normw_ref[:, gs]
    return v.astype(BF16), update_state, extra


def _out_proj_group(g, v16, wout_ref):
    return _dot(v16, wout_ref[g * GROUP_W:(g + 1) * GROUP_W, :])


def _ssm_prompt_kernel(x_ref, hist_ref, h0_ref, nw_ref, wzx_ref, wdt_ref, cw_ref, cb_ref, dtb_ref, alog_ref,
                       dskip_ref, normw_ref, wout_ref, yb_ref, ns_ref, hs_ref, z_ref, cfull_ref, xc_ref, pre_ref, *,
                       n_inert):
    s = pl.program_id(1)
    n_tiles = pl.num_programs(1) - 1
    nq, _, d = x_ref.shape
    seqs = range(nq)

    @pl.when(s == 0)
    def _():
        for q in seqs:
            cfull_ref[q, 0:SCONV_PAD, :] = hist_ref[...]
            hs_ref[q] = h0_ref[...]
        z_ref[...] = jnp.zeros(z_ref.shape, F32)
        xc_ref[...] = jnp.zeros(xc_ref.shape, F32)
        pre_ref[...] = jnp.zeros(pre_ref.shape, F32)

    xn = _rms(x_ref[...].reshape(nq * TILE, d), nw_ref[...]).astype(BF16)
    row = lax.broadcasted_iota(jnp.int32, (TILE, TILE), 0)
    col = lax.broadcasted_iota(jnp.int32, (TILE, TILE), 1)
    pres = []
    for q in seqs:
        pre = {k: pre_ref[q, i] for i, k in enumerate(_PRE_KEYS)}
        pre["causal"] = col <= row
        pres.append(pre)
    rowv = lax.broadcasted_iota(jnp.int32, (TILE, 1), 0)
    valid = jnp.where((s == 0) & (rowv < n_inert), 0.0, 1.0)
    dtr = _dot(xn, wdt_ref[...])

    def state_access(q):
        def h_in(_, g):
            return hs_ref[q, HEADS_PER_GROUP * g:HEADS_PER_GROUP * (g + 1)].reshape(GROUP_W, N_STATE)

        def h_out(_, g, v):
            hs_ref[q, HEADS_PER_GROUP * g:HEADS_PER_GROUP * (g + 1)] = v.reshape(HEADS_PER_GROUP, HEAD_DIM, N_STATE)

        return h_in, h_out

    acc = [jnp.zeros((TILE, d), F32) for _ in seqs]
    pending = []

    def finish_oldest():
        v16, update, q, g = pending.pop(0)
        acc[q] = acc[q] + _out_proj_group(g, v16, wout_ref)
        update()

    for g in range(N_GROUPS):
        projected = {}
        for q in seqs:
            def independent_dots(g=g, q=q, projected=projected):
                if q == 0:
                    projected["zx"] = _dot(xn, wzx_ref[:, g * ZX_GROUP_W:(g + 1) * ZX_GROUP_W])
                zx = projected["zx"][q * TILE:(q + 1) * TILE]
                z_ref[q, :, g * GROUP_W:(g + 1) * GROUP_W] = zx[:, :GROUP_W]
                cfull_ref[q, SCONV_PAD:SCONV_PAD + TILE, g * XBC_GROUP_W:(g + 1) * XBC_GROUP_W] = zx[:, GROUP_W:]
                if len(pending) >= nq:
                    finish_oldest()

            h_in, h_out = state_access(q)
            v16, update, _ = _ssd_group(g, pres[q], z_ref.at[q], xc_ref.at[q], TILE, 1, h_in, h_out, dskip_ref,
                                        normw_ref, independent_dots)

            def emit(cs, v, q=q):
                xc_ref[q, :, cs] = _silu(v)

            _dwconv(cfull_ref.at[q], SCONV_PAD - (SSM_K - 1), TILE, cw_ref, cb_ref, SSM_K, XBC_GROUP_W, emit,
                    col0=g * XBC_GROUP_W)
            pending.append((v16, update, q, g))
    while pending:
        finish_oldest()
    for q in seqs:
        yb_ref[q] = acc[q]
        pre_next = _ssd_prelude(dtr[q * TILE:(q + 1) * TILE], valid, TILE, dtb_ref, alog_ref)
        for i, k in enumerate(_PRE_KEYS):
            pre_ref[q, i] = pre_next[k]

    @pl.when(s == n_tiles - 1)
    def _():
        for q in seqs:
            ns_ref[q] = cfull_ref[q, pl.ds(SCONV_PAD + TILE - (SSM_K - 1), SSM_K - 1), :]

    for q in seqs:
        cfull_ref[q, 0:SCONV_PAD, :] = cfull_ref[q, TILE:TILE + SCONV_PAD, :]


def _ssm_sample_proj_kernel(x_ref, st_ref, nw_ref, wzx_ref, wdt_ref, cw_ref, cb_ref,
                            z_ref, xc_ref, dtr_ref, ns_ref, xbc_ref, cs_ref, *, n_new):
    conv_dim = xc_ref.shape[-1]
    d_inner = z_ref.shape[-1]
    n_seq = TILE // SAMPLE_ROWS
    xn = _rms(x_ref[...], nw_ref[...]).astype(BF16)
    for g in range(N_GROUPS):
        _project_group(g, xn, wzx_ref, z_ref, xbc_ref, 0)
    dtr_ref[...] = _dot(xn, wdt_ref[...])
    col_pairs = _xbc_col_pairs(d_inner)

    def seq_body(s, carry):
        r0 = pl.multiple_of(s * SAMPLE_ROWS, SAMPLE_ROWS)
        hist = st_ref[s]
        for ours, theirs in col_pairs:
            cs_ref[SCONV_PAD - (SSM_K - 1):SCONV_PAD, ours] = hist[:, theirs]
        cs_ref[SCONV_PAD:SCONV_PAD + SAMPLE_ROWS, :] = xbc_ref[pl.ds(r0, SAMPLE_ROWS), :]

        def emit(cs, v):
            xc_ref[pl.ds(r0, SAMPLE_ROWS), cs] = _silu(v)

        _dwconv(cs_ref, SCONV_PAD - (SSM_K - 1), SAMPLE_ROWS, cw_ref, cb_ref, SSM_K, conv_dim, emit)
        for ours, theirs in col_pairs:
            ns_ref[s, :, theirs] = cs_ref[pl.ds(SCONV_PAD + n_new - (SSM_K - 1), SSM_K - 1), ours]
        return carry

    lax.fori_loop(0, n_seq, seq_body, 0)


def _ssm_sample_ssd_kernel(z_in, xc_in, dtr_in, hin_ref, dtb_ref, alog_ref, dskip_ref, normw_ref, wout_ref,
                           yb_ref, hout_ref, z_ref, xc_ref, dtr_ref, *, n_new, n_seq):
    rows = n_seq * SAMPLE_ROWS
    for src, dst in ((z_in, z_ref), (xc_in, xc_ref), (dtr_in, dtr_ref)):
        dst[0:rows, :] = src[...]
        dst[rows:TILE, :] = jnp.zeros((TILE - rows, dst.shape[-1]), F32)
    rowv = lax.broadcasted_iota(jnp.int32, (TILE, 1), 0)
    valid = jnp.where((rowv < rows) & ((rowv % SAMPLE_ROWS) < n_new), 1.0, 0.0)

    def h_in(s, g):
        return hin_ref[s, HEADS_PER_GROUP * g:HEADS_PER_GROUP * (g + 1)].reshape(GROUP_W, N_STATE)

    def h_out(s, g, v):
        hout_ref[s, HEADS_PER_GROUP * g:HEADS_PER_GROUP * (g + 1)] = v.reshape(HEADS_PER_GROUP, HEAD_DIM, N_STATE)

    pre = _ssd_prelude(dtr_ref[...], valid, SAMPLE_ROWS, dtb_ref, alog_ref)
    acc = jnp.zeros((TILE, yb_ref.shape[-1]), F32)
    for g in range(N_GROUPS):
        v16, update, _ = _ssd_group(g, pre, z_ref, xc_ref, SAMPLE_ROWS, n_seq, h_in, h_out, dskip_ref, normw_ref)
        acc = acc + _out_proj_group(g, v16, wout_ref)
        update()
    yb_ref[...] = acc[0:rows, :]


def _ssm_prompt(x, hist, h0, p, n_inert):
    b, l, d = x.shape
    nb = l // TILE
    nq = 2 if b % 2 == 0 else 1
    d_inner = p["ssm_norm_w"].shape[-1]
    conv_dim = p["ssm_conv_b"].shape[-1]
    n_heads = d_inner // HEAD_DIM
    xmap = lambda i, j: (i, jnp.minimum(j, nb - 1), 0)
    ymap = lambda i, j: (i, jnp.maximum(j - 1, 0), 0)
    weights = [hist, h0, p["norm_w"], p["w_zx"], p["w_dt"], p["ssm_conv_w"], p["ssm_conv_b"], p["dt_bias"],
               p["a_log"], p["d_skip"], p["ssm_norm_w"], p["ssm_out_w"]]
    wspecs = [_const_spec(w.shape) for w in weights]
    return pl.pallas_call(
        functools.partial(_ssm_prompt_kernel, n_inert=n_inert),
        grid=(b // nq, nb + 1),
        in_specs=[pl.BlockSpec((nq, TILE, d), xmap)] + wspecs,
        out_specs=[pl.BlockSpec((nq, TILE, d), ymap),
                   pl.BlockSpec((nq, SSM_K - 1, conv_dim), lambda i, j: (i, 0, 0)),
                   pl.BlockSpec((nq, n_heads, HEAD_DIM, N_STATE), lambda i, j: (i, 0, 0, 0))],
        out_shape=[jax.ShapeDtypeStruct((b, l, d), F32), jax.ShapeDtypeStruct((b, SSM_K - 1, conv_dim), F32),
                   jax.ShapeDtypeStruct((b, n_heads, HEAD_DIM, N_STATE), F32)],
        scratch_shapes=[pltpu.VMEM((nq, TILE, d_inner), F32), pltpu.VMEM((nq, SCONV_PAD + TILE, conv_dim), F32),
                        pltpu.VMEM((nq, TILE, conv_dim), F32), pltpu.VMEM((nq, len(_PRE_KEYS), TILE, LANES), F32)],
        compiler_params=pltpu.CompilerParams(dimension_semantics=("arbitrary", "arbitrary"),
                                             vmem_limit_bytes=VMEM_LIMIT),
        name="ssm_prompt",
    )(x, *weights)


def _ssm_sample(x_rows, sconv, h0, p, n_new):
    rows, d = x_rows.shape
    d_inner = p["ssm_norm_w"].shape[-1]
    conv_dim = p["ssm_conv_b"].shape[-1]
    n_heads = d_inner // HEAD_DIM
    n_seq_proj = TILE // SAMPLE_ROWS
    weights = [p["norm_w"], p["w_zx"], p["w_dt"], p["ssm_conv_w"], p["ssm_conv_b"]]
    z, xc, dtr, new_sconv = pl.pallas_call(
        functools.partial(_ssm_sample_proj_kernel, n_new=n_new),
        grid=(rows // TILE,),
        in_specs=[pl.BlockSpec((TILE, d), lambda i: (i, 0)),
                  pl.BlockSpec((n_seq_proj, SSM_K - 1, conv_dim), lambda i: (i, 0, 0))]
        + [_const_spec(w.shape) for w in weights],
        out_specs=[pl.BlockSpec((TILE, d_inner), lambda i: (i, 0)), pl.BlockSpec((TILE, conv_dim), lambda i: (i, 0)),
                   pl.BlockSpec((TILE, LANES), lambda i: (i, 0)),
                   pl.BlockSpec((n_seq_proj, SSM_K - 1, conv_dim), lambda i: (i, 0, 0))],
        out_shape=[jax.ShapeDtypeStruct((rows, d_inner), F32), jax.ShapeDtypeStruct((rows, conv_dim), F32),
                   jax.ShapeDtypeStruct((rows, LANES), F32),
                   jax.ShapeDtypeStruct((rows // SAMPLE_ROWS, SSM_K - 1, conv_dim), F32)],
        scratch_shapes=[pltpu.VMEM((TILE, conv_dim), F32), pltpu.VMEM((SCONV_PAD + SAMPLE_ROWS, conv_dim), F32)],
        compiler_params=pltpu.CompilerParams(dimension_semantics=("arbitrary",), vmem_limit_bytes=VMEM_LIMIT),
        name="ssm_sample_proj",
    )(x_rows, sconv, *weights)

    n_seq = 8 if rows % (8 * SAMPLE_ROWS) == 0 else 4
    r = n_seq * SAMPLE_ROWS
    weights = [p["dt_bias"], p["a_log"], p["d_skip"], p["ssm_norm_w"], p["ssm_out_w"]]
    yb, h_new = pl.pallas_call(
        functools.partial(_ssm_sample_ssd_kernel, n_new=n_new, n_seq=n_seq),
        grid=(rows // r,),
        in_specs=[pl.BlockSpec((r, d_inner), lambda i: (i, 0)), pl.BlockSpec((r, conv_dim), lambda i: (i, 0)),
                  pl.BlockSpec((r, LANES), lambda i: (i, 0)),
                  pl.BlockSpec((n_seq, n_heads, HEAD_DIM, N_STATE), lambda i: (i, 0, 0, 0))]
        + [_const_spec(w.shape) for w in weights],
        out_specs=[pl.BlockSpec((r, d), lambda i: (i, 0)),
                   pl.BlockSpec((n_seq, n_heads, HEAD_DIM, N_STATE), lambda i: (i, 0, 0, 0))],
        out_shape=[jax.ShapeDtypeStruct((rows, d), F32), jax.ShapeDtypeStruct(h0.shape, F32)],
        scratch_shapes=[pltpu.VMEM((TILE, d_inner), F32), pltpu.VMEM((TILE, conv_dim), F32),
                        pltpu.VMEM((TILE, LANES), F32)],
        compiler_params=pltpu.CompilerParams(dimension_semantics=("arbitrary",), vmem_limit_bytes=VMEM_LIMIT),
        name="ssm_sample_ssd",
    )(z, xc, dtr, h0, *weights)
    return yb, new_sconv, h_new


def _merge_kernel(x_ref, ya_ref, yb_ref, nw_ref, wg_ref, gb_ref, wo_ref, nfw_ref, rwh_ref, rwl_ref, rb_ref,
                  x1_ref, xn2_ref, comb_ref):
    d = x_ref.shape[-1]
    half = x_ref.shape[0] // 2
    parts = [slice(0, half), slice(half, 2 * half)]
    xs = [x_ref[rs, :] for rs in parts]
    xns = [_rms(x, nw_ref[...]).astype(BF16) for x in xs]
    glog = [_dot(xn, wg_ref[...]) for xn in xns]
    merged = []
    for rs, gl_ in zip(parts, glog):
        gates = _sigmoid(gl_ + gb_ref[...])
        merged.append((gates[:, :d] * ya_ref[rs, :] + gates[:, d:] * yb_ref[rs, :]).astype(BF16))
    outs = [_dot(m, wo_ref[...]) for m in merged]
    logit_parts = []
    for rs, x, o in zip(parts, xs, outs):
        x1 = x + o
        x1_ref[rs, :] = x1
        xn2 = _rms(x1, nfw_ref[...])
        x_hi = xn2.astype(BF16)
        xn2_ref[rs, :] = x_hi
        x_lo = (xn2 - x_hi.astype(F32)).astype(BF16)
        logit_parts.append(_dot(x_hi, rwh_ref[...]) + _dot(x_lo, rwh_ref[...]) + _dot(x_hi, rwl_ref[...]))

    logits = jnp.concatenate(logit_parts, axis=0) + rb_ref[...]
    rows = logits.shape[0]
    lane = lax.broadcasted_iota(jnp.int32, (rows, LANES), 1)
    neg = -jnp.inf
    is_g = (lane >= N_EXPERTS) & (lane < N_EXPERTS + N_EXPERT_GROUPS)
    gl = jnp.where(is_g, logits, neg)
    gmax = jnp.max(gl, axis=-1, keepdims=True)
    gsel = jnp.min(jnp.where(gl == gmax, lane, LANES), axis=-1, keepdims=True) - N_EXPERTS
    gprob = 1.0 / jnp.sum(jnp.exp(gl - gmax), axis=-1, keepdims=True)
    el = jnp.where((lane < N_EXPERTS) & ((lane // EXPERTS_PER_GROUP) == gsel), logits, neg)
    m1 = jnp.max(el, axis=-1, keepdims=True)
    i1 = jnp.min(jnp.where(el == m1, lane, LANES), axis=-1, keepdims=True)
    el2 = jnp.where(lane == i1, neg, el)
    m2 = jnp.max(el2, axis=-1, keepdims=True)
    i2 = jnp.min(jnp.where(el2 == m2, lane, LANES), axis=-1, keepdims=True)
    e2 = jnp.exp(m2 - m1)
    den = 1.0 + e2
    comb = jnp.where(lane == i1, (1.0 / den) * gprob, jnp.where(lane == i2, (e2 / den) * gprob, 0.0))
    comb_ref[...] = jnp.where(lane == GSEL_LANE, gsel.astype(F32), comb)


def _merge(x, ya, yb, p, tm):
    rows, d = x.shape
    weights = [p["norm_w"], p["w_gate"], p["gate_b"], p["w_o"], p["norm_ffn_w"], p["router_w_hi"],
               p["router_w_lo"], p["router_b"]]
    row_spec = pl.BlockSpec((tm, d), lambda i: (i, 0))
    return pl.pallas_call(
        _merge_kernel,
        grid=(rows // tm,),
        in_specs=[row_spec, row_spec, row_spec] + [_const_spec(w.shape) for w in weights],
        out_specs=[row_spec, row_spec, pl.BlockSpec((tm, LANES), lambda i: (i, 0))],
        out_shape=[jax.ShapeDtypeStruct((rows, d), F32), jax.ShapeDtypeStruct((rows, d), BF16),
                   jax.ShapeDtypeStruct((rows, LANES), F32)],
        compiler_params=pltpu.CompilerParams(dimension_semantics=("arbitrary",), vmem_limit_bytes=VMEM_LIMIT),
        name="merge_router",
    )(x, ya, yb, *weights)


def _moe_kernel(xn_ref, comb_ref, x1_ref, w1_ref, w3_ref, w2_ref, nw_ref, y_ref, rt_ref, acc_ref, *, chunk):
    g = pl.program_id(1)
    h = pl.program_id(2)
    tm = xn_ref.shape[0]
    comb = comb_ref[...]
    lane = lax.broadcasted_iota(jnp.int32, (tm, LANES), 1)
    gsel = comb[:, GSEL_LANE:GSEL_LANE + 1]

    @pl.when(g == 0)
    def _():
        row = lax.broadcasted_iota(jnp.int32, (tm, LANES), 0)
        onehot = jnp.where(lane.astype(F32) == gsel, 1.0, 0.0)
        incl = _seg_cumsum(onehot, row, tm, reverse=False)
        rank = jnp.sum(onehot * (incl - onehot), axis=-1, keepdims=True)
        packed = jnp.where(lane == 0, rank, jnp.where(lane == 1, gsel, 0.0))
        rt_ref[h, 0:SUBLANES, :] = packed.T[0:SUBLANES, :]
        rt_ref[h, SUBLANES:2 * SUBLANES, 0:LANES] = jnp.broadcast_to(incl[tm - 1:tm, :], (SUBLANES, LANES))
        acc_ref[h] = jnp.zeros(acc_ref.shape[1:], F32)

    gf = g.astype(F32)
    lane_row = lax.broadcasted_iota(jnp.int32, (1, LANES), 1)
    count = jnp.sum(jnp.where(lane_row == g, rt_ref[h, SUBLANES:SUBLANES + 1, 0:LANES], 0.0), axis=-1, keepdims=True)
    n_chunks = (count[0, 0].astype(jnp.int32) + chunk - 1) // chunk
    w_hi = jnp.where(lane < N_EXPERTS, comb, 0.0).astype(BF16)
    w_lo = (jnp.where(lane < N_EXPERTS, comb, 0.0) - w_hi.astype(F32)).astype(BF16)
    rowi = lax.broadcasted_iota(jnp.int32, (chunk, tm), 0).astype(F32)
    lane_c = lax.broadcasted_iota(jnp.int32, (chunk, LANES), 1)

    def chunk_body(c, carry):
        base = (c * chunk).astype(F32)
        sel = jnp.where((rt_ref[h, 1:2, :] == gf) & ((rt_ref[h, 0:1, :] - base) == rowi), 1.0, 0.0).astype(BF16)
        xc = _dot(sel, xn_ref[...]).astype(BF16)
        wc = _dot(sel, w_hi) + _dot(sel, w_lo)
        acts = []
        for e in range(EXPERTS_PER_GROUP):
            ce = jnp.sum(jnp.where(lane_c == g * EXPERTS_PER_GROUP + e, wc, 0.0), axis=-1, keepdims=True)
            acts.append((_silu(_dot(xc, w1_ref[e])) * _dot(xc, w3_ref[e]) * ce).astype(BF16))
        y = _dot(jnp.concatenate(acts, axis=1), w2_ref[...])
        acc_ref[h] += _dot_tn(sel, y.astype(BF16))
        return carry

    lax.fori_loop(0, n_chunks, chunk_body, 0)

    @pl.when(g == pl.num_programs(1) - 1)
    def _():
        y_ref[...] = _rms(x1_ref[...] + acc_ref[h], nw_ref[...])


def _moe(xn2, comb, x1, p, tm):
    rows, d = x1.shape
    n_e, _, d_e = p["exp_w1"].shape
    n_g = n_e // EXPERTS_PER_GROUP
    w2 = p["exp_w2"].reshape(n_g, EXPERTS_PER_GROUP * d_e, d)
    pack = 2 * SUBLANES
    chunk = -(-(tm * 9 // (8 * n_g)) // pack) * pack
    n_tiles = rows // tm
    reuse = 2 if n_tiles % 2 == 0 else 1
    tile = lambda i, g, h: (i * reuse + h, 0)
    late = lambda i, g, h: (jnp.where(g == n_g - 1, i * reuse + h, i * reuse), 0)
    wmap = lambda i, g, h: (g, 0, 0)
    return pl.pallas_call(
        functools.partial(_moe_kernel, chunk=chunk),
        grid=(n_tiles // reuse, n_g, reuse),
        in_specs=[pl.BlockSpec((tm, d), tile), pl.BlockSpec((tm, LANES), tile),
                  pl.BlockSpec((tm, d), late),
                  pl.BlockSpec((EXPERTS_PER_GROUP, d, d_e), wmap),
                  pl.BlockSpec((EXPERTS_PER_GROUP, d, d_e), wmap),
                  pl.BlockSpec((None, EXPERTS_PER_GROUP * d_e, d), wmap),
                  _const_spec((1, d))],
        out_specs=pl.BlockSpec((tm, d), late),
        out_shape=jax.ShapeDtypeStruct((rows, d), F32),
        scratch_shapes=[pltpu.VMEM((reuse, 2 * SUBLANES, tm), F32), pltpu.VMEM((reuse, tm, d), F32)],
        compiler_params=pltpu.CompilerParams(dimension_semantics=("arbitrary", "arbitrary", "arbitrary"),
                                             vmem_limit_bytes=VMEM_LIMIT_MOE),
        name="moe",
    )(xn2, comb, x1, p["exp_w1"], p["exp_w3"], w2, p["norm_final_w"])


def _group_xbc_cols(v, d_inner):
    lead = v.shape[:-1]
    xs = v[..., :d_inner].reshape(lead + (N_GROUPS, GROUP_W))
    bm = v[..., d_inner:d_inner + N_GROUPS * N_STATE].reshape(lead + (N_GROUPS, N_STATE))
    cm = v[..., d_inner + N_GROUPS * N_STATE:].reshape(lead + (N_GROUPS, N_STATE))
    return jnp.concatenate([xs, bm, cm], axis=-1).reshape(lead + (-1,))


def _ungroup_xbc_cols(v, d_inner):
    lead = v.shape[:-1]
    v = v.reshape(lead + (N_GROUPS, XBC_GROUP_W))
    parts = [v[..., :GROUP_W], v[..., GROUP_W:GROUP_W + N_STATE], v[..., GROUP_W + N_STATE:]]
    return jnp.concatenate([q.reshape(lead + (-1,)) for q in parts], axis=-1)


def _prep_params(norm_mix_w, w_in, conf_dw_w, conf_dw_b, conf_ln_g, conf_ln_b, conf_out_w, conf_out_b, ssm_conv_w,
                 ssm_conv_b, dt_bias, a_log, d_skip, ssm_norm_w, ssm_out_w, gate_b, w_o, norm_ffn_w,
                 router_group_w, router_group_b, router_expert_w, router_expert_b, exp_w1, exp_w3, exp_w2,
                 norm_final_w):
    d = norm_mix_w.shape[-1]
    d_inner = ssm_norm_w.shape[-1]
    conv_dim = ssm_conv_b.shape[-1]
    n_heads = dt_bias.shape[-1]
    s1, s2 = d, 2 * d
    s3 = s2 + d_inner
    s4 = s3 + conv_dim
    s5 = s4 + n_heads
    w = w_in[0]
    row = lambda v: v.reshape(1, -1).astype(F32)
    pad_lanes = lambda v: jnp.pad(v, ((0, 0), (0, LANES - v.shape[-1])))
    router_w = pad_lanes(jnp.concatenate([router_expert_w[0], router_group_w[0]], axis=1))
    router_w_hi = router_w.astype(BF16)
    return {
        "norm_w": row(norm_mix_w[0]),
        "w_vg": w[:, :s2].astype(BF16),
        "w_zx": jnp.concatenate([w[:, s2:s3].reshape(d, N_GROUPS, GROUP_W),
                                 _group_xbc_cols(w[:, s3:s4], d_inner).reshape(d, N_GROUPS, XBC_GROUP_W)],
                                axis=-1).reshape(d, -1).astype(BF16),
        "w_dt": pad_lanes(w[:, s4:s5]).astype(BF16),
        "w_gate": w[:, s5:].astype(BF16),
        "conf_dw_w": jnp.pad(conf_dw_w[0], ((0, CONF_PAD - CONF_K), (0, 0))),
        "conf_dw_b": row(conf_dw_b[0]),
        "conf_ln_g": row(conf_ln_g[0]),
        "conf_ln_b": row(conf_ln_b[0]),
        "conf_out_w": conf_out_w[0].astype(BF16),
        "conf_out_b": row(conf_out_b[0]),
        "ssm_conv_w": jnp.pad(_group_xbc_cols(ssm_conv_w[0], d_inner), ((0, SUBLANES - SSM_K), (0, 0))),
        "ssm_conv_b": row(_group_xbc_cols(ssm_conv_b[0], d_inner)),
        "dt_bias": pad_lanes(row(dt_bias[0])),
        "a_log": pad_lanes(row(a_log[0])),
        "d_skip": row(jnp.repeat(d_skip[0], HEAD_DIM)),
        "ssm_norm_w": row(ssm_norm_w[0]),
        "ssm_out_w": ssm_out_w[0].astype(BF16),
        "gate_b": row(gate_b[0]),
        "w_o": w_o[0].astype(BF16),
        "norm_ffn_w": row(norm_ffn_w[0]),
        "router_w_hi": router_w_hi,
        "router_w_lo": (router_w - router_w_hi.astype(F32)).astype(BF16),
        "router_b": pad_lanes(row(jnp.concatenate([router_expert_b[0], router_group_b[0]]))),
        "exp_w1": exp_w1[0].astype(BF16),
        "exp_w3": exp_w3[0].astype(BF16),
        "exp_w2": exp_w2[0].astype(BF16),
        "norm_final_w": row(norm_final_w),
    }


def kernel(x_prompt, x_sample, state_conf_conv, state_ssm_conv, state_ssm, meta_tokens, norm_mix_w, w_in, conf_dw_w, conf_dw_b, conf_ln_g, conf_ln_b, conf_out_w, conf_out_b, ssm_conv_w, ssm_conv_b, dt_bias, a_log, d_skip, ssm_norm_w, ssm_out_w, gate_b, w_o, norm_ffn_w, router_group_w, router_group_b, router_expert_w, router_expert_b, exp_w1, exp_w3, exp_w2, norm_final_w):
    assert norm_mix_w.shape[0] == 1, "single-layer trunk"
    b, l, d = x_prompt.shape
    nb, n_new, _ = x_sample.shape
    assert l % TILE == 0 and n_new <= SAMPLE_ROWS and (nb * SAMPLE_ROWS) % TILE == 0
    p = _prep_params(norm_mix_w, w_in, conf_dw_w, conf_dw_b, conf_ln_g, conf_ln_b, conf_out_w, conf_out_b,
                     ssm_conv_w, ssm_conv_b, dt_bias, a_log, d_skip, ssm_norm_w, ssm_out_w, gate_b, w_o,
                     norm_ffn_w, router_group_w, router_group_b, router_expert_w, router_expert_b, exp_w1, exp_w3,
                     exp_w2, norm_final_w)

    assert meta_tokens.shape[0] == N_META
    ya_p, new_conf_p = _conf_prompt(meta_tokens.astype(F32), x_prompt, p)
    d_inner = ssm_norm_w.shape[-1]
    conv_dim = ssm_conv_b.shape[-1]
    head = jnp.concatenate([jnp.zeros((TILE - N_META, d), F32), meta_tokens.astype(F32)], axis=0)
    _, meta_sconv, meta_ssm = _ssm_prompt(head[None], jnp.zeros((SCONV_PAD, conv_dim), F32),
                                          jnp.zeros(state_ssm.shape[2:], F32), p, TILE - N_META)
    meta_hist = jnp.pad(meta_sconv[0], ((SCONV_PAD - (SSM_K - 1), 0), (0, 0)))
    yb_p, new_sconv_p, new_ssm_p = _ssm_prompt(x_prompt, meta_hist, meta_ssm[0], p, 0)

    xs_rows = jnp.pad(x_sample, ((0, 0), (0, SAMPLE_ROWS - n_new), (0, 0))).reshape(nb * SAMPLE_ROWS, d)
    ya_s, new_conf_s = _conf_sample(xs_rows, state_conf_conv[0], p, n_new)
    yb_s, new_sconv_s, new_ssm_s = _ssm_sample(xs_rows, state_ssm_conv[0], state_ssm[0], p, n_new)
    unpad = lambda v: v.reshape(nb, SAMPLE_ROWS, d)[:, :n_new].reshape(nb * n_new, d)

    outs = []
    for x, ya, yb in ((x_prompt.reshape(b * l, d), ya_p.reshape(b * l, d), yb_p.reshape(b * l, d)),
                      (x_sample.reshape(nb * n_new, d), unpad(ya_s), unpad(yb_s))):
        tm = next(t for t in (1024, 512, TILE) if x.shape[0] % t == 0)
        x1, xn2, comb = _merge(x, ya, yb, p, min(tm, 512))
        outs.append(_moe(xn2, comb, x1, p, tm))
    y_prompt = outs[0].reshape(b, l, d)
    y_sample = outs[1].reshape(nb, n_new, d)
    return (y_prompt, y_sample, new_conf_p[None], _ungroup_xbc_cols(new_sconv_p, d_inner)[None], new_ssm_p[None],
            new_conf_s[None], new_sconv_s[None], new_ssm_s[None])
```

```python
import functools

import jax
import jax.numpy as jnp
from jax import lax
from jax.experimental import pallas as pl
from jax.experimental.pallas import tpu as pltpu

F32 = jnp.float32
BF16 = jnp.bfloat16
EPS = 1e-6

LANES = 128
SUBLANES = 8
TILE = 128
VMEM_LIMIT = 56 * 1024 * 1024

N_META = 16
CONF_K = 31
SSM_K = 4
HEAD_DIM = 64
N_STATE = 128
N_GROUPS = 8
HEADS_PER_GROUP = 4
GROUP_W = HEADS_PER_GROUP * HEAD_DIM
N_EXPERTS = 32
EXPERTS_PER_GROUP = 8
N_EXPERT_GROUPS = 4
SAMPLE_ROWS = 8
CONF_PAD = 32
SCONV_PAD = 8
XBC_GROUP_W = GROUP_W + 2 * N_STATE
ZX_GROUP_W = GROUP_W + XBC_GROUP_W
GSEL_LANE = 64


def _rms(x, w):
    return x * lax.rsqrt(jnp.mean(x * x, axis=-1, keepdims=True) + EPS) * w


def _sigmoid(x):
    return 0.5 * jnp.tanh(0.5 * x) + 0.5


def _silu(x):
    return x * _sigmoid(x)


def _dot(a, b):
    return jnp.dot(a, b, preferred_element_type=F32)


def _dot_nt(a, b):
    return lax.dot_general(a, b, (((1,), (1,)), ((), ())), preferred_element_type=F32)


def _dot_tn(a, b):
    return lax.dot_general(a, b, (((0,), (0,)), ((), ())), preferred_element_type=F32)


def _const_spec(shape):
    zeros = (0,) * len(shape)
    return pl.BlockSpec(shape, lambda *_: zeros, pipeline_mode=pl.Buffered(1))


def _dwconv(src_ref, base, rows, w_ref, b_ref, taps, width, emit, col0=0):
    for c in range(col0 // LANES, (col0 + width) // LANES):
        cs = slice(c * LANES, (c + 1) * LANES)
        acc = jnp.broadcast_to(b_ref[:, cs], (rows, LANES))
        for r in range(min(SUBLANES, taps)):
            qs = range((taps - r + SUBLANES - 1) // SUBLANES)
            slab = src_ref[pl.ds(base + r, rows + SUBLANES * (len(qs) - 1)), cs]
            part = w_ref[r:r + 1, cs] * slab[0:rows]
            for q in qs[1:]:
                k = SUBLANES * q + r
                part = part + w_ref[k:k + 1, cs] * slab[SUBLANES * q:SUBLANES * q + rows]
            acc = acc + part
        emit(cs, acc)


def _conf_tail(conv_ref, ln_g_ref, ln_b_ref, wout_ref, bout_ref):
    c = conv_ref[...]
    mu = jnp.mean(c, axis=-1, keepdims=True)
    d = c - mu
    var = jnp.mean(d * d, axis=-1, keepdims=True)
    y = d * lax.rsqrt(var + EPS) * ln_g_ref[...] + ln_b_ref[...]
    y = _silu(y)
    return _dot(y.astype(BF16), wout_ref[...]) + bout_ref[...]


def _conf_glu(x, nw_ref, wvg_ref):
    d = x.shape[-1]
    xn = _rms(x, nw_ref[...]).astype(BF16)
    vg = _dot(xn, wvg_ref[...])
    return vg[:, :d] * _sigmoid(vg[:, d:])


def _conf_prompt_kernel(meta_ref, x_ref, nw_ref, wvg_ref, dww_ref, dwb_ref, lng_ref, lnb_ref, wout_ref, bout_ref,
                        ya_ref, nc_ref, afull_ref, conv_ref):
    j = pl.program_id(1)
    nq, rows, d = x_ref.shape

    @pl.when(j == 0)
    def _():
        a_meta = _conf_glu(meta_ref[...], nw_ref, wvg_ref)
        for q in range(nq):
            afull_ref[q, 0:CONF_PAD - N_META, :] = jnp.zeros((CONF_PAD - N_META, d), F32)
            afull_ref[q, CONF_PAD - N_META:CONF_PAD, :] = a_meta

    a = _conf_glu(x_ref[...].reshape(nq * rows, d), nw_ref, wvg_ref)
    for q in range(nq):
        afull_ref[q, CONF_PAD:CONF_PAD + rows, :] = a[q * rows:(q + 1) * rows]
        for r0 in range(0, rows, TILE):
            def emit(cs, v, r0=q * rows + r0):
                conv_ref[r0:r0 + TILE, cs] = v

            _dwconv(afull_ref.at[q], CONF_PAD - (CONF_K - 1) + r0, TILE, dww_ref, dwb_ref, CONF_K, d, emit)
        nc_ref[q] = afull_ref[q, pl.ds(CONF_PAD + rows - (CONF_K - 1), CONF_K - 1), :]
        afull_ref[q, 0:CONF_PAD, :] = afull_ref[q, rows:rows + CONF_PAD, :]
    ya_ref[...] = _conf_tail(conv_ref, lng_ref, lnb_ref, wout_ref, bout_ref).reshape(nq, rows, d)


def _conf_sample_kernel(x_ref, st_ref, nw_ref, wvg_ref, dww_ref, dwb_ref, lng_ref, lnb_ref, wout_ref, bout_ref,
                        ya_ref, nc_ref, a_ref, cs_ref, conv_ref, *, n_new):
    d = x_ref.shape[-1]
    n_seq = TILE // SAMPLE_ROWS
    a_ref[...] = _conf_glu(x_ref[...], nw_ref, wvg_ref)

    def seq_body(s, carry):
        r0 = pl.multiple_of(s * SAMPLE_ROWS, SAMPLE_ROWS)
        cs_ref[CONF_PAD - (CONF_K - 1):CONF_PAD, :] = st_ref[s]
        cs_ref[CONF_PAD:CONF_PAD + SAMPLE_ROWS, :] = a_ref[pl.ds(r0, SAMPLE_ROWS), :]

        def emit(cs, v):
            conv_ref[pl.ds(r0, SAMPLE_ROWS), cs] = v

        _dwconv(cs_ref, CONF_PAD - (CONF_K - 1), SAMPLE_ROWS, dww_ref, dwb_ref, CONF_K, d, emit)
        nc_ref[s] = cs_ref[pl.ds(CONF_PAD + n_new - (CONF_K - 1), CONF_K - 1), :]
        return carry

    lax.fori_loop(0, n_seq, seq_body, 0)
    ya_ref[...] = _conf_tail(conv_ref, lng_ref, lnb_ref, wout_ref, bout_ref)


def _conf_weights(p):
    d = p["norm_w"].shape[-1]
    return [p["norm_w"], p["w_vg"], p["conf_dw_w"], p["conf_dw_b"], p["conf_ln_g"], p["conf_ln_b"],
            p["conf_out_w"], p["conf_out_b"]], [
        _const_spec((1, d)), _const_spec((d, 2 * d)), _const_spec((CONF_PAD, d)), _const_spec((1, d)),
        _const_spec((1, d)), _const_spec((1, d)), _const_spec((d, d)), _const_spec((1, d))]


def _conf_prompt(meta, x_prompt, p):
    b, l, d = x_prompt.shape
    rows = 2 * TILE if l % (2 * TILE) == 0 else TILE
    nq = 2 if b % 2 == 0 else 1
    weights, wspecs = _conf_weights(p)
    xmap = lambda i, j: (i, j, 0)
    return pl.pallas_call(
        _conf_prompt_kernel,
        grid=(b // nq, l // rows),
        in_specs=[_const_spec(meta.shape), pl.BlockSpec((nq, rows, d), xmap)] + wspecs,
        out_specs=[pl.BlockSpec((nq, rows, d), xmap),
                   pl.BlockSpec((nq, CONF_K - 1, d), lambda i, j: (i, 0, 0))],
        out_shape=[jax.ShapeDtypeStruct((b, l, d), F32), jax.ShapeDtypeStruct((b, CONF_K - 1, d), F32)],
        scratch_shapes=[pltpu.VMEM((nq, CONF_PAD + rows, d), F32), pltpu.VMEM((nq * rows, d), F32)],
        compiler_params=pltpu.CompilerParams(dimension_semantics=("arbitrary", "arbitrary"),
                                             vmem_limit_bytes=VMEM_LIMIT),
        name="conf_prompt",
    )(meta, x_prompt, *weights)


def _conf_sample(x_rows, state, p, n_new):
    rows, d = x_rows.shape
    n_seq = TILE // SAMPLE_ROWS
    weights, wspecs = _conf_weights(p)
    return pl.pallas_call(
        functools.partial(_conf_sample_kernel, n_new=n_new),
        grid=(rows // TILE,),
        in_specs=[pl.BlockSpec((TILE, d), lambda i: (i, 0)),
                  pl.BlockSpec((n_seq, CONF_K - 1, d), lambda i: (i, 0, 0))] + wspecs,
        out_specs=[pl.BlockSpec((TILE, d), lambda i: (i, 0)),
                   pl.BlockSpec((n_seq, CONF_K - 1, d), lambda i: (i, 0, 0))],
        out_shape=[jax.ShapeDtypeStruct((rows, d), F32),
                   jax.ShapeDtypeStruct((rows // SAMPLE_ROWS, CONF_K - 1, d), F32)],
        scratch_shapes=[pltpu.VMEM((TILE, d), F32), pltpu.VMEM((CONF_PAD + SAMPLE_ROWS, d), F32),
                        pltpu.VMEM((TILE, d), F32)],
        compiler_params=pltpu.CompilerParams(dimension_semantics=("arbitrary",), vmem_limit_bytes=VMEM_LIMIT),
        name="conf_sample",
    )(x_rows, state, *weights)


def _xbc_col_pairs(d_inner):
    pairs = []
    for g in range(N_GROUPS):
        c0 = g * XBC_GROUP_W
        pairs.append((slice(c0, c0 + GROUP_W), slice(g * GROUP_W, (g + 1) * GROUP_W)))
        for k in range(2):
            ours = c0 + GROUP_W + k * N_STATE
            theirs = d_inner + (k * N_GROUPS + g) * N_STATE
            pairs.append((slice(ours, ours + N_STATE), slice(theirs, theirs + N_STATE)))
    return pairs


def _project_group(g, xn, wzx_ref, z_ref, pre_ref, row0):
    zx = _dot(xn, wzx_ref[:, g * ZX_GROUP_W:(g + 1) * ZX_GROUP_W])
    z_ref[:, g * GROUP_W:(g + 1) * GROUP_W] = zx[:, :GROUP_W]
    pre_ref[row0:row0 + xn.shape[0], g * XBC_GROUP_W:(g + 1) * XBC_GROUP_W] = zx[:, GROUP_W:]


def _cumsum_rows(x, row):
    step = 1
    while step < x.shape[0]:
        x = x + jnp.where(row >= step, pltpu.roll(x, step, axis=0), 0.0)
        step *= 2
    return x


def _expand_heads(m, g, lane):
    rows = m.shape[0]
    cols = [jnp.broadcast_to(m[:, HEADS_PER_GROUP * g + i:HEADS_PER_GROUP * g + i + 1], (rows, LANES))
            for i in range(HEADS_PER_GROUP)]
    lo = jnp.where(lane < HEAD_DIM, cols[0], cols[1])
    hi = jnp.where(lane < HEAD_DIM, cols[2], cols[3])
    return jnp.concatenate([lo, hi], axis=1)


def _ssd_prelude(dtr, valid, seg_len, dtb_ref, alog_ref):
    row = lax.broadcasted_iota(jnp.int32, (TILE, TILE), 0)
    col = lax.broadcasted_iota(jnp.int32, (TILE, TILE), 1)
    same_seq = (col // seg_len) == (row // seg_len)
    causal = (col <= row) & same_seq
    xdt = dtr + dtb_ref[...]
    e = jnp.exp(-jnp.abs(xdt))
    u = 1.0 + e
    dt = (jnp.maximum(xdt, 0.0) + jnp.where(u == 1.0, e, jnp.log(u) * e / (u - 1.0))) * valid
    da = dt * (-jnp.exp(alog_ref[...]))
    d1 = da.astype(BF16)
    r1 = da - d1.astype(F32)
    d2 = r1.astype(BF16)
    d3 = (r1 - d2.astype(F32)).astype(BF16)
    lower = jnp.where(causal, 1.0, 0.0).astype(BF16)
    upper = jnp.where((col > row) & same_seq, 1.0, 0.0).astype(BF16)
    cum = _dot(lower, d1) + _dot(lower, d2) + _dot(lower, d3)
    rest = _dot(upper, d1) + _dot(upper, d2) + _dot(upper, d3)
    return {
        "causal": causal,
        "cum": cum,
        "cum_t": cum.T,
        "dt_t": dt.T,
        "ecum": jnp.exp(cum),
        "tail": jnp.exp(rest) * dt,
    }


_PRE_KEYS = ("cum", "cum_t", "dt_t", "ecum", "tail")


def _ssd_group(g, pre, z_ref, xc_ref, seg_len, n_seq, h_in, h_out, dskip_ref, normw_ref, after_first_dot=None):
    lane = lax.broadcasted_iota(jnp.int32, (TILE, LANES), 1)
    lane_g = lax.broadcasted_iota(jnp.int32, (TILE, GROUP_W), 1)
    cum, cum_t, dt_t = pre["cum"], pre["cum_t"], pre["dt_t"]
    c0 = g * XBC_GROUP_W
    gs = slice(g * GROUP_W, (g + 1) * GROUP_W)
    xs = xc_ref[:, c0:c0 + GROUP_W]
    zg = z_ref[:, gs]
    bm16 = xc_ref[:, c0 + GROUP_W:c0 + GROUP_W + N_STATE].astype(BF16)
    cm16 = xc_ref[:, c0 + GROUP_W + N_STATE:c0 + XBC_GROUP_W].astype(BF16)
    cb = _dot_nt(cm16, bm16)
    extra = after_first_dot() if after_first_dot is not None else None
    seqs = [(s, slice(s * seg_len, (s + 1) * seg_len), (s + 1) * seg_len - 1) for s in range(n_seq)]
    ystate = [_dot_nt(cm16[rs], h_in(s, g).astype(BF16)) for s, rs, _ in seqs]
    if n_seq * seg_len < TILE:
        ystate.append(jnp.zeros((TILE - n_seq * seg_len, GROUP_W), F32))
    ys = ystate[0] if len(ystate) == 1 else jnp.concatenate(ystate, axis=0)
    ws, xms = [], []
    for i in range(HEADS_PER_GROUP):
        h = HEADS_PER_GROUP * g + i
        seg = cum[:, h:h + 1] - cum_t[h:h + 1, :]
        decay = jnp.exp(jnp.where(pre["causal"], seg, -jnp.inf))
        ws.append((cb * decay * dt_t[h:h + 1, :]).astype(BF16))
        xms.append(jnp.where((lane_g // HEAD_DIM) == i, xs, 0.0).astype(BF16))
    yg = _dot(jnp.concatenate(ws, axis=1), jnp.concatenate(xms, axis=0))
    yg = yg + ys * _expand_heads(pre["ecum"], g, lane)
    xw16 = (xs * _expand_heads(pre["tail"], g, lane)).astype(BF16)

    def update_state():
        for s, rs, last in seqs:
            upd = _dot_tn(xw16[rs], bm16[rs])
            dec = jnp.concatenate(
                [jnp.broadcast_to(jnp.exp(cum_t[HEADS_PER_GROUP * g + i:HEADS_PER_GROUP * g + i + 1, last:last + 1]),
                                  (HEAD_DIM, N_STATE)) for i in range(HEADS_PER_GROUP)], axis=0)
            h_out(s, g, h_in(s, g) * dec + upd)

    v = (yg + dskip_ref[:, gs] * xs) * _silu(zg)
    v = v * lax.rsqrt(jnp.mean(v * v, axis=-1, keepdims=True) + EPS) * normw_ref[:, gs]
    return v.astype(BF16), update_state, extra


def _out_proj_group(g, v16, wout_ref):
    return _dot(v16, wout_ref[g * GROUP_W:(g + 1) * GROUP_W, :])


def _ssm_prompt_kernel(x_ref, hist_ref, h0_ref, nw_ref, wzx_ref, wdt_ref, cw_ref, cb_ref, dtb_ref, alog_ref,
                       dskip_ref, normw_ref, wout_ref, yb_ref, ns_ref, hs_ref, z_ref, cfull_ref, xc_ref, pre_ref, *,
                       n_inert):
    s = pl.program_id(1)
    n_tiles = pl.num_programs(1) - 1
    nq, _, d = x_ref.shape
    seqs = range(nq)

    @pl.when(s == 0)
    def _():
        for q in seqs:
            cfull_ref[q, 0:SCONV_PAD, :] = hist_ref[...]
            hs_ref[q] = h0_ref[...]
        z_ref[...] = jnp.zeros(z_ref.shape, F32)
        xc_ref[...] = jnp.zeros(xc_ref.shape, F32)
        pre_ref[...] = jnp.zeros(pre_ref.shape, F32)

    xn = _rms(x_ref[...].reshape(nq * TILE, d), nw_ref[...]).astype(BF16)
    row = lax.broadcasted_iota(jnp.int32, (TILE, TILE), 0)
    col = lax.broadcasted_iota(jnp.int32, (TILE, TILE), 1)
    pres = []
    for q in seqs:
        pre = {k: pre_ref[q, i] for i, k in enumerate(_PRE_KEYS)}
        pre["causal"] = col <= row
        pres.append(pre)
    rowv = lax.broadcasted_iota(jnp.int32, (TILE, 1), 0)
    valid = jnp.where((s == 0) & (rowv < n_inert), 0.0, 1.0)
    dtr = _dot(xn, wdt_ref[...])

    def state_access(q):
        def h_in(_, g):
            return hs_ref[q, HEADS_PER_GROUP * g:HEADS_PER_GROUP * (g + 1)].reshape(GROUP_W, N_STATE)

        def h_out(_, g, v):
            hs_ref[q, HEADS_PER_GROUP * g:HEADS_PER_GROUP * (g + 1)] = v.reshape(HEADS_PER_GROUP, HEAD_DIM, N_STATE)

        return h_in, h_out

    acc = [jnp.zeros((TILE, d), F32) for _ in seqs]
    pending = []

    def finish_oldest():
        v16, update, q, g = pending.pop(0)
        acc[q] = acc[q] + _out_proj_group(g, v16, wout_ref)
        update()

    for g in range(N_GROUPS):
        projected = {}
        for q in seqs:
            def independent_dots(g=g, q=q, projected=projected):
                if q == 0:
                    projected["zx"] = _dot(xn, wzx_ref[:, g * ZX_GROUP_W:(g + 1) * ZX_GROUP_W])
                zx = projected["zx"][q * TILE:(q + 1) * TILE]
                z_ref[q, :, g * GROUP_W:(g + 1) * GROUP_W] = zx[:, :GROUP_W]
                cfull_ref[q, SCONV_PAD:SCONV_PAD + TILE, g * XBC_GROUP_W:(g + 1) * XBC_GROUP_W] = zx[:, GROUP_W:]
                if len(pending) >= nq:
                    finish_oldest()

            h_in, h_out = state_access(q)
            v16, update, _ = _ssd_group(g, pres[q], z_ref.at[q], xc_ref.at[q], TILE, 1, h_in, h_out, dskip_ref,
                                        normw_ref, independent_dots)

            def emit(cs, v, q=q):
                xc_ref[q, :, cs] = _silu(v)

            _dwconv(cfull_ref.at[q], SCONV_PAD - (SSM_K - 1), TILE, cw_ref, cb_ref, SSM_K, XBC_GROUP_W, emit,
                    col0=g * XBC_GROUP_W)
            pending.append((v16, update, q, g))
    while pending:
        finish_oldest()
    for q in seqs:
        yb_ref[q] = acc[q]
        pre_next = _ssd_prelude(dtr[q * TILE:(q + 1) * TILE], valid, TILE, dtb_ref, alog_ref)
        for i, k in enumerate(_PRE_KEYS):
            pre_ref[q, i] = pre_next[k]

    @pl.when(s == n_tiles - 1)
    def _():
        for q in seqs:
            ns_ref[q] = cfull_ref[q, pl.ds(SCONV_PAD + TILE - (SSM_K - 1), SSM_K - 1), :]

    for q in seqs:
        cfull_ref[q, 0:SCONV_PAD, :] = cfull_ref[q, TILE:TILE + SCONV_PAD, :]


def _ssm_sample_proj_kernel(x_ref, st_ref, nw_ref, wzx_ref, wdt_ref, cw_ref, cb_ref,
                            z_ref, xc_ref, dtr_ref, ns_ref, xbc_ref, cs_ref, *, n_new):
    conv_dim = xc_ref.shape[-1]
    d_inner = z_ref.shape[-1]
    n_seq = TILE // SAMPLE_ROWS
    xn = _rms(x_ref[...], nw_ref[...]).astype(BF16)
    for g in range(N_GROUPS):
        _project_group(g, xn, wzx_ref, z_ref, xbc_ref, 0)
    dtr_ref[...] = _dot(xn, wdt_ref[...])
    col_pairs = _xbc_col_pairs(d_inner)

    def seq_body(s, carry):
        r0 = pl.multiple_of(s * SAMPLE_ROWS, SAMPLE_ROWS)
        hist = st_ref[s]
        for ours, theirs in col_pairs:
            cs_ref[SCONV_PAD - (SSM_K - 1):SCONV_PAD, ours] = hist[:, theirs]
        cs_ref[SCONV_PAD:SCONV_PAD + SAMPLE_ROWS, :] = xbc_ref[pl.ds(r0, SAMPLE_ROWS), :]

        def emit(cs, v):
            xc_ref[pl.ds(r0, SAMPLE_ROWS), cs] = _silu(v)

        _dwconv(cs_ref, SCONV_PAD - (SSM_K - 1), SAMPLE_ROWS, cw_ref, cb_ref, SSM_K, conv_dim, emit)
        for ours, theirs in col_pairs:
            ns_ref[s, :, theirs] = cs_ref[pl.ds(SCONV_PAD + n_new - (SSM_K - 1), SSM_K - 1), ours]
        return carry

    lax.fori_loop(0, n_seq, seq_body, 0)


def _ssm_sample_ssd_kernel(z_in, xc_in, dtr_in, hin_ref, dtb_ref, alog_ref, dskip_ref, normw_ref, wout_ref,
                           yb_ref, hout_ref, z_ref, xc_ref, dtr_ref, *, n_new, n_seq):
    rows = n_seq * SAMPLE_ROWS
    for src, dst in ((z_in, z_ref), (xc_in, xc_ref), (dtr_in, dtr_ref)):
        dst[0:rows, :] = src[...]
        dst[rows:TILE, :] = jnp.zeros((TILE - rows, dst.shape[-1]), F32)
    rowv = lax.broadcasted_iota(jnp.int32, (TILE, 1), 0)
    valid = jnp.where((rowv < rows) & ((rowv % SAMPLE_ROWS) < n_new), 1.0, 0.0)

    def h_in(s, g):
        return hin_ref[s, HEADS_PER_GROUP * g:HEADS_PER_GROUP * (g + 1)].reshape(GROUP_W, N_STATE)

    def h_out(s, g, v):
        hout_ref[s, HEADS_PER_GROUP * g:HEADS_PER_GROUP * (g + 1)] = v.reshape(HEADS_PER_GROUP, HEAD_DIM, N_STATE)

    pre = _ssd_prelude(dtr_ref[...], valid, SAMPLE_ROWS, dtb_ref, alog_ref)
    acc = jnp.zeros((TILE, yb_ref.shape[-1]), F32)
    for g in range(N_GROUPS):
        v16, update, _ = _ssd_group(g, pre, z_ref, xc_ref, SAMPLE_ROWS, n_seq, h_in, h_out, dskip_ref, normw_ref)
        acc = acc + _out_proj_group(g, v16, wout_ref)
        update()
    yb_ref[...] = acc[0:rows, :]


def _ssm_prompt(x, hist, h0, p, n_inert):
    b, l, d = x.shape
    nb = l // TILE
    nq = 2 if b % 2 == 0 else 1
    d_inner = p["ssm_norm_w"].shape[-1]
    conv_dim = p["ssm_conv_b"].shape[-1]
    n_heads = d_inner // HEAD_DIM
    xmap = lambda i, j: (i, jnp.minimum(j, nb - 1), 0)
    ymap = lambda i, j: (i, jnp.maximum(j - 1, 0), 0)
    weights = [hist, h0, p["norm_w"], p["w_zx"], p["w_dt"], p["ssm_conv_w"], p["ssm_conv_b"], p["dt_bias"],
               p["a_log"], p["d_skip"], p["ssm_norm_w"], p["ssm_out_w"]]
    wspecs = [_const_spec(w.shape) for w in weights]
    return pl.pallas_call(
        functools.partial(_ssm_prompt_kernel, n_inert=n_inert),
        grid=(b // nq, nb + 1),
        in_specs=[pl.BlockSpec((nq, TILE, d), xmap)] + wspecs,
        out_specs=[pl.BlockSpec((nq, TILE, d), ymap),
                   pl.BlockSpec((nq, SSM_K - 1, conv_dim), lambda i, j: (i, 0, 0)),
                   pl.BlockSpec((nq, n_heads, HEAD_DIM, N_STATE), lambda i, j: (i, 0, 0, 0))],
        out_shape=[jax.ShapeDtypeStruct((b, l, d), F32), jax.ShapeDtypeStruct((b, SSM_K - 1, conv_dim), F32),
                   jax.ShapeDtypeStruct((b, n_heads, HEAD_DIM, N_STATE), F32)],
        scratch_shapes=[pltpu.VMEM((nq, TILE, d_inner), F32), pltpu.VMEM((nq, SCONV_PAD + TILE, conv_dim), F32),
                        pltpu.VMEM((nq, TILE, conv_dim), F32), pltpu.VMEM((nq, len(_PRE_KEYS), TILE, LANES), F32)],
        compiler_params=pltpu.CompilerParams(dimension_semantics=("arbitrary", "arbitrary"),
                                             vmem_limit_bytes=VMEM_LIMIT),
        name="ssm_prompt",
    )(x, *weights)


def _ssm_sample(x_rows, sconv, h0, p, n_new):
    rows, d = x_rows.shape
    d_inner = p["ssm_norm_w"].shape[-1]
    conv_dim = p["ssm_conv_b"].shape[-1]
    n_heads = d_inner // HEAD_DIM
    n_seq_proj = TILE // SAMPLE_ROWS
    weights = [p["norm_w"], p["w_zx"], p["w_dt"], p["ssm_conv_w"], p["ssm_conv_b"]]
    z, xc, dtr, new_sconv = pl.pallas_call(
        functools.partial(_ssm_sample_proj_kernel, n_new=n_new),
        grid=(rows // TILE,),
        in_specs=[pl.BlockSpec((TILE, d), lambda i: (i, 0)),
                  pl.BlockSpec((n_seq_proj, SSM_K - 1, conv_dim), lambda i: (i, 0, 0))]
        + [_const_spec(w.shape) for w in weights],
        out_specs=[pl.BlockSpec((TILE, d_inner), lambda i: (i, 0)), pl.BlockSpec((TILE, conv_dim), lambda i: (i, 0)),
                   pl.BlockSpec((TILE, LANES), lambda i: (i, 0)),
                   pl.BlockSpec((n_seq_proj, SSM_K - 1, conv_dim), lambda i: (i, 0, 0))],
        out_shape=[jax.ShapeDtypeStruct((rows, d_inner), F32), jax.ShapeDtypeStruct((rows, conv_dim), F32),
                   jax.ShapeDtypeStruct((rows, LANES), F32),
                   jax.ShapeDtypeStruct((rows // SAMPLE_ROWS, SSM_K - 1, conv_dim), F32)],
        scratch_shapes=[pltpu.VMEM((TILE, conv_dim), F32), pltpu.VMEM((SCONV_PAD + SAMPLE_ROWS, conv_dim), F32)],
        compiler_params=pltpu.CompilerParams(dimension_semantics=("arbitrary",), vmem_limit_bytes=VMEM_LIMIT),
        name="ssm_sample_proj",
    )(x_rows, sconv, *weights)

    n_seq = 8 if rows % (8 * SAMPLE_ROWS) == 0 else 4
    r = n_seq * SAMPLE_ROWS
    weights = [p["dt_bias"], p["a_log"], p["d_skip"], p["ssm_norm_w"], p["ssm_out_w"]]
    yb, h_new = pl.pallas_call(
        functools.partial(_ssm_sample_ssd_kernel, n_new=n_new, n_seq=n_seq),
        grid=(rows // r,),
        in_specs=[pl.BlockSpec((r, d_inner), lambda i: (i, 0)), pl.BlockSpec((r, conv_dim), lambda i: (i, 0)),
                  pl.BlockSpec((r, LANES), lambda i: (i, 0)),
                  pl.BlockSpec((n_seq, n_heads, HEAD_DIM, N_STATE), lambda i: (i, 0, 0, 0))]
        + [_const_spec(w.shape) for w in weights],
        out_specs=[pl.BlockSpec((r, d), lambda i: (i, 0)),
                   pl.BlockSpec((n_seq, n_heads, HEAD_DIM, N_STATE), lambda i: (i, 0, 0, 0))],
        out_shape=[jax.ShapeDtypeStruct((rows, d), F32), jax.ShapeDtypeStruct(h0.shape, F32)],
        scratch_shapes=[pltpu.VMEM((TILE, d_inner), F32), pltpu.VMEM((TILE, conv_dim), F32),
                        pltpu.VMEM((TILE, LANES), F32)],
        compiler_params=pltpu.CompilerParams(dimension_semantics=("arbitrary",), vmem_limit_bytes=VMEM_LIMIT),
        name="ssm_sample_ssd",
    )(z, xc, dtr, h0, *weights)
    return yb, new_sconv, h_new


def _merge_kernel(x_ref, ya_ref, yb_ref, nw_ref, wg_ref, gb_ref, wo_ref, nfw_ref, rwh_ref, rwl_ref, rb_ref,
                  x1_ref, xn2_ref, comb_ref):
    d = x_ref.shape[-1]
    half = x_ref.shape[0] // 2
    parts = [slice(0, half), slice(half, 2 * half)]
    xs = [x_ref[rs, :] for rs in parts]
    xns = [_rms(x, nw_ref[...]).astype(BF16) for x in xs]
    glog = [_dot(xn, wg_ref[...]) for xn in xns]
    merged = []
    for rs, gl_ in zip(parts, glog):
        gates = _sigmoid(gl_ + gb_ref[...])
        merged.append((gates[:, :d] * ya_ref[rs, :] + gates[:, d:] * yb_ref[rs, :]).astype(BF16))
    outs = [_dot(m, wo_ref[...]) for m in merged]
    logit_parts = []
    for rs, x, o in zip(parts, xs, outs):
        x1 = x + o
        x1_ref[rs, :] = x1
        xn2 = _rms(x1, nfw_ref[...])
        x_hi = xn2.astype(BF16)
        xn2_ref[rs, :] = x_hi
        x_lo = (xn2 - x_hi.astype(F32)).astype(BF16)
        logit_parts.append(_dot(x_hi, rwh_ref[...]) + _dot(x_lo, rwh_ref[...]) + _dot(x_hi, rwl_ref[...]))

    logits = jnp.concatenate(logit_parts, axis=0) + rb_ref[...]
    rows = logits.shape[0]
    lane = lax.broadcasted_iota(jnp.int32, (rows, LANES), 1)
    neg = -jnp.inf
    is_g = (lane >= N_EXPERTS) & (lane < N_EXPERTS + N_EXPERT_GROUPS)
    gl = jnp.where(is_g, logits, neg)
    gmax = jnp.max(gl, axis=-1, keepdims=True)
    gsel = jnp.min(jnp.where(gl == gmax, lane, LANES), axis=-1, keepdims=True) - N_EXPERTS
    gprob = 1.0 / jnp.sum(jnp.exp(gl - gmax), axis=-1, keepdims=True)
    el = jnp.where((lane < N_EXPERTS) & ((lane // EXPERTS_PER_GROUP) == gsel), logits, neg)
    m1 = jnp.max(el, axis=-1, keepdims=True)
    i1 = jnp.min(jnp.where(el == m1, lane, LANES), axis=-1, keepdims=True)
    el2 = jnp.where(lane == i1, neg, el)
    m2 = jnp.max(el2, axis=-1, keepdims=True)
    i2 = jnp.min(jnp.where(el2 == m2, lane, LANES), axis=-1, keepdims=True)
    e2 = jnp.exp(m2 - m1)
    den = 1.0 + e2
    comb = jnp.where(lane == i1, (1.0 / den) * gprob, jnp.where(lane == i2, (e2 / den) * gprob, 0.0))
    comb_ref[...] = jnp.where(lane == GSEL_LANE, gsel.astype(F32), comb)


def _merge(x, ya, yb, p, tm):
    rows, d = x.shape
    weights = [p["norm_w"], p["w_gate"], p["gate_b"], p["w_o"], p["norm_ffn_w"], p["router_w_hi"],
               p["router_w_lo"], p["router_b"]]
    row_spec = pl.BlockSpec((tm, d), lambda i: (i, 0))
    return pl.pallas_call(
        _merge_kernel,
        grid=(rows // tm,),
        in_specs=[row_spec, row_spec, row_spec] + [_const_spec(w.shape) for w in weights],
        out_specs=[row_spec, row_spec, pl.BlockSpec((tm, LANES), lambda i: (i, 0))],
        out_shape=[jax.ShapeDtypeStruct((rows, d), F32), jax.ShapeDtypeStruct((rows, d), BF16),
                   jax.ShapeDtypeStruct((rows, LANES), F32)],
        compiler_params=pltpu.CompilerParams(dimension_semantics=("arbitrary",), vmem_limit_bytes=VMEM_LIMIT),
        name="merge_router",
    )(x, ya, yb, *weights)


def _moe_kernel(xn_ref, comb_ref, x1_ref, w1_ref, w3_ref, w2_ref, nw_ref, y_ref, rt_ref, *, chunk):
    g = pl.program_id(1)
    tm = xn_ref.shape[0]
    comb = comb_ref[...]
    lane = lax.broadcasted_iota(jnp.int32, (tm, LANES), 1)
    gsel = comb[:, GSEL_LANE:GSEL_LANE + 1]

    @pl.when(g == 0)
    def _():
        row = lax.broadcasted_iota(jnp.int32, (tm, LANES), 0)
        onehot = jnp.where(lane.astype(F32) == gsel, 1.0, 0.0)
        incl = _cumsum_rows(onehot, row)
        rank = jnp.sum(onehot * (incl - onehot), axis=-1, keepdims=True)
        packed = jnp.where(lane == 0, rank, jnp.where(lane == 1, gsel, 0.0))
        rt_ref[0:SUBLANES, :] = packed.T[0:SUBLANES, :]
        rt_ref[SUBLANES:2 * SUBLANES, 0:LANES] = jnp.broadcast_to(incl[tm - 1:tm, :], (SUBLANES, LANES))
        y_ref[...] = jnp.zeros(y_ref.shape, F32)

    gf = g.astype(F32)
    lane_row = lax.broadcasted_iota(jnp.int32, (1, LANES), 1)
    count = jnp.sum(jnp.where(lane_row == g, rt_ref[SUBLANES:SUBLANES + 1, 0:LANES], 0.0), axis=-1, keepdims=True)
    n_chunks = (count[0, 0].astype(jnp.int32) + chunk - 1) // chunk
    w_hi = jnp.where(lane < N_EXPERTS, comb, 0.0).astype(BF16)
    w_lo = (jnp.where(lane < N_EXPERTS, comb, 0.0) - w_hi.astype(F32)).astype(BF16)
    rowi = lax.broadcasted_iota(jnp.int32, (chunk, tm), 0).astype(F32)
    lane_c = lax.broadcasted_iota(jnp.int32, (chunk, LANES), 1)

    def chunk_body(c, carry):
        base = (c * chunk).astype(F32)
        sel = jnp.where((rt_ref[1:2, :] == gf) & ((rt_ref[0:1, :] - base) == rowi), 1.0, 0.0).astype(BF16)
        xc = _dot(sel, xn_ref[...]).astype(BF16)
        wc = _dot(sel, w_hi) + _dot(sel, w_lo)
        acts = []
        for e in range(EXPERTS_PER_GROUP):
            ce = jnp.sum(jnp.where(lane_c == g * EXPERTS_PER_GROUP + e, wc, 0.0), axis=-1, keepdims=True)
            acts.append((_silu(_dot(xc, w1_ref[e])) * _dot(xc, w3_ref[e]) * ce).astype(BF16))
        y = _dot(jnp.concatenate(acts, axis=1), w2_ref[...])
        y_ref[...] += _dot_tn(sel, y.astype(BF16))
        return carry

    lax.fori_loop(0, n_chunks, chunk_body, 0)

    @pl.when(g == pl.num_programs(1) - 1)
    def _():
        y_ref[...] = _rms(x1_ref[...] + y_ref[...], nw_ref[...])


def _moe(xn2, comb, x1, p, tm):
    rows, d = x1.shape
    n_e, _, d_e = p["exp_w1"].shape
    n_g = n_e // EXPERTS_PER_GROUP
    w2 = p["exp_w2"].reshape(n_g, EXPERTS_PER_GROUP * d_e, d)
    pack = 2 * SUBLANES
    chunk = -(-(tm * 9 // (8 * n_g)) // pack) * pack
    row = lambda i, g: (i, 0)
    return pl.pallas_call(
        functools.partial(_moe_kernel, chunk=chunk),
        grid=(rows // tm, n_g),
        in_specs=[pl.BlockSpec((tm, d), row), pl.BlockSpec((tm, LANES), row),
                  pl.BlockSpec((tm, d), row),
                  pl.BlockSpec((EXPERTS_PER_GROUP, d, d_e), lambda i, g: (g, 0, 0)),
                  pl.BlockSpec((EXPERTS_PER_GROUP, d, d_e), lambda i, g: (g, 0, 0)),
                  pl.BlockSpec((None, EXPERTS_PER_GROUP * d_e, d), lambda i, g: (g, 0, 0)),
                  _const_spec((1, d))],
        out_specs=pl.BlockSpec((tm, d), row),
        out_shape=jax.ShapeDtypeStruct((rows, d), F32),
        scratch_shapes=[pltpu.VMEM((2 * SUBLANES, tm), F32)],
        compiler_params=pltpu.CompilerParams(dimension_semantics=("arbitrary", "arbitrary"),
                                             vmem_limit_bytes=VMEM_LIMIT),
        name="moe",
    )(xn2, comb, x1, p["exp_w1"], p["exp_w3"], w2, p["norm_final_w"])


def _group_xbc_cols(v, d_inner):
    lead = v.shape[:-1]
    xs = v[..., :d_inner].reshape(lead + (N_GROUPS, GROUP_W))
    bm = v[..., d_inner:d_inner + N_GROUPS * N_STATE].reshape(lead + (N_GROUPS, N_STATE))
    cm = v[..., d_inner + N_GROUPS * N_STATE:].reshape(lead + (N_GROUPS, N_STATE))
    return jnp.concatenate([xs, bm, cm], axis=-1).reshape(lead + (-1,))


def _ungroup_xbc_cols(v, d_inner):
    lead = v.shape[:-1]
    v = v.reshape(lead + (N_GROUPS, XBC_GROUP_W))
    parts = [v[..., :GROUP_W], v[..., GROUP_W:GROUP_W + N_STATE], v[..., GROUP_W + N_STATE:]]
    return jnp.concatenate([q.reshape(lead + (-1,)) for q in parts], axis=-1)


def _prep_params(norm_mix_w, w_in, conf_dw_w, conf_dw_b, conf_ln_g, conf_ln_b, conf_out_w, conf_out_b, ssm_conv_w,
                 ssm_conv_b, dt_bias, a_log, d_skip, ssm_norm_w, ssm_out_w, gate_b, w_o, norm_ffn_w,
                 router_group_w, router_group_b, router_expert_w, router_expert_b, exp_w1, exp_w3, exp_w2,
                 norm_final_w):
    d = norm_mix_w.shape[-1]
    d_inner = ssm_norm_w.shape[-1]
    conv_dim = ssm_conv_b.shape[-1]
    n_heads = dt_bias.shape[-1]
    s2 = 2 * d
    s3 = s2 + d_inner
    s4 = s3 + conv_dim
    s5 = s4 + n_heads
    w = w_in[0]
    row = lambda v: v.reshape(1, -1).astype(F32)
    pad_lanes = lambda v: jnp.pad(v, ((0, 0), (0, LANES - v.shape[-1])))
    router_w = pad_lanes(jnp.concatenate([router_expert_w[0], router_group_w[0]], axis=1))
    router_w_hi = router_w.astype(BF16)
    return {
        "norm_w": row(norm_mix_w[0]),
        "w_vg": w[:, :s2].astype(BF16),
        "w_zx": jnp.concatenate([w[:, s2:s3].reshape(d, N_GROUPS, GROUP_W),
                                 _group_xbc_cols(w[:, s3:s4], d_inner).reshape(d, N_GROUPS, XBC_GROUP_W)],
                                axis=-1).reshape(d, -1).astype(BF16),
        "w_dt": pad_lanes(w[:, s4:s5]).astype(BF16),
        "w_gate": w[:, s5:].astype(BF16),
        "conf_dw_w": jnp.pad(conf_dw_w[0], ((0, CONF_PAD - CONF_K), (0, 0))),
        "conf_dw_b": row(conf_dw_b[0]),
        "conf_ln_g": row(conf_ln_g[0]),
        "conf_ln_b": row(conf_ln_b[0]),
        "conf_out_w": conf_out_w[0].astype(BF16),
        "conf_out_b": row(conf_out_b[0]),
        "ssm_conv_w": jnp.pad(_group_xbc_cols(ssm_conv_w[0], d_inner), ((0, SUBLANES - SSM_K), (0, 0))),
        "ssm_conv_b": row(_group_xbc_cols(ssm_conv_b[0], d_inner)),
        "dt_bias": pad_lanes(row(dt_bias[0])),
        "a_log": pad_lanes(row(a_log[0])),
        "d_skip": row(jnp.repeat(d_skip[0], HEAD_DIM)),
        "ssm_norm_w": row(ssm_norm_w[0]),
        "ssm_out_w": ssm_out_w[0].astype(BF16),
        "gate_b": row(gate_b[0]),
        "w_o": w_o[0].astype(BF16),
        "norm_ffn_w": row(norm_ffn_w[0]),
        "router_w_hi": router_w_hi,
        "router_w_lo": (router_w - router_w_hi.astype(F32)).astype(BF16),
        "router_b": pad_lanes(row(jnp.concatenate([router_expert_b[0], router_group_b[0]]))),
        "exp_w1": exp_w1[0].astype(BF16),
        "exp_w3": exp_w3[0].astype(BF16),
        "exp_w2": exp_w2[0].astype(BF16),
        "norm_final_w": row(norm_final_w),
    }


def kernel(x_prompt, x_sample, state_conf_conv, state_ssm_conv, state_ssm, meta_tokens, norm_mix_w, w_in, conf_dw_w, conf_dw_b, conf_ln_g, conf_ln_b, conf_out_w, conf_out_b, ssm_conv_w, ssm_conv_b, dt_bias, a_log, d_skip, ssm_norm_w, ssm_out_w, gate_b, w_o, norm_ffn_w, router_group_w, router_group_b, router_expert_w, router_expert_b, exp_w1, exp_w3, exp_w2, norm_final_w):
    assert norm_mix_w.shape[0] == 1, "single-layer trunk"
    b, l, d = x_prompt.shape
    nb, n_new, _ = x_sample.shape
    assert l % TILE == 0 and n_new <= SAMPLE_ROWS and (nb * SAMPLE_ROWS) % TILE == 0
    p = _prep_params(norm_mix_w, w_in, conf_dw_w, conf_dw_b, conf_ln_g, conf_ln_b, conf_out_w, conf_out_b,
                     ssm_conv_w, ssm_conv_b, dt_bias, a_log, d_skip, ssm_norm_w, ssm_out_w, gate_b, w_o,
                     norm_ffn_w, router_group_w, router_group_b, router_expert_w, router_expert_b, exp_w1, exp_w3,
                     exp_w2, norm_final_w)

    assert meta_tokens.shape[0] == N_META
    ya_p, new_conf_p = _conf_prompt(meta_tokens.astype(F32), x_prompt, p)
    d_inner = ssm_norm_w.shape[-1]
    conv_dim = ssm_conv_b.shape[-1]
    head = jnp.concatenate([jnp.zeros((TILE - N_META, d), F32), meta_tokens.astype(F32)], axis=0)
    _, meta_sconv, meta_ssm = _ssm_prompt(head[None], jnp.zeros((SCONV_PAD, conv_dim), F32),
                                          jnp.zeros(state_ssm.shape[2:], F32), p, TILE - N_META)
    meta_hist = jnp.pad(meta_sconv[0], ((SCONV_PAD - (SSM_K - 1), 0), (0, 0)))
    yb_p, new_sconv_p, new_ssm_p = _ssm_prompt(x_prompt, meta_hist, meta_ssm[0], p, 0)

    xs_rows = jnp.pad(x_sample, ((0, 0), (0, SAMPLE_ROWS - n_new), (0, 0))).reshape(nb * SAMPLE_ROWS, d)
    ya_s, new_conf_s = _conf_sample(xs_rows, state_conf_conv[0], p, n_new)
    yb_s, new_sconv_s, new_ssm_s = _ssm_sample(xs_rows, state_ssm_conv[0], state_ssm[0], p, n_new)
    unpad = lambda v: v.reshape(nb, SAMPLE_ROWS, d)[:, :n_new].reshape(nb * n_new, d)

    outs = []
    for x, ya, yb in ((x_prompt.reshape(b * l, d), ya_p.reshape(b * l, d), yb_p.reshape(b * l, d)),
                      (x_sample.reshape(nb * n_new, d), unpad(ya_s), unpad(yb_s))):
        tm = next(t for t in (1024, 512, TILE) if x.shape[0] % t == 0)
        x1, xn2, comb = _merge(x, ya, yb, p, min(tm, 512))
        outs.append(_moe(xn2, comb, x1, p, tm))
    y_prompt = outs[0].reshape(b, l, d)
    y_sample = outs[1].reshape(nb, n_new, d)
    return (y_prompt, y_sample, new_conf_p[None], _ungroup_xbc_cols(new_sconv_p, d_inner)[None], new_ssm_p[None],
            new_conf_s[None], new_sconv_s[None], new_ssm_s[None])
```

```python
import functools

import jax
import jax.numpy as jnp
from jax import lax
from jax.experimental import pallas as pl
from jax.experimental.pallas import tpu as pltpu

F32 = jnp.float32
BF16 = jnp.bfloat16
EPS = 1e-6

LANES = 128
SUBLANES = 8
TILE = 128
VMEM_LIMIT = 56 * 1024 * 1024

N_META = 16
CONF_K = 31
SSM_K = 4
HEAD_DIM = 64
N_STATE = 128
N_GROUPS = 8
HEADS_PER_GROUP = 4
GROUP_W = HEADS_PER_GROUP * HEAD_DIM
N_EXPERTS = 32
EXPERTS_PER_GROUP = 8
N_EXPERT_GROUPS = 4
SAMPLE_ROWS = 8
CONF_PAD = 32
SCONV_PAD = 8
XBC_GROUP_W = GROUP_W + 2 * N_STATE
ZX_GROUP_W = GROUP_W + XBC_GROUP_W
GSEL_LANE = 64


def _rms(x, w):
    return x * lax.rsqrt(jnp.mean(x * x, axis=-1, keepdims=True) + EPS) * w


def _sigmoid(x):
    return 0.5 * jnp.tanh(0.5 * x) + 0.5


def _silu(x):
    return x * _sigmoid(x)


def _dot(a, b):
    return jnp.dot(a, b, preferred_element_type=F32)


def _dot_nt(a, b):
    return lax.dot_general(a, b, (((1,), (1,)), ((), ())), preferred_element_type=F32)


def _dot_tn(a, b):
    return lax.dot_general(a, b, (((0,), (0,)), ((), ())), preferred_element_type=F32)


def _const_spec(shape):
    zeros = (0,) * len(shape)
    return pl.BlockSpec(shape, lambda *_: zeros, pipeline_mode=pl.Buffered(1))


def _dwconv(src_ref, base, rows, w_ref, b_ref, taps, width, emit, col0=0):
    for c in range(col0 // LANES, (col0 + width) // LANES):
        cs = slice(c * LANES, (c + 1) * LANES)
        acc = jnp.broadcast_to(b_ref[:, cs], (rows, LANES))
        for r in range(min(SUBLANES, taps)):
            qs = range((taps - r + SUBLANES - 1) // SUBLANES)
            slab = src_ref[pl.ds(base + r, rows + SUBLANES * (len(qs) - 1)), cs]
            part = w_ref[r:r + 1, cs] * slab[0:rows]
            for q in qs[1:]:
                k = SUBLANES * q + r
                part = part + w_ref[k:k + 1, cs] * slab[SUBLANES * q:SUBLANES * q + rows]
            acc = acc + part
        emit(cs, acc)


def _conf_tail(conv_ref, ln_g_ref, ln_b_ref, wout_ref, bout_ref):
    c = conv_ref[...]
    mu = jnp.mean(c, axis=-1, keepdims=True)
    d = c - mu
    var = jnp.mean(d * d, axis=-1, keepdims=True)
    y = d * lax.rsqrt(var + EPS) * ln_g_ref[...] + ln_b_ref[...]
    y = _silu(y)
    return _dot(y.astype(BF16), wout_ref[...]) + bout_ref[...]


def _conf_glu(x, nw_ref, wvg_ref):
    d = x.shape[-1]
    xn = _rms(x, nw_ref[...]).astype(BF16)
    vg = _dot(xn, wvg_ref[...])
    return vg[:, :d] * _sigmoid(vg[:, d:])


def _conf_prompt_kernel(meta_ref, x_ref, nw_ref, wvg_ref, dww_ref, dwb_ref, lng_ref, lnb_ref, wout_ref, bout_ref,
                        ya_ref, nc_ref, afull_ref, conv_ref):
    j = pl.program_id(1)
    nq, rows, d = x_ref.shape

    @pl.when(j == 0)
    def _():
        a_meta = _conf_glu(meta_ref[...], nw_ref, wvg_ref)
        for q in range(nq):
            afull_ref[q, 0:CONF_PAD - N_META, :] = jnp.zeros((CONF_PAD - N_META, d), F32)
            afull_ref[q, CONF_PAD - N_META:CONF_PAD, :] = a_meta

    a = _conf_glu(x_ref[...].reshape(nq * rows, d), nw_ref, wvg_ref)
    for q in range(nq):
        afull_ref[q, CONF_PAD:CONF_PAD + rows, :] = a[q * rows:(q + 1) * rows]
        for r0 in range(0, rows, TILE):
            def emit(cs, v, r0=q * rows + r0):
                conv_ref[r0:r0 + TILE, cs] = v

            _dwconv(afull_ref.at[q], CONF_PAD - (CONF_K - 1) + r0, TILE, dww_ref, dwb_ref, CONF_K, d, emit)
        nc_ref[q] = afull_ref[q, pl.ds(CONF_PAD + rows - (CONF_K - 1), CONF_K - 1), :]
        afull_ref[q, 0:CONF_PAD, :] = afull_ref[q, rows:rows + CONF_PAD, :]
    ya_ref[...] = _conf_tail(conv_ref, lng_ref, lnb_ref, wout_ref, bout_ref).reshape(nq, rows, d).astype(BF16)


def _conf_sample_kernel(x_ref, st_ref, nw_ref, wvg_ref, dww_ref, dwb_ref, lng_ref, lnb_ref, wout_ref, bout_ref,
                        ya_ref, nc_ref, a_ref, cs_ref, conv_ref, *, n_new):
    d = x_ref.shape[-1]
    n_seq = TILE // SAMPLE_ROWS
    a_ref[...] = _conf_glu(x_ref[...], nw_ref, wvg_ref)

    def seq_body(s, carry):
        r0 = pl.multiple_of(s * SAMPLE_ROWS, SAMPLE_ROWS)
        cs_ref[CONF_PAD - (CONF_K - 1):CONF_PAD, :] = st_ref[s]
        cs_ref[CONF_PAD:CONF_PAD + SAMPLE_ROWS, :] = a_ref[pl.ds(r0, SAMPLE_ROWS), :]

        def emit(cs, v):
            conv_ref[pl.ds(r0, SAMPLE_ROWS), cs] = v

        _dwconv(cs_ref, CONF_PAD - (CONF_K - 1), SAMPLE_ROWS, dww_ref, dwb_ref, CONF_K, d, emit)
        nc_ref[s] = cs_ref[pl.ds(CONF_PAD + n_new - (CONF_K - 1), CONF_K - 1), :]
        return carry

    lax.fori_loop(0, n_seq, seq_body, 0)
    ya_ref[...] = _conf_tail(conv_ref, lng_ref, lnb_ref, wout_ref, bout_ref).astype(BF16)


def _conf_weights(p):
    d = p["norm_w"].shape[-1]
    return [p["norm_w"], p["w_vg"], p["conf_dw_w"], p["conf_dw_b"], p["conf_ln_g"], p["conf_ln_b"],
            p["conf_out_w"], p["conf_out_b"]], [
        _const_spec((1, d)), _const_spec((d, 2 * d)), _const_spec((CONF_PAD, d)), _const_spec((1, d)),
        _const_spec((1, d)), _const_spec((1, d)), _const_spec((d, d)), _const_spec((1, d))]


def _conf_prompt(meta, x_prompt, p):
    b, l, d = x_prompt.shape
    rows = 2 * TILE if l % (2 * TILE) == 0 else TILE
    nq = 2 if b % 2 == 0 else 1
    weights, wspecs = _conf_weights(p)
    xmap = lambda i, j: (i, j, 0)
    return pl.pallas_call(
        _conf_prompt_kernel,
        grid=(b // nq, l // rows),
        in_specs=[_const_spec(meta.shape), pl.BlockSpec((nq, rows, d), xmap)] + wspecs,
        out_specs=[pl.BlockSpec((nq, rows, d), xmap),
                   pl.BlockSpec((nq, CONF_K - 1, d), lambda i, j: (i, 0, 0))],
        out_shape=[jax.ShapeDtypeStruct((b, l, d), BF16), jax.ShapeDtypeStruct((b, CONF_K - 1, d), F32)],
        scratch_shapes=[pltpu.VMEM((nq, CONF_PAD + rows, d), F32), pltpu.VMEM((nq * rows, d), F32)],
        compiler_params=pltpu.CompilerParams(dimension_semantics=("arbitrary", "arbitrary"),
                                             vmem_limit_bytes=VMEM_LIMIT),
        name="conf_prompt",
    )(meta, x_prompt, *weights)


def _conf_sample(x_rows, state, p, n_new):
    rows, d = x_rows.shape
    n_seq = TILE // SAMPLE_ROWS
    weights, wspecs = _conf_weights(p)
    return pl.pallas_call(
        functools.partial(_conf_sample_kernel, n_new=n_new),
        grid=(rows // TILE,),
        in_specs=[pl.BlockSpec((TILE, d), lambda i: (i, 0)),
                  pl.BlockSpec((n_seq, CONF_K - 1, d), lambda i: (i, 0, 0))] + wspecs,
        out_specs=[pl.BlockSpec((TILE, d), lambda i: (i, 0)),
                   pl.BlockSpec((n_seq, CONF_K - 1, d), lambda i: (i, 0, 0))],
        out_shape=[jax.ShapeDtypeStruct((rows, d), BF16),
                   jax.ShapeDtypeStruct((rows // SAMPLE_ROWS, CONF_K - 1, d), F32)],
        scratch_shapes=[pltpu.VMEM((TILE, d), F32), pltpu.VMEM((CONF_PAD + SAMPLE_ROWS, d), F32),
                        pltpu.VMEM((TILE, d), F32)],
        compiler_params=pltpu.CompilerParams(dimension_semantics=("arbitrary",), vmem_limit_bytes=VMEM_LIMIT),
        name="conf_sample",
    )(x_rows, state, *weights)


def _xbc_col_pairs(d_inner):
    pairs = []
    for g in range(N_GROUPS):
        c0 = g * XBC_GROUP_W
        pairs.append((slice(c0, c0 + GROUP_W), slice(g * GROUP_W, (g + 1) * GROUP_W)))
        for k in range(2):
            ours = c0 + GROUP_W + k * N_STATE
            theirs = d_inner + (k * N_GROUPS + g) * N_STATE
            pairs.append((slice(ours, ours + N_STATE), slice(theirs, theirs + N_STATE)))
    return pairs


def _project_group(g, xn, wzx_ref, z_ref, pre_ref, row0):
    zx = _dot(xn, wzx_ref[:, g * ZX_GROUP_W:(g + 1) * ZX_GROUP_W])
    z_ref[:, g * GROUP_W:(g + 1) * GROUP_W] = zx[:, :GROUP_W]
    pre_ref[row0:row0 + xn.shape[0], g * XBC_GROUP_W:(g + 1) * XBC_GROUP_W] = zx[:, GROUP_W:]


def _cumsum_rows(x, row):
    step = 1
    while step < x.shape[0]:
        x = x + jnp.where(row >= step, pltpu.roll(x, step, axis=0), 0.0)
        step *= 2
    return x


def _expand_heads(m, g, lane):
    rows = m.shape[0]
    cols = [jnp.broadcast_to(m[:, HEADS_PER_GROUP * g + i:HEADS_PER_GROUP * g + i + 1], (rows, LANES))
            for i in range(HEADS_PER_GROUP)]
    lo = jnp.where(lane < HEAD_DIM, cols[0], cols[1])
    hi = jnp.where(lane < HEAD_DIM, cols[2], cols[3])
    return jnp.concatenate([lo, hi], axis=1)


def _ssd_prelude(dtr, valid, seg_len, dtb_ref, alog_ref):
    row = lax.broadcasted_iota(jnp.int32, (TILE, TILE), 0)
    col = lax.broadcasted_iota(jnp.int32, (TILE, TILE), 1)
    same_seq = (col // seg_len) == (row // seg_len)
    causal = (col <= row) & same_seq
    xdt = dtr + dtb_ref[...]
    e = jnp.exp(-jnp.abs(xdt))
    u = 1.0 + e
    dt = (jnp.maximum(xdt, 0.0) + jnp.where(u == 1.0, e, jnp.log(u) * e / (u - 1.0))) * valid
    da = dt * (-jnp.exp(alog_ref[...]))
    d1 = da.astype(BF16)
    r1 = da - d1.astype(F32)
    d2 = r1.astype(BF16)
    d3 = (r1 - d2.astype(F32)).astype(BF16)
    lower = jnp.where(causal, 1.0, 0.0).astype(BF16)
    upper = jnp.where((col > row) & same_seq, 1.0, 0.0).astype(BF16)
    cum = _dot(lower, d1) + _dot(lower, d2) + _dot(lower, d3)
    rest = _dot(upper, d1) + _dot(upper, d2) + _dot(upper, d3)
    return {
        "causal": causal,
        "cum": cum,
        "cum_t": cum.T,
        "dt_t": dt.T,
        "ecum": jnp.exp(cum),
        "tail": jnp.exp(rest) * dt,
    }


_PRE_KEYS = ("cum", "cum_t", "dt_t", "ecum", "tail")


def _ssd_group(g, pre, z_ref, xc_ref, seg_len, n_seq, h_in, h_out, dskip_ref, normw_ref, after_first_dot=None):
    lane = lax.broadcasted_iota(jnp.int32, (TILE, LANES), 1)
    lane_g = lax.broadcasted_iota(jnp.int32, (TILE, GROUP_W), 1)
    cum, cum_t, dt_t = pre["cum"], pre["cum_t"], pre["dt_t"]
    c0 = g * XBC_GROUP_W
    gs = slice(g * GROUP_W, (g + 1) * GROUP_W)
    xs = xc_ref[:, c0:c0 + GROUP_W]
    zg = z_ref[:, gs]
    bm16 = xc_ref[:, c0 + GROUP_W:c0 + GROUP_W + N_STATE].astype(BF16)
    cm16 = xc_ref[:, c0 + GROUP_W + N_STATE:c0 + XBC_GROUP_W].astype(BF16)
    cb = _dot_nt(cm16, bm16)
    extra = after_first_dot() if after_first_dot is not None else None
    seqs = [(s, slice(s * seg_len, (s + 1) * seg_len), (s + 1) * seg_len - 1) for s in range(n_seq)]
    ystate = [_dot_nt(cm16[rs], h_in(s, g).astype(BF16)) for s, rs, _ in seqs]
    if n_seq * seg_len < TILE:
        ystate.append(jnp.zeros((TILE - n_seq * seg_len, GROUP_W), F32))
    ys = ystate[0] if len(ystate) == 1 else jnp.concatenate(ystate, axis=0)
    ws, xms = [], []
    for i in range(HEADS_PER_GROUP):
        h = HEADS_PER_GROUP * g + i
        seg = cum[:, h:h + 1] - cum_t[h:h + 1, :]
        decay = jnp.exp(jnp.where(pre["causal"], seg, -jnp.inf))
        ws.append((cb * decay * dt_t[h:h + 1, :]).astype(BF16))
        xms.append(jnp.where((lane_g // HEAD_DIM) == i, xs, 0.0).astype(BF16))
    yg = _dot(jnp.concatenate(ws, axis=1), jnp.concatenate(xms, axis=0))
    yg = yg + ys * _expand_heads(pre["ecum"], g, lane)
    xw16 = (xs * _expand_heads(pre["tail"], g, lane)).astype(BF16)

    def update_state():
        for s, rs, last in seqs:
            upd = _dot_tn(xw16[rs], bm16[rs])
            dec = jnp.concatenate(
                [jnp.broadcast_to(jnp.exp(cum_t[HEADS_PER_GROUP * g + i:HEADS_PER_GROUP * g + i + 1, last:last + 1]),
                                  (HEAD_DIM, N_STATE)) for i in range(HEADS_PER_GROUP)], axis=0)
            h_out(s, g, h_in(s, g) * dec + upd)

    v = (yg + dskip_ref[:, gs] * xs) * _silu(zg)
    v = v * lax.rsqrt(jnp.mean(v * v, axis=-1, keepdims=True) + EPS) * normw_ref[:, gs]
    return v.astype(BF16), update_state, extra


def _out_proj_group(g, v16, wout_ref):
    return _dot(v16, wout_ref[g * GROUP_W:(g + 1) * GROUP_W, :])


def _ssm_prompt_kernel(x_ref, hist_ref, h0_ref, nw_ref, wzx_ref, wdt_ref, cw_ref, cb_ref, dtb_ref, alog_ref,
                       dskip_ref, normw_ref, wout_ref, yb_ref, ns_ref, hs_ref, z_ref, cfull_ref, xc_ref, pre_ref, *,
                       n_inert):
    s = pl.program_id(1)
    n_tiles = pl.num_programs(1) - 1
    nq, _, d = x_ref.shape
    seqs = range(nq)

    @pl.when(s == 0)
    def _():
        for q in seqs:
            cfull_ref[q, 0:SCONV_PAD, :] = hist_ref[...]
            hs_ref[q] = h0_ref[...]
        z_ref[...] = jnp.zeros(z_ref.shape, F32)
        xc_ref[...] = jnp.zeros(xc_ref.shape, F32)
        pre_ref[...] = jnp.zeros(pre_ref.shape, F32)

    xn = _rms(x_ref[...].reshape(nq * TILE, d), nw_ref[...]).astype(BF16)
    row = lax.broadcasted_iota(jnp.int32, (TILE, TILE), 0)
    col = lax.broadcasted_iota(jnp.int32, (TILE, TILE), 1)
    pres = []
    for q in seqs:
        pre = {k: pre_ref[q, i] for i, k in enumerate(_PRE_KEYS)}
        pre["causal"] = col <= row
        pres.append(pre)
    rowv = lax.broadcasted_iota(jnp.int32, (TILE, 1), 0)
    valid = jnp.where((s == 0) & (rowv < n_inert), 0.0, 1.0)
    dtr = _dot(xn, wdt_ref[...])

    def state_access(q):
        def h_in(_, g):
            return hs_ref[q, HEADS_PER_GROUP * g:HEADS_PER_GROUP * (g + 1)].reshape(GROUP_W, N_STATE)

        def h_out(_, g, v):
            hs_ref[q, HEADS_PER_GROUP * g:HEADS_PER_GROUP * (g + 1)] = v.reshape(HEADS_PER_GROUP, HEAD_DIM, N_STATE)

        return h_in, h_out

    acc = [jnp.zeros((TILE, d), F32) for _ in seqs]
    pending = []

    def finish_oldest():
        v16, update, q, g = pending.pop(0)
        acc[q] = acc[q] + _out_proj_group(g, v16, wout_ref)
        update()

    for g in range(N_GROUPS):
        projected = {}
        for q in seqs:
            def independent_dots(g=g, q=q, projected=projected):
                if q == 0:
                    projected["zx"] = _dot(xn, wzx_ref[:, g * ZX_GROUP_W:(g + 1) * ZX_GROUP_W])
                zx = projected["zx"][q * TILE:(q + 1) * TILE]
                z_ref[q, :, g * GROUP_W:(g + 1) * GROUP_W] = zx[:, :GROUP_W]
                cfull_ref[q, SCONV_PAD:SCONV_PAD + TILE, g * XBC_GROUP_W:(g + 1) * XBC_GROUP_W] = zx[:, GROUP_W:]
                if len(pending) >= nq:
                    finish_oldest()

            h_in, h_out = state_access(q)
            v16, update, _ = _ssd_group(g, pres[q], z_ref.at[q], xc_ref.at[q], TILE, 1, h_in, h_out, dskip_ref,
                                        normw_ref, independent_dots)

            def emit(cs, v, q=q):
                xc_ref[q, :, cs] = _silu(v)

            _dwconv(cfull_ref.at[q], SCONV_PAD - (SSM_K - 1), TILE, cw_ref, cb_ref, SSM_K, XBC_GROUP_W, emit,
                    col0=g * XBC_GROUP_W)
            pending.append((v16, update, q, g))
    while pending:
        finish_oldest()
    for q in seqs:
        yb_ref[q] = acc[q].astype(BF16)
        pre_next = _ssd_prelude(dtr[q * TILE:(q + 1) * TILE], valid, TILE, dtb_ref, alog_ref)
        for i, k in enumerate(_PRE_KEYS):
            pre_ref[q, i] = pre_next[k]

    @pl.when(s == n_tiles - 1)
    def _():
        for q in seqs:
            ns_ref[q] = cfull_ref[q, pl.ds(SCONV_PAD + TILE - (SSM_K - 1), SSM_K - 1), :]

    for q in seqs:
        cfull_ref[q, 0:SCONV_PAD, :] = cfull_ref[q, TILE:TILE + SCONV_PAD, :]


def _ssm_sample_proj_kernel(x_ref, st_ref, nw_ref, wzx_ref, wdt_ref, cw_ref, cb_ref,
                            z_ref, xc_ref, dtr_ref, ns_ref, xbc_ref, cs_ref, *, n_new):
    conv_dim = xc_ref.shape[-1]
    d_inner = z_ref.shape[-1]
    n_seq = TILE // SAMPLE_ROWS
    xn = _rms(x_ref[...], nw_ref[...]).astype(BF16)
    for g in range(N_GROUPS):
        _project_group(g, xn, wzx_ref, z_ref, xbc_ref, 0)
    dtr_ref[...] = _dot(xn, wdt_ref[...])
    col_pairs = _xbc_col_pairs(d_inner)

    def seq_body(s, carry):
        r0 = pl.multiple_of(s * SAMPLE_ROWS, SAMPLE_ROWS)
        hist = st_ref[s]
        for ours, theirs in col_pairs:
            cs_ref[SCONV_PAD - (SSM_K - 1):SCONV_PAD, ours] = hist[:, theirs]
        cs_ref[SCONV_PAD:SCONV_PAD + SAMPLE_ROWS, :] = xbc_ref[pl.ds(r0, SAMPLE_ROWS), :]

        def emit(cs, v):
            xc_ref[pl.ds(r0, SAMPLE_ROWS), cs] = _silu(v)

        _dwconv(cs_ref, SCONV_PAD - (SSM_K - 1), SAMPLE_ROWS, cw_ref, cb_ref, SSM_K, conv_dim, emit)
        for ours, theirs in col_pairs:
            ns_ref[s, :, theirs] = cs_ref[pl.ds(SCONV_PAD + n_new - (SSM_K - 1), SSM_K - 1), ours]
        return carry

    lax.fori_loop(0, n_seq, seq_body, 0)


def _ssm_sample_ssd_kernel(z_in, xc_in, dtr_in, hin_ref, dtb_ref, alog_ref, dskip_ref, normw_ref, wout_ref,
                           yb_ref, hout_ref, z_ref, xc_ref, dtr_ref, *, n_new, n_seq):
    rows = n_seq * SAMPLE_ROWS
    for src, dst in ((z_in, z_ref), (xc_in, xc_ref), (dtr_in, dtr_ref)):
        dst[0:rows, :] = src[...]
        dst[rows:TILE, :] = jnp.zeros((TILE - rows, dst.shape[-1]), F32)
    rowv = lax.broadcasted_iota(jnp.int32, (TILE, 1), 0)
    valid = jnp.where((rowv < rows) & ((rowv % SAMPLE_ROWS) < n_new), 1.0, 0.0)

    def h_in(s, g):
        return hin_ref[s, HEADS_PER_GROUP * g:HEADS_PER_GROUP * (g + 1)].reshape(GROUP_W, N_STATE)

    def h_out(s, g, v):
        hout_ref[s, HEADS_PER_GROUP * g:HEADS_PER_GROUP * (g + 1)] = v.reshape(HEADS_PER_GROUP, HEAD_DIM, N_STATE)

    pre = _ssd_prelude(dtr_ref[...], valid, SAMPLE_ROWS, dtb_ref, alog_ref)
    acc = jnp.zeros((TILE, yb_ref.shape[-1]), F32)
    for g in range(N_GROUPS):
        v16, update, _ = _ssd_group(g, pre, z_ref, xc_ref, SAMPLE_ROWS, n_seq, h_in, h_out, dskip_ref, normw_ref)
        acc = acc + _out_proj_group(g, v16, wout_ref)
        update()
    yb_ref[...] = acc[0:rows, :].astype(BF16)


def _ssm_prompt(x, hist, h0, p, n_inert):
    b, l, d = x.shape
    nb = l // TILE
    nq = 2 if b % 2 == 0 else 1
    d_inner = p["ssm_norm_w"].shape[-1]
    conv_dim = p["ssm_conv_b"].shape[-1]
    n_heads = d_inner // HEAD_DIM
    xmap = lambda i, j: (i, jnp.minimum(j, nb - 1), 0)
    ymap = lambda i, j: (i, jnp.maximum(j - 1, 0), 0)
    weights = [hist, h0, p["norm_w"], p["w_zx"], p["w_dt"], p["ssm_conv_w"], p["ssm_conv_b"], p["dt_bias"],
               p["a_log"], p["d_skip"], p["ssm_norm_w"], p["ssm_out_w"]]
    wspecs = [_const_spec(w.shape) for w in weights]
    return pl.pallas_call(
        functools.partial(_ssm_prompt_kernel, n_inert=n_inert),
        grid=(b // nq, nb + 1),
        in_specs=[pl.BlockSpec((nq, TILE, d), xmap)] + wspecs,
        out_specs=[pl.BlockSpec((nq, TILE, d), ymap),
                   pl.BlockSpec((nq, SSM_K - 1, conv_dim), lambda i, j: (i, 0, 0)),
                   pl.BlockSpec((nq, n_heads, HEAD_DIM, N_STATE), lambda i, j: (i, 0, 0, 0))],
        out_shape=[jax.ShapeDtypeStruct((b, l, d), BF16), jax.ShapeDtypeStruct((b, SSM_K - 1, conv_dim), F32),
                   jax.ShapeDtypeStruct((b, n_heads, HEAD_DIM, N_STATE), F32)],
        scratch_shapes=[pltpu.VMEM((nq, TILE, d_inner), F32), pltpu.VMEM((nq, SCONV_PAD + TILE, conv_dim), F32),
                        pltpu.VMEM((nq, TILE, conv_dim), F32), pltpu.VMEM((nq, len(_PRE_KEYS), TILE, LANES), F32)],
        compiler_params=pltpu.CompilerParams(dimension_semantics=("arbitrary", "arbitrary"),
                                             vmem_limit_bytes=VMEM_LIMIT),
        name="ssm_prompt",
    )(x, *weights)


def _ssm_sample(x_rows, sconv, h0, p, n_new):
    rows, d = x_rows.shape
    d_inner = p["ssm_norm_w"].shape[-1]
    conv_dim = p["ssm_conv_b"].shape[-1]
    n_heads = d_inner // HEAD_DIM
    n_seq_proj = TILE // SAMPLE_ROWS
    weights = [p["norm_w"], p["w_zx"], p["w_dt"], p["ssm_conv_w"], p["ssm_conv_b"]]
    z, xc, dtr, new_sconv = pl.pallas_call(
        functools.partial(_ssm_sample_proj_kernel, n_new=n_new),
        grid=(rows // TILE,),
        in_specs=[pl.BlockSpec((TILE, d), lambda i: (i, 0)),
                  pl.BlockSpec((n_seq_proj, SSM_K - 1, conv_dim), lambda i: (i, 0, 0))]
        + [_const_spec(w.shape) for w in weights],
        out_specs=[pl.BlockSpec((TILE, d_inner), lambda i: (i, 0)), pl.BlockSpec((TILE, conv_dim), lambda i: (i, 0)),
                   pl.BlockSpec((TILE, LANES), lambda i: (i, 0)),
                   pl.BlockSpec((n_seq_proj, SSM_K - 1, conv_dim), lambda i: (i, 0, 0))],
        out_shape=[jax.ShapeDtypeStruct((rows, d_inner), F32), jax.ShapeDtypeStruct((rows, conv_dim), F32),
                   jax.ShapeDtypeStruct((rows, LANES), F32),
                   jax.ShapeDtypeStruct((rows // SAMPLE_ROWS, SSM_K - 1, conv_dim), F32)],
        scratch_shapes=[pltpu.VMEM((TILE, conv_dim), F32), pltpu.VMEM((SCONV_PAD + SAMPLE_ROWS, conv_dim), F32)],
        compiler_params=pltpu.CompilerParams(dimension_semantics=("arbitrary",), vmem_limit_bytes=VMEM_LIMIT),
        name="ssm_sample_proj",
    )(x_rows, sconv, *weights)

    n_seq = 8 if rows % (8 * SAMPLE_ROWS) == 0 else 4
    r = n_seq * SAMPLE_ROWS
    weights = [p["dt_bias"], p["a_log"], p["d_skip"], p["ssm_norm_w"], p["ssm_out_w"]]
    yb, h_new = pl.pallas_call(
        functools.partial(_ssm_sample_ssd_kernel, n_new=n_new, n_seq=n_seq),
        grid=(rows // r,),
        in_specs=[pl.BlockSpec((r, d_inner), lambda i: (i, 0)), pl.BlockSpec((r, conv_dim), lambda i: (i, 0)),
                  pl.BlockSpec((r, LANES), lambda i: (i, 0)),
                  pl.BlockSpec((n_seq, n_heads, HEAD_DIM, N_STATE), lambda i: (i, 0, 0, 0))]
        + [_const_spec(w.shape) for w in weights],
        out_specs=[pl.BlockSpec((r, d), lambda i: (i, 0)),
                   pl.BlockSpec((n_seq, n_heads, HEAD_DIM, N_STATE), lambda i: (i, 0, 0, 0))],
        out_shape=[jax.ShapeDtypeStruct((rows, d), BF16), jax.ShapeDtypeStruct(h0.shape, F32)],
        scratch_shapes=[pltpu.VMEM((TILE, d_inner), F32), pltpu.VMEM((TILE, conv_dim), F32),
                        pltpu.VMEM((TILE, LANES), F32)],
        compiler_params=pltpu.CompilerParams(dimension_semantics=("arbitrary",), vmem_limit_bytes=VMEM_LIMIT),
        name="ssm_sample_ssd",
    )(z, xc, dtr, h0, *weights)
    return yb, new_sconv, h_new


def _merge_kernel(x_ref, ya_ref, yb_ref, nw_ref, wg_ref, gb_ref, wo_ref, nfw_ref, rwh_ref, rwl_ref, rb_ref,
                  x1_ref, xn2_ref, comb_ref):
    d = x_ref.shape[-1]
    half = x_ref.shape[0] // 2
    parts = [slice(0, half), slice(half, 2 * half)]
    xs = [x_ref[rs, :] for rs in parts]
    xns = [_rms(x, nw_ref[...]).astype(BF16) for x in xs]
    glog = [_dot(xn, wg_ref[...]) for xn in xns]
    merged = []
    for rs, gl_ in zip(parts, glog):
        gates = _sigmoid(gl_ + gb_ref[...])
        merged.append((gates[:, :d] * ya_ref[rs, :] + gates[:, d:] * yb_ref[rs, :]).astype(BF16))
    outs = [_dot(m, wo_ref[...]) for m in merged]
    logit_parts = []
    for rs, x, o in zip(parts, xs, outs):
        x1 = x + o
        x1_ref[rs, :] = x1
        xn2 = _rms(x1, nfw_ref[...])
        x_hi = xn2.astype(BF16)
        xn2_ref[rs, :] = x_hi
        x_lo = (xn2 - x_hi.astype(F32)).astype(BF16)
        logit_parts.append(_dot(x_hi, rwh_ref[...]) + _dot(x_lo, rwh_ref[...]) + _dot(x_hi, rwl_ref[...]))

    logits = jnp.concatenate(logit_parts, axis=0) + rb_ref[...]
    rows = logits.shape[0]
    lane = lax.broadcasted_iota(jnp.int32, (rows, LANES), 1)
    neg = -jnp.inf
    is_g = (lane >= N_EXPERTS) & (lane < N_EXPERTS + N_EXPERT_GROUPS)
    gl = jnp.where(is_g, logits, neg)
    gmax = jnp.max(gl, axis=-1, keepdims=True)
    gsel = jnp.min(jnp.where(gl == gmax, lane, LANES), axis=-1, keepdims=True) - N_EXPERTS
    gprob = 1.0 / jnp.sum(jnp.exp(gl - gmax), axis=-1, keepdims=True)
    el = jnp.where((lane < N_EXPERTS) & ((lane // EXPERTS_PER_GROUP) == gsel), logits, neg)
    m1 = jnp.max(el, axis=-1, keepdims=True)
    i1 = jnp.min(jnp.where(el == m1, lane, LANES), axis=-1, keepdims=True)
    el2 = jnp.where(lane == i1, neg, el)
    m2 = jnp.max(el2, axis=-1, keepdims=True)
    i2 = jnp.min(jnp.where(el2 == m2, lane, LANES), axis=-1, keepdims=True)
    e2 = jnp.exp(m2 - m1)
    den = 1.0 + e2
    comb = jnp.where(lane == i1, (1.0 / den) * gprob, jnp.where(lane == i2, (e2 / den) * gprob, 0.0))
    comb_ref[...] = jnp.where(lane == GSEL_LANE, gsel.astype(F32), comb)


def _merge(x, ya, yb, p, tm):
    rows, d = x.shape
    weights = [p["norm_w"], p["w_gate"], p["gate_b"], p["w_o"], p["norm_ffn_w"], p["router_w_hi"],
               p["router_w_lo"], p["router_b"]]
    row_spec = pl.BlockSpec((tm, d), lambda i: (i, 0))
    return pl.pallas_call(
        _merge_kernel,
        grid=(rows // tm,),
        in_specs=[row_spec, row_spec, row_spec] + [_const_spec(w.shape) for w in weights],
        out_specs=[row_spec, row_spec, pl.BlockSpec((tm, LANES), lambda i: (i, 0))],
        out_shape=[jax.ShapeDtypeStruct((rows, d), F32), jax.ShapeDtypeStruct((rows, d), BF16),
                   jax.ShapeDtypeStruct((rows, LANES), F32)],
        compiler_params=pltpu.CompilerParams(dimension_semantics=("arbitrary",), vmem_limit_bytes=VMEM_LIMIT),
        name="merge_router",
    )(x, ya, yb, *weights)


def _moe_kernel(xn_ref, comb_ref, x1_ref, w1_ref, w3_ref, w2_ref, nw_ref, y_ref, rt_ref, *, chunk):
    g = pl.program_id(1)
    tm = xn_ref.shape[0]
    comb = comb_ref[...]
    lane = lax.broadcasted_iota(jnp.int32, (tm, LANES), 1)
    gsel = comb[:, GSEL_LANE:GSEL_LANE + 1]

    @pl.when(g == 0)
    def _():
        row = lax.broadcasted_iota(jnp.int32, (tm, LANES), 0)
        onehot = jnp.where(lane.astype(F32) == gsel, 1.0, 0.0)
        incl = _cumsum_rows(onehot, row)
        rank = jnp.sum(onehot * (incl - onehot), axis=-1, keepdims=True)
        packed = jnp.where(lane == 0, rank, jnp.where(lane == 1, gsel, 0.0))
        rt_ref[0:SUBLANES, :] = packed.T[0:SUBLANES, :]
        rt_ref[SUBLANES:2 * SUBLANES, 0:LANES] = jnp.broadcast_to(incl[tm - 1:tm, :], (SUBLANES, LANES))
        y_ref[...] = jnp.zeros(y_ref.shape, F32)

    gf = g.astype(F32)
    lane_row = lax.broadcasted_iota(jnp.int32, (1, LANES), 1)
    count = jnp.sum(jnp.where(lane_row == g, rt_ref[SUBLANES:SUBLANES + 1, 0:LANES], 0.0), axis=-1, keepdims=True)
    n_chunks = (count[0, 0].astype(jnp.int32) + chunk - 1) // chunk
    w_hi = jnp.where(lane < N_EXPERTS, comb, 0.0).astype(BF16)
    w_lo = (jnp.where(lane < N_EXPERTS, comb, 0.0) - w_hi.astype(F32)).astype(BF16)
    rowi = lax.broadcasted_iota(jnp.int32, (chunk, tm), 0).astype(F32)
    lane_c = lax.broadcasted_iota(jnp.int32, (chunk, LANES), 1)

    def chunk_body(c, carry):
        base = (c * chunk).astype(F32)
        sel = jnp.where((rt_ref[1:2, :] == gf) & ((rt_ref[0:1, :] - base) == rowi), 1.0, 0.0).astype(BF16)
        xc = _dot(sel, xn_ref[...]).astype(BF16)
        wc = _dot(sel, w_hi) + _dot(sel, w_lo)
        acts = []
        for e in range(EXPERTS_PER_GROUP):
            ce = jnp.sum(jnp.where(lane_c == g * EXPERTS_PER_GROUP + e, wc, 0.0), axis=-1, keepdims=True)
            acts.append((_silu(_dot(xc, w1_ref[e])) * _dot(xc, w3_ref[e]) * ce).astype(BF16))
        y = _dot(jnp.concatenate(acts, axis=1), w2_ref[...])
        y_ref[...] += _dot_tn(sel, y.astype(BF16))
        return carry

    lax.fori_loop(0, n_chunks, chunk_body, 0)

    @pl.when(g == pl.num_programs(1) - 1)
    def _():
        y_ref[...] = _rms(x1_ref[...] + y_ref[...], nw_ref[...])


def _moe(xn2, comb, x1, p, tm):
    rows, d = x1.shape
    n_e, _, d_e = p["exp_w1"].shape
    n_g = n_e // EXPERTS_PER_GROUP
    w2 = p["exp_w2"].reshape(n_g, EXPERTS_PER_GROUP * d_e, d)
    pack = 2 * SUBLANES
    chunk = -(-(tm * 9 // (8 * n_g)) // pack) * pack
    row = lambda i, g: (i, 0)
    return pl.pallas_call(
        functools.partial(_moe_kernel, chunk=chunk),
        grid=(rows // tm, n_g),
        in_specs=[pl.BlockSpec((tm, d), row), pl.BlockSpec((tm, LANES), row),
                  pl.BlockSpec((tm, d), row),
                  pl.BlockSpec((EXPERTS_PER_GROUP, d, d_e), lambda i, g: (g, 0, 0)),
                  pl.BlockSpec((EXPERTS_PER_GROUP, d, d_e), lambda i, g: (g, 0, 0)),
                  pl.BlockSpec((None, EXPERTS_PER_GROUP * d_e, d), lambda i, g: (g, 0, 0)),
                  _const_spec((1, d))],
        out_specs=pl.BlockSpec((tm, d), row),
        out_shape=jax.ShapeDtypeStruct((rows, d), F32),
        scratch_shapes=[pltpu.VMEM((2 * SUBLANES, tm), F32)],
        compiler_params=pltpu.CompilerParams(dimension_semantics=("arbitrary", "arbitrary"),
                                             vmem_limit_bytes=VMEM_LIMIT),
        name="moe",
    )(xn2, comb, x1, p["exp_w1"], p["exp_w3"], w2, p["norm_final_w"])


def _group_xbc_cols(v, d_inner):
    lead = v.shape[:-1]
    xs = v[..., :d_inner].reshape(lead + (N_GROUPS, GROUP_W))
    bm = v[..., d_inner:d_inner + N_GROUPS * N_STATE].reshape(lead + (N_GROUPS, N_STATE))
    cm = v[..., d_inner + N_GROUPS * N_STATE:].reshape(lead + (N_GROUPS, N_STATE))
    return jnp.concatenate([xs, bm, cm], axis=-1).reshape(lead + (-1,))


def _ungroup_xbc_cols(v, d_inner):
    lead = v.shape[:-1]
    v = v.reshape(lead + (N_GROUPS, XBC_GROUP_W))
    parts = [v[..., :GROUP_W], v[..., GROUP_W:GROUP_W + N_STATE], v[..., GROUP_W + N_STATE:]]
    return jnp.concatenate([q.reshape(lead + (-1,)) for q in parts], axis=-1)


def _prep_params(norm_mix_w, w_in, conf_dw_w, conf_dw_b, conf_ln_g, conf_ln_b, conf_out_w, conf_out_b, ssm_conv_w,
                 ssm_conv_b, dt_bias, a_log, d_skip, ssm_norm_w, ssm_out_w, gate_b, w_o, norm_ffn_w,
                 router_group_w, router_group_b, router_expert_w, router_expert_b, exp_w1, exp_w3, exp_w2,
                 norm_final_w):
    d = norm_mix_w.shape[-1]
    d_inner = ssm_norm_w.shape[-1]
    conv_dim = ssm_conv_b.shape[-1]
    n_heads = dt_bias.shape[-1]
    s2 = 2 * d
    s3 = s2 + d_inner
    s4 = s3 + conv_dim
    s5 = s4 + n_heads
    w = w_in[0]
    row = lambda v: v.reshape(1, -1).astype(F32)
    pad_lanes = lambda v: jnp.pad(v, ((0, 0), (0, LANES - v.shape[-1])))
    router_w = pad_lanes(jnp.concatenate([router_expert_w[0], router_group_w[0]], axis=1))
    router_w_hi = router_w.astype(BF16)
    return {
        "norm_w": row(norm_mix_w[0]),
        "w_vg": w[:, :s2].astype(BF16),
        "w_zx": jnp.concatenate([w[:, s2:s3].reshape(d, N_GROUPS, GROUP_W),
                                 _group_xbc_cols(w[:, s3:s4], d_inner).reshape(d, N_GROUPS, XBC_GROUP_W)],
                                axis=-1).reshape(d, -1).astype(BF16),
        "w_dt": pad_lanes(w[:, s4:s5]).astype(BF16),
        "w_gate": w[:, s5:].astype(BF16),
        "conf_dw_w": jnp.pad(conf_dw_w[0], ((0, CONF_PAD - CONF_K), (0, 0))),
        "conf_dw_b": row(conf_dw_b[0]),
        "conf_ln_g": row(conf_ln_g[0]),
        "conf_ln_b": row(conf_ln_b[0]),
        "conf_out_w": conf_out_w[0].astype(BF16),
        "conf_out_b": row(conf_out_b[0]),
        "ssm_conv_w": jnp.pad(_group_xbc_cols(ssm_conv_w[0], d_inner), ((0, SUBLANES - SSM_K), (0, 0))),
        "ssm_conv_b": row(_group_xbc_cols(ssm_conv_b[0], d_inner)),
        "dt_bias": pad_lanes(row(dt_bias[0])),
        "a_log": pad_lanes(row(a_log[0])),
        "d_skip": row(jnp.repeat(d_skip[0], HEAD_DIM)),
        "ssm_norm_w": row(ssm_norm_w[0]),
        "ssm_out_w": ssm_out_w[0].astype(BF16),
        "gate_b": row(gate_b[0]),
        "w_o": w_o[0].astype(BF16),
        "norm_ffn_w": row(norm_ffn_w[0]),
        "router_w_hi": router_w_hi,
        "router_w_lo": (router_w - router_w_hi.astype(F32)).astype(BF16),
        "router_b": pad_lanes(row(jnp.concatenate([router_expert_b[0], router_group_b[0]]))),
        "exp_w1": exp_w1[0].astype(BF16),
        "exp_w3": exp_w3[0].astype(BF16),
        "exp_w2": exp_w2[0].astype(BF16),
        "norm_final_w": row(norm_final_w),
    }


def kernel(x_prompt, x_sample, state_conf_conv, state_ssm_conv, state_ssm, meta_tokens, norm_mix_w, w_in, conf_dw_w, conf_dw_b, conf_ln_g, conf_ln_b, conf_out_w, conf_out_b, ssm_conv_w, ssm_conv_b, dt_bias, a_log, d_skip, ssm_norm_w, ssm_out_w, gate_b, w_o, norm_ffn_w, router_group_w, router_group_b, router_expert_w, router_expert_b, exp_w1, exp_w3, exp_w2, norm_final_w):
    assert norm_mix_w.shape[0] == 1, "single-layer trunk"
    b, l, d = x_prompt.shape
    nb, n_new, _ = x_sample.shape
    assert l % TILE == 0 and n_new <= SAMPLE_ROWS and (nb * SAMPLE_ROWS) % TILE == 0
    p = _prep_params(norm_mix_w, w_in, conf_dw_w, conf_dw_b, conf_ln_g, conf_ln_b, conf_out_w, conf_out_b,
                     ssm_conv_w, ssm_conv_b, dt_bias, a_log, d_skip, ssm_norm_w, ssm_out_w, gate_b, w_o,
                     norm_ffn_w, router_group_w, router_group_b, router_expert_w, router_expert_b, exp_w1, exp_w3,
                     exp_w2, norm_final_w)

    assert meta_tokens.shape[0] == N_META
    ya_p, new_conf_p = _conf_prompt(meta_tokens.astype(F32), x_prompt, p)
    d_inner = ssm_norm_w.shape[-1]
    conv_dim = ssm_conv_b.shape[-1]
    head = jnp.concatenate([jnp.zeros((TILE - N_META, d), F32), meta_tokens.astype(F32)], axis=0)
    _, meta_sconv, meta_ssm = _ssm_prompt(head[None], jnp.zeros((SCONV_PAD, conv_dim), F32),
                                          jnp.zeros(state_ssm.shape[2:], F32), p, TILE - N_META)
    meta_hist = jnp.pad(meta_sconv[0], ((SCONV_PAD - (SSM_K - 1), 0), (0, 0)))
    yb_p, new_sconv_p, new_ssm_p = _ssm_prompt(x_prompt, meta_hist, meta_ssm[0], p, 0)

    xs_rows = jnp.pad(x_sample, ((0, 0), (0, SAMPLE_ROWS - n_new), (0, 0))).reshape(nb * SAMPLE_ROWS, d)
    ya_s, new_conf_s = _conf_sample(xs_rows, state_conf_conv[0], p, n_new)
    yb_s, new_sconv_s, new_ssm_s = _ssm_sample(xs_rows, state_ssm_conv[0], state_ssm[0], p, n_new)
    unpad = lambda v: v.reshape(nb, SAMPLE_ROWS, d)[:, :n_new].reshape(nb * n_new, d)

    outs = []
    for x, ya, yb in ((x_prompt.reshape(b * l, d), ya_p.reshape(b * l, d), yb_p.reshape(b * l, d)),
                      (x_sample.reshape(nb * n_new, d), unpad(ya_s), unpad(yb_s))):
        tm = next(t for t in (1024, 512, TILE) if x.shape[0] % t == 0)
        x1, xn2, comb = _merge(x, ya, yb, p, min(tm, 512))
        outs.append(_moe(xn2, comb, x1, p, tm))
    y_prompt = outs[0].reshape(b, l, d)
    y_sample = outs[1].reshape(nb, n_new, d)
    return (y_prompt, y_sample, new_conf_p[None], _ungroup_xbc_cols(new_sconv_p, d_inner)[None], new_ssm_p[None],
            new_conf_s[None], new_sconv_s[None], new_ssm_s[None])
```

```python
import functools

import jax
import jax.numpy as jnp
from jax import lax
from jax.experimental import pallas as pl
from jax.experimental.pallas import tpu as pltpu

F32 = jnp.float32
BF16 = jnp.bfloat16
EPS = 1e-6

LANES = 128
SUBLANES = 8
TILE = 128
VMEM_LIMIT = 56 * 1024 * 1024

N_META = 16
CONF_K = 31
SSM_K = 4
HEAD_DIM = 64
N_STATE = 128
N_GROUPS = 8
HEADS_PER_GROUP = 4
GROUP_W = HEADS_PER_GROUP * HEAD_DIM
N_EXPERTS = 32
EXPERTS_PER_GROUP = 8
N_EXPERT_GROUPS = 4
SAMPLE_ROWS = 8
CONF_PAD = 32
SCONV_PAD = 8
XBC_GROUP_W = GROUP_W + 2 * N_STATE
ZX_GROUP_W = GROUP_W + XBC_GROUP_W
GSEL_LANE = 64


def _rms(x, w):
    return x * lax.rsqrt(jnp.mean(x * x, axis=-1, keepdims=True) + EPS) * w


def _sigmoid(x):
    return 0.5 * jnp.tanh(0.5 * x) + 0.5


def _silu(x):
    return x * _sigmoid(x)


def _dot(a, b):
    return jnp.dot(a, b, preferred_element_type=F32)


def _dot_nt(a, b):
    return lax.dot_general(a, b, (((1,), (1,)), ((), ())), preferred_element_type=F32)


def _dot_tn(a, b):
    return lax.dot_general(a, b, (((0,), (0,)), ((), ())), preferred_element_type=F32)


def _const_spec(shape):
    zeros = (0,) * len(shape)
    return pl.BlockSpec(shape, lambda *_: zeros, pipeline_mode=pl.Buffered(1))


def _dwconv(src_ref, base, rows, w_ref, b_ref, taps, width, emit, col0=0):
    for c in range(col0 // LANES, (col0 + width) // LANES):
        cs = slice(c * LANES, (c + 1) * LANES)
        acc = jnp.broadcast_to(b_ref[:, cs], (rows, LANES))
        for r in range(min(SUBLANES, taps)):
            qs = range((taps - r + SUBLANES - 1) // SUBLANES)
            slab = src_ref[pl.ds(base + r, rows + SUBLANES * (len(qs) - 1)), cs]
            part = w_ref[r:r + 1, cs] * slab[0:rows]
            for q in qs[1:]:
                k = SUBLANES * q + r
                part = part + w_ref[k:k + 1, cs] * slab[SUBLANES * q:SUBLANES * q + rows]
            acc = acc + part
        emit(cs, acc)


def _conf_tail(conv_ref, ln_g_ref, ln_b_ref, wout_ref, bout_ref):
    c = conv_ref[...]
    mu = jnp.mean(c, axis=-1, keepdims=True)
    d = c - mu
    var = jnp.mean(d * d, axis=-1, keepdims=True)
    y = d * lax.rsqrt(var + EPS) * ln_g_ref[...] + ln_b_ref[...]
    y = _silu(y)
    return _dot(y.astype(BF16), wout_ref[...]) + bout_ref[...]


def _conf_glu(x, nw_ref, wvg_ref):
    d = x.shape[-1]
    xn = _rms(x, nw_ref[...]).astype(BF16)
    vg = _dot(xn, wvg_ref[...])
    return vg[:, :d] * _sigmoid(vg[:, d:])


def _conf_prompt_kernel(meta_ref, x_ref, nw_ref, wvg_ref, dww_ref, dwb_ref, lng_ref, lnb_ref, wout_ref, bout_ref,
                        ya_ref, nc_ref, afull_ref, conv_ref):
    j = pl.program_id(1)
    nq, rows, d = x_ref.shape

    @pl.when(j == 0)
    def _():
        a_meta = _conf_glu(meta_ref[...], nw_ref, wvg_ref)
        for q in range(nq):
            afull_ref[q, 0:CONF_PAD - N_META, :] = jnp.zeros((CONF_PAD - N_META, d), F32)
            afull_ref[q, CONF_PAD - N_META:CONF_PAD, :] = a_meta

    a = _conf_glu(x_ref[...].reshape(nq * rows, d), nw_ref, wvg_ref)
    for q in range(nq):
        afull_ref[q, CONF_PAD:CONF_PAD + rows, :] = a[q * rows:(q + 1) * rows]
        for r0 in range(0, rows, TILE):
            def emit(cs, v, r0=q * rows + r0):
                conv_ref[r0:r0 + TILE, cs] = v

            _dwconv(afull_ref.at[q], CONF_PAD - (CONF_K - 1) + r0, TILE, dww_ref, dwb_ref, CONF_K, d, emit)
        nc_ref[q] = afull_ref[q, pl.ds(CONF_PAD + rows - (CONF_K - 1), CONF_K - 1), :]
        afull_ref[q, 0:CONF_PAD, :] = afull_ref[q, rows:rows + CONF_PAD, :]
    ya_ref[...] = _conf_tail(conv_ref, lng_ref, lnb_ref, wout_ref, bout_ref).reshape(nq, rows, d)


def _conf_sample_kernel(x_ref, st_ref, nw_ref, wvg_ref, dww_ref, dwb_ref, lng_ref, lnb_ref, wout_ref, bout_ref,
                        ya_ref, nc_ref, a_ref, cs_ref, conv_ref, *, n_new):
    d = x_ref.shape[-1]
    n_seq = TILE // SAMPLE_ROWS
    a_ref[...] = _conf_glu(x_ref[...], nw_ref, wvg_ref)

    def seq_body(s, carry):
        r0 = pl.multiple_of(s * SAMPLE_ROWS, SAMPLE_ROWS)
        cs_ref[CONF_PAD - (CONF_K - 1):CONF_PAD, :] = st_ref[s]
        cs_ref[CONF_PAD:CONF_PAD + SAMPLE_ROWS, :] = a_ref[pl.ds(r0, SAMPLE_ROWS), :]

        def emit(cs, v):
            conv_ref[pl.ds(r0, SAMPLE_ROWS), cs] = v

        _dwconv(cs_ref, CONF_PAD - (CONF_K - 1), SAMPLE_ROWS, dww_ref, dwb_ref, CONF_K, d, emit)
        nc_ref[s] = cs_ref[pl.ds(CONF_PAD + n_new - (CONF_K - 1), CONF_K - 1), :]
        return carry

    lax.fori_loop(0, n_seq, seq_body, 0)
    ya_ref[...] = _conf_tail(conv_ref, lng_ref, lnb_ref, wout_ref, bout_ref)


def _conf_weights(p):
    d = p["norm_w"].shape[-1]
    return [p["norm_w"], p["w_vg"], p["conf_dw_w"], p["conf_dw_b"], p["conf_ln_g"], p["conf_ln_b"],
            p["conf_out_w"], p["conf_out_b"]], [
        _const_spec((1, d)), _const_spec((d, 2 * d)), _const_spec((CONF_PAD, d)), _const_spec((1, d)),
        _const_spec((1, d)), _const_spec((1, d)), _const_spec((d, d)), _const_spec((1, d))]


def _conf_prompt(meta, x_prompt, p):
    b, l, d = x_prompt.shape
    rows = 2 * TILE if l % (2 * TILE) == 0 else TILE
    nq = 2 if b % 2 == 0 else 1
    weights, wspecs = _conf_weights(p)
    xmap = lambda i, j: (i, j, 0)
    return pl.pallas_call(
        _conf_prompt_kernel,
        grid=(b // nq, l // rows),
        in_specs=[_const_spec(meta.shape), pl.BlockSpec((nq, rows, d), xmap)] + wspecs,
        out_specs=[pl.BlockSpec((nq, rows, d), xmap),
                   pl.BlockSpec((nq, CONF_K - 1, d), lambda i, j: (i, 0, 0))],
        out_shape=[jax.ShapeDtypeStruct((b, l, d), F32), jax.ShapeDtypeStruct((b, CONF_K - 1, d), F32)],
        scratch_shapes=[pltpu.VMEM((nq, CONF_PAD + rows, d), F32), pltpu.VMEM((nq * rows, d), F32)],
        compiler_params=pltpu.CompilerParams(dimension_semantics=("arbitrary", "arbitrary"),
                                             vmem_limit_bytes=VMEM_LIMIT),
        name="conf_prompt",
    )(meta, x_prompt, *weights)


def _conf_sample(x_rows, state, p, n_new):
    rows, d = x_rows.shape
    n_seq = TILE // SAMPLE_ROWS
    weights, wspecs = _conf_weights(p)
    return pl.pallas_call(
        functools.partial(_conf_sample_kernel, n_new=n_new),
        grid=(rows // TILE,),
        in_specs=[pl.BlockSpec((TILE, d), lambda i: (i, 0)),
                  pl.BlockSpec((n_seq, CONF_K - 1, d), lambda i: (i, 0, 0))] + wspecs,
        out_specs=[pl.BlockSpec((TILE, d), lambda i: (i, 0)),
                   pl.BlockSpec((n_seq, CONF_K - 1, d), lambda i: (i, 0, 0))],
        out_shape=[jax.ShapeDtypeStruct((rows, d), F32),
                   jax.ShapeDtypeStruct((rows // SAMPLE_ROWS, CONF_K - 1, d), F32)],
        scratch_shapes=[pltpu.VMEM((TILE, d), F32), pltpu.VMEM((CONF_PAD + SAMPLE_ROWS, d), F32),
                        pltpu.VMEM((TILE, d), F32)],
        compiler_params=pltpu.CompilerParams(dimension_semantics=("arbitrary",), vmem_limit_bytes=VMEM_LIMIT),
        name="conf_sample",
    )(x_rows, state, *weights)


def _xbc_col_pairs(d_inner):
    pairs = []
    for g in range(N_GROUPS):
        c0 = g * XBC_GROUP_W
        pairs.append((slice(c0, c0 + GROUP_W), slice(g * GROUP_W, (g + 1) * GROUP_W)))
        for k in range(2):
            ours = c0 + GROUP_W + k * N_STATE
            theirs = d_inner + (k * N_GROUPS + g) * N_STATE
            pairs.append((slice(ours, ours + N_STATE), slice(theirs, theirs + N_STATE)))
    return pairs


def _project_group(g, xn, wzx_ref, z_ref, pre_ref, row0):
    zx = _dot(xn, wzx_ref[:, g * ZX_GROUP_W:(g + 1) * ZX_GROUP_W])
    z_ref[:, g * GROUP_W:(g + 1) * GROUP_W] = zx[:, :GROUP_W]
    pre_ref[row0:row0 + xn.shape[0], g * XBC_GROUP_W:(g + 1) * XBC_GROUP_W] = zx[:, GROUP_W:]


def _cumsum_rows(x, row):
    step = 1
    while step < x.shape[0]:
        x = x + jnp.where(row >= step, pltpu.roll(x, step, axis=0), 0.0)
        step *= 2
    return x


def _expand_heads(m, g, lane):
    rows = m.shape[0]
    cols = [jnp.broadcast_to(m[:, HEADS_PER_GROUP * g + i:HEADS_PER_GROUP * g + i + 1], (rows, LANES))
            for i in range(HEADS_PER_GROUP)]
    lo = jnp.where(lane < HEAD_DIM, cols[0], cols[1])
    hi = jnp.where(lane < HEAD_DIM, cols[2], cols[3])
    return jnp.concatenate([lo, hi], axis=1)


def _ssd_prelude(dtr, valid, seg_len, dtb_ref, alog_ref):
    row = lax.broadcasted_iota(jnp.int32, (TILE, TILE), 0)
    col = lax.broadcasted_iota(jnp.int32, (TILE, TILE), 1)
    same_seq = (col // seg_len) == (row // seg_len)
    causal = (col <= row) & same_seq
    xdt = dtr + dtb_ref[...]
    e = jnp.exp(-jnp.abs(xdt))
    u = 1.0 + e
    dt = (jnp.maximum(xdt, 0.0) + jnp.where(u == 1.0, e, jnp.log(u) * e / (u - 1.0))) * valid
    da = dt * (-jnp.exp(alog_ref[...]))
    d1 = da.astype(BF16)
    r1 = da - d1.astype(F32)
    d2 = r1.astype(BF16)
    d3 = (r1 - d2.astype(F32)).astype(BF16)
    lower = jnp.where(causal, 1.0, 0.0).astype(BF16)
    upper = jnp.where((col > row) & same_seq, 1.0, 0.0).astype(BF16)
    cum = _dot(lower, d1) + _dot(lower, d2) + _dot(lower, d3)
    rest = _dot(upper, d1) + _dot(upper, d2) + _dot(upper, d3)
    return {
        "causal": causal,
        "cum": cum,
        "cum_t": cum.T,
        "dt_t": dt.T,
        "ecum": jnp.exp(cum),
        "tail": jnp.exp(rest) * dt,
    }


_PRE_KEYS = ("cum", "cum_t", "dt_t", "ecum", "tail")


def _ssd_group(g, pre, z_ref, xc_ref, seg_len, n_seq, h_in, h_out, dskip_ref, normw_ref, after_first_dot=None):
    lane = lax.broadcasted_iota(jnp.int32, (TILE, LANES), 1)
    lane_g = lax.broadcasted_iota(jnp.int32, (TILE, GROUP_W), 1)
    cum, cum_t, dt_t = pre["cum"], pre["cum_t"], pre["dt_t"]
    c0 = g * XBC_GROUP_W
    gs = slice(g * GROUP_W, (g + 1) * GROUP_W)
    xs = xc_ref[:, c0:c0 + GROUP_W]
    zg = z_ref[:, gs]
    bm16 = xc_ref[:, c0 + GROUP_W:c0 + GROUP_W + N_STATE].astype(BF16)
    cm16 = xc_ref[:, c0 + GROUP_W + N_STATE:c0 + XBC_GROUP_W].astype(BF16)
    cb = _dot_nt(cm16, bm16)
    extra = after_first_dot() if after_first_dot is not None else None
    seqs = [(s, slice(s * seg_len, (s + 1) * seg_len), (s + 1) * seg_len - 1) for s in range(n_seq)]
    ystate = [_dot_nt(cm16[rs], h_in(s, g).astype(BF16)) for s, rs, _ in seqs]
    if n_seq * seg_len < TILE:
        ystate.append(jnp.zeros((TILE - n_seq * seg_len, GROUP_W), F32))
    ys = ystate[0] if len(ystate) == 1 else jnp.concatenate(ystate, axis=0)
    ws, xms = [], []
    for i in range(HEADS_PER_GROUP):
        h = HEADS_PER_GROUP * g + i
        seg = cum[:, h:h + 1] - cum_t[h:h + 1, :]
        decay = jnp.exp(jnp.where(pre["causal"], seg, -jnp.inf))
        ws.append((cb * decay * dt_t[h:h + 1, :]).astype(BF16))
        xms.append(jnp.where((lane_g // HEAD_DIM) == i, xs, 0.0).astype(BF16))
    yg = _dot(jnp.concatenate(ws, axis=1), jnp.concatenate(xms, axis=0))
    yg = yg + ys * _expand_heads(pre["ecum"], g, lane)
    xw16 = (xs * _expand_heads(pre["tail"], g, lane)).astype(BF16)

    def update_state():
        for s, rs, last in seqs:
            upd = _dot_tn(xw16[rs], bm16[rs])
            dec = jnp.concatenate(
                [jnp.broadcast_to(jnp.exp(cum_t[HEADS_PER_GROUP * g + i:HEADS_PER_GROUP * g + i + 1, last:last + 1]),
                                  (HEAD_DIM, N_STATE)) for i in range(HEADS_PER_GROUP)], axis=0)
            h_out(s, g, h_in(s, g) * dec + upd)

    v = (yg + dskip_ref[:, gs] * xs) * _silu(zg)
    v = v * lax.rsqrt(jnp.mean(v * v, axis=-1, keepdims=True) + EPS) * normw_ref[:, gs]
    return v.astype(BF16), update_state, extra


def _out_proj_group(g, v16, wout_ref):
    return _dot(v16, wout_ref[g * GROUP_W:(g + 1) * GROUP_W, :])


def _ssm_prompt_kernel(x_ref, hist_ref, h0_ref, nw_ref, wzx_ref, wdt_ref, cw_ref, cb_ref, dtb_ref, alog_ref,
                       dskip_ref, normw_ref, wout_ref, yb_ref, ns_ref, hs_ref, z_ref, cfull_ref, xc_ref, pre_ref, *,
                       n_inert):
    s = pl.program_id(1)
    n_tiles = pl.num_programs(1) - 1
    nq, _, d = x_ref.shape
    seqs = range(nq)

    @pl.when(s == 0)
    def _():
        for q in seqs:
            cfull_ref[q, 0:SCONV_PAD, :] = hist_ref[...]
            hs_ref[q] = h0_ref[...]
        z_ref[...] = jnp.zeros(z_ref.shape, F32)
        xc_ref[...] = jnp.zeros(xc_ref.shape, F32)
        pre_ref[...] = jnp.zeros(pre_ref.shape, F32)

    xn = _rms(x_ref[...].reshape(nq * TILE, d), nw_ref[...]).astype(BF16)
    row = lax.broadcasted_iota(jnp.int32, (TILE, TILE), 0)
    col = lax.broadcasted_iota(jnp.int32, (TILE, TILE), 1)
    pres = []
    for q in seqs:
        pre = {k: pre_ref[q, i] for i, k in enumerate(_PRE_KEYS)}
        pre["causal"] = col <= row
        pres.append(pre)
    rowv = lax.broadcasted_iota(jnp.int32, (TILE, 1), 0)
    valid = jnp.where((s == 0) & (rowv < n_inert), 0.0, 1.0)
    dtr = _dot(xn, wdt_ref[...])

    def state_access(q):
        def h_in(_, g):
            return hs_ref[q, HEADS_PER_GROUP * g:HEADS_PER_GROUP * (g + 1)].reshape(GROUP_W, N_STATE)

        def h_out(_, g, v):
            hs_ref[q, HEADS_PER_GROUP * g:HEADS_PER_GROUP * (g + 1)] = v.reshape(HEADS_PER_GROUP, HEAD_DIM, N_STATE)

        return h_in, h_out

    acc = [jnp.zeros((TILE, d), F32) for _ in seqs]
    pending = []

    def finish_oldest():
        v16, update, q, g = pending.pop(0)
        acc[q] = acc[q] + _out_proj_group(g, v16, wout_ref)
        update()

    for g in range(N_GROUPS):
        projected = {}
        for q in seqs:
            def independent_dots(g=g, q=q, projected=projected):
                if q == 0:
                    projected["zx"] = _dot(xn, wzx_ref[:, g * ZX_GROUP_W:(g + 1) * ZX_GROUP_W])
                zx = projected["zx"][q * TILE:(q + 1) * TILE]
                z_ref[q, :, g * GROUP_W:(g + 1) * GROUP_W] = zx[:, :GROUP_W]
                cfull_ref[q, SCONV_PAD:SCONV_PAD + TILE, g * XBC_GROUP_W:(g + 1) * XBC_GROUP_W] = zx[:, GROUP_W:]
                if len(pending) >= nq:
                    finish_oldest()

            h_in, h_out = state_access(q)
            v16, update, _ = _ssd_group(g, pres[q], z_ref.at[q], xc_ref.at[q], TILE, 1, h_in, h_out, dskip_ref,
                                        normw_ref, independent_dots)

            def emit(cs, v, q=q):
                xc_ref[q, :, cs] = _silu(v)

            _dwconv(cfull_ref.at[q], SCONV_PAD - (SSM_K - 1), TILE, cw_ref, cb_ref, SSM_K, XBC_GROUP_W, emit,
                    col0=g * XBC_GROUP_W)
            pending.append((v16, update, q, g))
    while pending:
        finish_oldest()
    for q in seqs:
        yb_ref[q] = acc[q]
        pre_next = _ssd_prelude(dtr[q * TILE:(q + 1) * TILE], valid, TILE, dtb_ref, alog_ref)
        for i, k in enumerate(_PRE_KEYS):
            pre_ref[q, i] = pre_next[k]

    @pl.when(s == n_tiles - 1)
    def _():
        for q in seqs:
            ns_ref[q] = cfull_ref[q, pl.ds(SCONV_PAD + TILE - (SSM_K - 1), SSM_K - 1), :]

    for q in seqs:
        cfull_ref[q, 0:SCONV_PAD, :] = cfull_ref[q, TILE:TILE + SCONV_PAD, :]


def _ssm_sample_proj_kernel(x_ref, st_ref, nw_ref, wzx_ref, wdt_ref, cw_ref, cb_ref,
                            z_ref, xc_ref, dtr_ref, ns_ref, xbc_ref, cs_ref, *, n_new):
    conv_dim = xc_ref.shape[-1]
    d_inner = z_ref.shape[-1]
    n_seq = TILE // SAMPLE_ROWS
    xn = _rms(x_ref[...], nw_ref[...]).astype(BF16)
    for g in range(N_GROUPS):
        _project_group(g, xn, wzx_ref, z_ref, xbc_ref, 0)
    dtr_ref[...] = _dot(xn, wdt_ref[...])
    col_pairs = _xbc_col_pairs(d_inner)

    def seq_body(s, carry):
        r0 = pl.multiple_of(s * SAMPLE_ROWS, SAMPLE_ROWS)
        hist = st_ref[s]
        for ours, theirs in col_pairs:
            cs_ref[SCONV_PAD - (SSM_K - 1):SCONV_PAD, ours] = hist[:, theirs]
        cs_ref[SCONV_PAD:SCONV_PAD + SAMPLE_ROWS, :] = xbc_ref[pl.ds(r0, SAMPLE_ROWS), :]

        def emit(cs, v):
            xc_ref[pl.ds(r0, SAMPLE_ROWS), cs] = _silu(v)

        _dwconv(cs_ref, SCONV_PAD - (SSM_K - 1), SAMPLE_ROWS, cw_ref, cb_ref, SSM_K, conv_dim, emit)
        for ours, theirs in col_pairs:
            ns_ref[s, :, theirs] = cs_ref[pl.ds(SCONV_PAD + n_new - (SSM_K - 1), SSM_K - 1), ours]
        return carry

    lax.fori_loop(0, n_seq, seq_body, 0)


def _ssm_sample_ssd_kernel(z_in, xc_in, dtr_in, hin_ref, dtb_ref, alog_ref, dskip_ref, normw_ref, wout_ref,
                           yb_ref, hout_ref, z_ref, xc_ref, dtr_ref, *, n_new, n_seq):
    rows = n_seq * SAMPLE_ROWS
    for src, dst in ((z_in, z_ref), (xc_in, xc_ref), (dtr_in, dtr_ref)):
        dst[0:rows, :] = src[...]
        dst[rows:TILE, :] = jnp.zeros((TILE - rows, dst.shape[-1]), F32)
    rowv = lax.broadcasted_iota(jnp.int32, (TILE, 1), 0)
    valid = jnp.where((rowv < rows) & ((rowv % SAMPLE_ROWS) < n_new), 1.0, 0.0)

    def h_in(s, g):
        return hin_ref[s, HEADS_PER_GROUP * g:HEADS_PER_GROUP * (g + 1)].reshape(GROUP_W, N_STATE)

    def h_out(s, g, v):
        hout_ref[s, HEADS_PER_GROUP * g:HEADS_PER_GROUP * (g + 1)] = v.reshape(HEADS_PER_GROUP, HEAD_DIM, N_STATE)

    pre = _ssd_prelude(dtr_ref[...], valid, SAMPLE_ROWS, dtb_ref, alog_ref)
    acc = jnp.zeros((TILE, yb_ref.shape[-1]), F32)
    for g in range(N_GROUPS):
        v16, update, _ = _ssd_group(g, pre, z_ref, xc_ref, SAMPLE_ROWS, n_seq, h_in, h_out, dskip_ref, normw_ref)
        acc = acc + _out_proj_group(g, v16, wout_ref)
        update()
    yb_ref[...] = acc[0:rows, :]


def _ssm_prompt(x, hist, h0, p, n_inert):
    b, l, d = x.shape
    nb = l // TILE
    nq = 2 if b % 2 == 0 else 1
    d_inner = p["ssm_norm_w"].shape[-1]
    conv_dim = p["ssm_conv_b"].shape[-1]
    n_heads = d_inner // HEAD_DIM
    xmap = lambda i, j: (i, jnp.minimum(j, nb - 1), 0)
    ymap = lambda i, j: (i, jnp.maximum(j - 1, 0), 0)
    weights = [hist, h0, p["norm_w"], p["w_zx"], p["w_dt"], p["ssm_conv_w"], p["ssm_conv_b"], p["dt_bias"],
               p["a_log"], p["d_skip"], p["ssm_norm_w"], p["ssm_out_w"]]
    wspecs = [_const_spec(w.shape) for w in weights]
    return pl.pallas_call(
        functools.partial(_ssm_prompt_kernel, n_inert=n_inert),
        grid=(b // nq, nb + 1),
        in_specs=[pl.BlockSpec((nq, TILE, d), xmap)] + wspecs,
        out_specs=[pl.BlockSpec((nq, TILE, d), ymap),
                   pl.BlockSpec((nq, SSM_K - 1, conv_dim), lambda i, j: (i, 0, 0)),
                   pl.BlockSpec((nq, n_heads, HEAD_DIM, N_STATE), lambda i, j: (i, 0, 0, 0))],
        out_shape=[jax.ShapeDtypeStruct((b, l, d), F32), jax.ShapeDtypeStruct((b, SSM_K - 1, conv_dim), F32),
                   jax.ShapeDtypeStruct((b, n_heads, HEAD_DIM, N_STATE), F32)],
        scratch_shapes=[pltpu.VMEM((nq, TILE, d_inner), F32), pltpu.VMEM((nq, SCONV_PAD + TILE, conv_dim), F32),
                        pltpu.VMEM((nq, TILE, conv_dim), F32), pltpu.VMEM((nq, len(_PRE_KEYS), TILE, LANES), F32)],
        compiler_params=pltpu.CompilerParams(dimension_semantics=("arbitrary", "arbitrary"),
                                             vmem_limit_bytes=VMEM_LIMIT),
        name="ssm_prompt",
    )(x, *weights)


def _ssm_sample(x_rows, sconv, h0, p, n_new):
    rows, d = x_rows.shape
    d_inner = p["ssm_norm_w"].shape[-1]
    conv_dim = p["ssm_conv_b"].shape[-1]
    n_heads = d_inner // HEAD_DIM
    n_seq_proj = TILE // SAMPLE_ROWS
    weights = [p["norm_w"], p["w_zx"], p["w_dt"], p["ssm_conv_w"], p["ssm_conv_b"]]
    z, xc, dtr, new_sconv = pl.pallas_call(
        functools.partial(_ssm_sample_proj_kernel, n_new=n_new),
        grid=(rows // TILE,),
        in_specs=[pl.BlockSpec((TILE, d), lambda i: (i, 0)),
                  pl.BlockSpec((n_seq_proj, SSM_K - 1, conv_dim), lambda i: (i, 0, 0))]
        + [_const_spec(w.shape) for w in weights],
        out_specs=[pl.BlockSpec((TILE, d_inner), lambda i: (i, 0)), pl.BlockSpec((TILE, conv_dim), lambda i: (i, 0)),
                   pl.BlockSpec((TILE, LANES), lambda i: (i, 0)),
                   pl.BlockSpec((n_seq_proj, SSM_K - 1, conv_dim), lambda i: (i, 0, 0))],
        out_shape=[jax.ShapeDtypeStruct((rows, d_inner), F32), jax.ShapeDtypeStruct((rows, conv_dim), F32),
                   jax.ShapeDtypeStruct((rows, LANES), F32),
                   jax.ShapeDtypeStruct((rows // SAMPLE_ROWS, SSM_K - 1, conv_dim), F32)],
        scratch_shapes=[pltpu.VMEM((TILE, conv_dim), F32), pltpu.VMEM((SCONV_PAD + SAMPLE_ROWS, conv_dim), F32)],
        compiler_params=pltpu.CompilerParams(dimension_semantics=("arbitrary",), vmem_limit_bytes=VMEM_LIMIT),
        name="ssm_sample_proj",
    )(x_rows, sconv, *weights)

    n_seq = 8 if rows % (8 * SAMPLE_ROWS) == 0 else 4
    r = n_seq * SAMPLE_ROWS
    weights = [p["dt_bias"], p["a_log"], p["d_skip"], p["ssm_norm_w"], p["ssm_out_w"]]
    yb, h_new = pl.pallas_call(
        functools.partial(_ssm_sample_ssd_kernel, n_new=n_new, n_seq=n_seq),
        grid=(rows // r,),
        in_specs=[pl.BlockSpec((r, d_inner), lambda i: (i, 0)), pl.BlockSpec((r, conv_dim), lambda i: (i, 0)),
                  pl.BlockSpec((r, LANES), lambda i: (i, 0)),
                  pl.BlockSpec((n_seq, n_heads, HEAD_DIM, N_STATE), lambda i: (i, 0, 0, 0))]
        + [_const_spec(w.shape) for w in weights],
        out_specs=[pl.BlockSpec((r, d), lambda i: (i, 0)),
                   pl.BlockSpec((n_seq, n_heads, HEAD_DIM, N_STATE), lambda i: (i, 0, 0, 0))],
        out_shape=[jax.ShapeDtypeStruct((rows, d), F32), jax.ShapeDtypeStruct(h0.shape, F32)],
        scratch_shapes=[pltpu.VMEM((TILE, d_inner), F32), pltpu.VMEM((TILE, conv_dim), F32),
                        pltpu.VMEM((TILE, LANES), F32)],
        compiler_params=pltpu.CompilerParams(dimension_semantics=("arbitrary",), vmem_limit_bytes=VMEM_LIMIT),
        name="ssm_sample_ssd",
    )(z, xc, dtr, h0, *weights)
    return yb, new_sconv, h_new


def _merge_kernel(x_ref, ya_ref, yb_ref, nw_ref, wg_ref, gb_ref, wo_ref, nfw_ref, rwh_ref, rwl_ref, rb_ref,
                  x1_ref, xn2_ref, comb_ref):
    d = x_ref.shape[-1]
    half = x_ref.shape[0] // 2
    parts = [slice(0, half), slice(half, 2 * half)]
    xs = [x_ref[rs, :] for rs in parts]
    xns = [_rms(x, nw_ref[...]).astype(BF16) for x in xs]
    glog = [_dot(xn, wg_ref[...]) for xn in xns]
    merged = []
    for rs, gl_ in zip(parts, glog):
        gates = _sigmoid(gl_ + gb_ref[...])
        merged.append((gates[:, :d] * ya_ref[rs, :] + gates[:, d:] * yb_ref[rs, :]).astype(BF16))
    outs = [_dot(m, wo_ref[...]) for m in merged]
    logit_parts = []
    for rs, x, o in zip(parts, xs, outs):
        x1 = x + o
        x1_ref[rs, :] = x1
        xn2 = _rms(x1, nfw_ref[...])
        x_hi = xn2.astype(BF16)
        xn2_ref[rs, :] = x_hi
        x_lo = (xn2 - x_hi.astype(F32)).astype(BF16)
        hi_terms = _dot(x_hi, jnp.concatenate([rwh_ref[...], rwl_ref[...]], axis=1))
        logit_parts.append(hi_terms[:, :LANES] + _dot(x_lo, rwh_ref[...]) + hi_terms[:, LANES:])

    logits = jnp.concatenate(logit_parts, axis=0) + rb_ref[...]
    rows = logits.shape[0]
    lane = lax.broadcasted_iota(jnp.int32, (rows, LANES), 1)
    neg = -jnp.inf
    is_g = (lane >= N_EXPERTS) & (lane < N_EXPERTS + N_EXPERT_GROUPS)
    gl = jnp.where(is_g, logits, neg)
    gmax = jnp.max(gl, axis=-1, keepdims=True)
    gsel = jnp.min(jnp.where(gl == gmax, lane, LANES), axis=-1, keepdims=True) - N_EXPERTS
    gprob = 1.0 / jnp.sum(jnp.exp(gl - gmax), axis=-1, keepdims=True)
    el = jnp.where((lane < N_EXPERTS) & ((lane // EXPERTS_PER_GROUP) == gsel), logits, neg)
    m1 = jnp.max(el, axis=-1, keepdims=True)
    i1 = jnp.min(jnp.where(el == m1, lane, LANES), axis=-1, keepdims=True)
    el2 = jnp.where(lane == i1, neg, el)
    m2 = jnp.max(el2, axis=-1, keepdims=True)
    i2 = jnp.min(jnp.where(el2 == m2, lane, LANES), axis=-1, keepdims=True)
    e2 = jnp.exp(m2 - m1)
    den = 1.0 + e2
    comb = jnp.where(lane == i1, (1.0 / den) * gprob, jnp.where(lane == i2, (e2 / den) * gprob, 0.0))
    comb_ref[...] = jnp.where(lane == GSEL_LANE, gsel.astype(F32), comb)


def _merge(x, ya, yb, p, tm):
    rows, d = x.shape
    weights = [p["norm_w"], p["w_gate"], p["gate_b"], p["w_o"], p["norm_ffn_w"], p["router_w_hi"],
               p["router_w_lo"], p["router_b"]]
    row_spec = pl.BlockSpec((tm, d), lambda i: (i, 0))
    return pl.pallas_call(
        _merge_kernel,
        grid=(rows // tm,),
        in_specs=[row_spec, row_spec, row_spec] + [_const_spec(w.shape) for w in weights],
        out_specs=[row_spec, row_spec, pl.BlockSpec((tm, LANES), lambda i: (i, 0))],
        out_shape=[jax.ShapeDtypeStruct((rows, d), F32), jax.ShapeDtypeStruct((rows, d), BF16),
                   jax.ShapeDtypeStruct((rows, LANES), F32)],
        compiler_params=pltpu.CompilerParams(dimension_semantics=("arbitrary",), vmem_limit_bytes=VMEM_LIMIT),
        name="merge_router",
    )(x, ya, yb, *weights)


def _moe_kernel(xn_ref, comb_ref, x1_ref, w1_ref, w3_ref, w2_ref, nw_ref, y_ref, rt_ref, *, chunk):
    g = pl.program_id(1)
    tm = xn_ref.shape[0]
    comb = comb_ref[...]
    lane = lax.broadcasted_iota(jnp.int32, (tm, LANES), 1)
    gsel = comb[:, GSEL_LANE:GSEL_LANE + 1]

    @pl.when(g == 0)
    def _():
        row = lax.broadcasted_iota(jnp.int32, (tm, LANES), 0)
        onehot = jnp.where(lane.astype(F32) == gsel, 1.0, 0.0)
        incl = _cumsum_rows(onehot, row)
        rank = jnp.sum(onehot * (incl - onehot), axis=-1, keepdims=True)
        packed = jnp.where(lane == 0, rank, jnp.where(lane == 1, gsel, 0.0))
        rt_ref[0:SUBLANES, :] = packed.T[0:SUBLANES, :]
        rt_ref[SUBLANES:2 * SUBLANES, 0:LANES] = jnp.broadcast_to(incl[tm - 1:tm, :], (SUBLANES, LANES))
        y_ref[...] = jnp.zeros(y_ref.shape, F32)

    gf = g.astype(F32)
    lane_row = lax.broadcasted_iota(jnp.int32, (1, LANES), 1)
    count = jnp.sum(jnp.where(lane_row == g, rt_ref[SUBLANES:SUBLANES + 1, 0:LANES], 0.0), axis=-1, keepdims=True)
    n_chunks = (count[0, 0].astype(jnp.int32) + chunk - 1) // chunk
    w_hi = jnp.where(lane < N_EXPERTS, comb, 0.0).astype(BF16)
    w_lo = (jnp.where(lane < N_EXPERTS, comb, 0.0) - w_hi.astype(F32)).astype(BF16)
    w_hl = jnp.concatenate([w_hi, w_lo], axis=1)
    rowi = lax.broadcasted_iota(jnp.int32, (chunk, tm), 0).astype(F32)
    lane_c = lax.broadcasted_iota(jnp.int32, (chunk, LANES), 1)

    def chunk_body(c, carry):
        base = (c * chunk).astype(F32)
        sel = jnp.where((rt_ref[1:2, :] == gf) & ((rt_ref[0:1, :] - base) == rowi), 1.0, 0.0).astype(BF16)
        xc = _dot(sel, xn_ref[...]).astype(BF16)
        wc2 = _dot(sel, w_hl)
        wc = wc2[:, :LANES] + wc2[:, LANES:]
        acts = []
        for e in range(EXPERTS_PER_GROUP):
            ce = jnp.sum(jnp.where(lane_c == g * EXPERTS_PER_GROUP + e, wc, 0.0), axis=-1, keepdims=True)
            acts.append((_silu(_dot(xc, w1_ref[e])) * _dot(xc, w3_ref[e]) * ce).astype(BF16))
        y = _dot(jnp.concatenate(acts, axis=1), w2_ref[...])
        y_ref[...] += _dot_tn(sel, y.astype(BF16))
        return carry

    lax.fori_loop(0, n_chunks, chunk_body, 0)

    @pl.when(g == pl.num_programs(1) - 1)
    def _():
        y_ref[...] = _rms(x1_ref[...] + y_ref[...], nw_ref[...])


def _moe(xn2, comb, x1, p, tm):
    rows, d = x1.shape
    n_e, _, d_e = p["exp_w1"].shape
    n_g = n_e // EXPERTS_PER_GROUP
    w2 = p["exp_w2"].reshape(n_g, EXPERTS_PER_GROUP * d_e, d)
    pack = 2 * SUBLANES
    chunk = -(-(tm * 9 // (8 * n_g)) // pack) * pack
    row = lambda i, g: (i, 0)
    return pl.pallas_call(
        functools.partial(_moe_kernel, chunk=chunk),
        grid=(rows // tm, n_g),
        in_specs=[pl.BlockSpec((tm, d), row), pl.BlockSpec((tm, LANES), row),
                  pl.BlockSpec((tm, d), row),
                  pl.BlockSpec((EXPERTS_PER_GROUP, d, d_e), lambda i, g: (g, 0, 0)),
                  pl.BlockSpec((EXPERTS_PER_GROUP, d, d_e), lambda i, g: (g, 0, 0)),
                  pl.BlockSpec((None, EXPERTS_PER_GROUP * d_e, d), lambda i, g: (g, 0, 0)),
                  _const_spec((1, d))],
        out_specs=pl.BlockSpec((tm, d), row),
        out_shape=jax.ShapeDtypeStruct((rows, d), F32),
        scratch_shapes=[pltpu.VMEM((2 * SUBLANES, tm), F32)],
        compiler_params=pltpu.CompilerParams(dimension_semantics=("arbitrary", "arbitrary"),
                                             vmem_limit_bytes=VMEM_LIMIT),
        name="moe",
    )(xn2, comb, x1, p["exp_w1"], p["exp_w3"], w2, p["norm_final_w"])


def _group_xbc_cols(v, d_inner):
    lead = v.shape[:-1]
    xs = v[..., :d_inner].reshape(lead + (N_GROUPS, GROUP_W))
    bm = v[..., d_inner:d_inner + N_GROUPS * N_STATE].reshape(lead + (N_GROUPS, N_STATE))
    cm = v[..., d_inner + N_GROUPS * N_STATE:].reshape(lead + (N_GROUPS, N_STATE))
    return jnp.concatenate([xs, bm, cm], axis=-1).reshape(lead + (-1,))


def _ungroup_xbc_cols(v, d_inner):
    lead = v.shape[:-1]
    v = v.reshape(lead + (N_GROUPS, XBC_GROUP_W))
    parts = [v[..., :GROUP_W], v[..., GROUP_W:GROUP_W + N_STATE], v[..., GROUP_W + N_STATE:]]
    return jnp.concatenate([q.reshape(lead + (-1,)) for q in parts], axis=-1)


def _prep_params(norm_mix_w, w_in, conf_dw_w, conf_dw_b, conf_ln_g, conf_ln_b, conf_out_w, conf_out_b, ssm_conv_w,
                 ssm_conv_b, dt_bias, a_log, d_skip, ssm_norm_w, ssm_out_w, gate_b, w_o, norm_ffn_w,
                 router_group_w, router_group_b, router_expert_w, router_expert_b, exp_w1, exp_w3, exp_w2,
                 norm_final_w):
    d = norm_mix_w.shape[-1]
    d_inner = ssm_norm_w.shape[-1]
    conv_dim = ssm_conv_b.shape[-1]
    n_heads = dt_bias.shape[-1]
    s2 = 2 * d
    s3 = s2 + d_inner
    s4 = s3 + conv_dim
    s5 = s4 + n_heads
    w = w_in[0]
    row = lambda v: v.reshape(1, -1).astype(F32)
    pad_lanes = lambda v: jnp.pad(v, ((0, 0), (0, LANES - v.shape[-1])))
    router_w = pad_lanes(jnp.concatenate([router_expert_w[0], router_group_w[0]], axis=1))
    router_w_hi = router_w.astype(BF16)
    return {
        "norm_w": row(norm_mix_w[0]),
        "w_vg": w[:, :s2].astype(BF16),
        "w_zx": jnp.concatenate([w[:, s2:s3].reshape(d, N_GROUPS, GROUP_W),
                                 _group_xbc_cols(w[:, s3:s4], d_inner).reshape(d, N_GROUPS, XBC_GROUP_W)],
                                axis=-1).reshape(d, -1).astype(BF16),
        "w_dt": pad_lanes(w[:, s4:s5]).astype(BF16),
        "w_gate": w[:, s5:].astype(BF16),
        "conf_dw_w": jnp.pad(conf_dw_w[0], ((0, CONF_PAD - CONF_K), (0, 0))),
        "conf_dw_b": row(conf_dw_b[0]),
        "conf_ln_g": row(conf_ln_g[0]),
        "conf_ln_b": row(conf_ln_b[0]),
        "conf_out_w": conf_out_w[0].astype(BF16),
        "conf_out_b": row(conf_out_b[0]),
        "ssm_conv_w": jnp.pad(_group_xbc_cols(ssm_conv_w[0], d_inner), ((0, SUBLANES - SSM_K), (0, 0))),
        "ssm_conv_b": row(_group_xbc_cols(ssm_conv_b[0], d_inner)),
        "dt_bias": pad_lanes(row(dt_bias[0])),
        "a_log": pad_lanes(row(a_log[0])),
        "d_skip": row(jnp.repeat(d_skip[0], HEAD_DIM)),
        "ssm_norm_w": row(ssm_norm_w[0]),
        "ssm_out_w": ssm_out_w[0].astype(BF16),
        "gate_b": row(gate_b[0]),
        "w_o": w_o[0].astype(BF16),
        "norm_ffn_w": row(norm_ffn_w[0]),
        "router_w_hi": router_w_hi,
        "router_w_lo": (router_w - router_w_hi.astype(F32)).astype(BF16),
        "router_b": pad_lanes(row(jnp.concatenate([router_expert_b[0], router_group_b[0]]))),
        "exp_w1": exp_w1[0].astype(BF16),
        "exp_w3": exp_w3[0].astype(BF16),
        "exp_w2": exp_w2[0].astype(BF16),
        "norm_final_w": row(norm_final_w),
    }


def kernel(x_prompt, x_sample, state_conf_conv, state_ssm_conv, state_ssm, meta_tokens, norm_mix_w, w_in, conf_dw_w, conf_dw_b, conf_ln_g, conf_ln_b, conf_out_w, conf_out_b, ssm_conv_w, ssm_conv_b, dt_bias, a_log, d_skip, ssm_norm_w, ssm_out_w, gate_b, w_o, norm_ffn_w, router_group_w, router_group_b, router_expert_w, router_expert_b, exp_w1, exp_w3, exp_w2, norm_final_w):
    assert norm_mix_w.shape[0] == 1, "single-layer trunk"
    b, l, d = x_prompt.shape
    nb, n_new, _ = x_sample.shape
    assert l % TILE == 0 and n_new <= SAMPLE_ROWS and (nb * SAMPLE_ROWS) % TILE == 0
    p = _prep_params(norm_mix_w, w_in, conf_dw_w, conf_dw_b, conf_ln_g, conf_ln_b, conf_out_w, conf_out_b,
                     ssm_conv_w, ssm_conv_b, dt_bias, a_log, d_skip, ssm_norm_w, ssm_out_w, gate_b, w_o,
                     norm_ffn_w, router_group_w, router_group_b, router_expert_w, router_expert_b, exp_w1, exp_w3,
                     exp_w2, norm_final_w)

    assert meta_tokens.shape[0] == N_META
    ya_p, new_conf_p = _conf_prompt(meta_tokens.astype(F32), x_prompt, p)
    d_inner = ssm_norm_w.shape[-1]
    conv_dim = ssm_conv_b.shape[-1]
    head = jnp.concatenate([jnp.zeros((TILE - N_META, d), F32), meta_tokens.astype(F32)], axis=0)
    _, meta_sconv, meta_ssm = _ssm_prompt(head[None], jnp.zeros((SCONV_PAD, conv_dim), F32),
                                          jnp.zeros(state_ssm.shape[2:], F32), p, TILE - N_META)
    meta_hist = jnp.pad(meta_sconv[0], ((SCONV_PAD - (SSM_K - 1), 0), (0, 0)))
    yb_p, new_sconv_p, new_ssm_p = _ssm_prompt(x_prompt, meta_hist, meta_ssm[0], p, 0)

    xs_rows = jnp.pad(x_sample, ((0, 0), (0, SAMPLE_ROWS - n_new), (0, 0))).reshape(nb * SAMPLE_ROWS, d)
    ya_s, new_conf_s = _conf_sample(xs_rows, state_conf_conv[0], p, n_new)
    yb_s, new_sconv_s, new_ssm_s = _ssm_sample(xs_rows, state_ssm_conv[0], state_ssm[0], p, n_new)
    unpad = lambda v: v.reshape(nb, SAMPLE_ROWS, d)[:, :n_new].reshape(nb * n_new, d)

    outs = []
    for x, ya, yb in ((x_prompt.reshape(b * l, d), ya_p.reshape(b * l, d), yb_p.reshape(b * l, d)),
                      (x_sample.reshape(nb * n_new, d), unpad(ya_s), unpad(yb_s))):
        tm = next(t for t in (1024, 512, TILE) if x.shape[0] % t == 0)
        x1, xn2, comb = _merge(x, ya, yb, p, min(tm, 512))
        outs.append(_moe(xn2, comb, x1, p, tm))
    y_prompt = outs[0].reshape(b, l, d)
    y_sample = outs[1].reshape(nb, n_new, d)
    return (y_prompt, y_sample, new_conf_p[None], _ungroup_xbc_cols(new_sconv_p, d_inner)[None], new_ssm_p[None],
            new_conf_s[None], new_sconv_s[None], new_ssm_s[None])
```
